```python
import jax, jax.numpy as jnp
from jax import lax
import numpy as np

D_MODEL = 2048
BATCH = 1
SEQ = 8192
DEPTH = 4

GRID_W = 64
CTX_LEN = 256
RMS_EPS = 1e-6
N_BRANCH = 3
BRANCH_WIDTH = D_MODEL // 2

A_HEAD_DIM = 64
A_HEADS = BRANCH_WIDTH // A_HEAD_DIM
A_WIDTH = A_HEADS * A_HEAD_DIM
LORA_W = 96
LORA_A = 96
LORA_G = 256
GN_EPS = 64e-5
RWKV_COLS = 3 * A_WIDTH + 2 * LORA_W + 2 * LORA_A + LORA_G

QK_NOPE = 128
QK_ROPE = 64
V_DIM = 128
B_HEADS = BRANCH_WIDTH // V_DIM
Q_LORA = 448
KV_LORA = 128
MLA_COLS = Q_LORA + KV_LORA + QK_ROPE
MLA_SCALE = (QK_NOPE + QK_ROPE) ** -0.5
ROPE_BASE = 10000.0

C_HEAD_DIM = 64
C_HEADS = BRANCH_WIDTH // C_HEAD_DIM
C_WIDTH = C_HEADS * C_HEAD_DIM
WIN_ROWS = 8
WIN_COLS = 16
NA_COLS = 3 * C_WIDTH
NA_SCALE = C_HEAD_DIM ** -0.5

GATE_COLS = N_BRANCH * D_MODEL
IN_COLS = RWKV_COLS + MLA_COLS + NA_COLS + GATE_COLS
D_FF = 4 * D_MODEL
Q_BLOCK = 128

kernel_name = "hybrid_rwkv7_mla_natten_dit"


def split_cols(u, sizes):
    out, start = [], 0
    for s in sizes:
        out.append(u[..., start:start + s])
        start += s
    return out


def rms_norm(x, g):
    xf = x.astype(jnp.float32)
    y = xf * lax.rsqrt(jnp.mean(xf * xf, axis=-1, keepdims=True) + RMS_EPS)
    return (y * g.astype(jnp.float32)).astype(x.dtype)


def modulate(h, shift, scale):
    return h * (1.0 + scale) + shift


def sq_relu_mlp(h, w1, w2):
    return jnp.square(jax.nn.relu(h @ w1)) @ w2


def axial_angles(seq):
    t = jnp.arange(seq)
    row = (t // GRID_W).astype(jnp.float32)
    col = (t % GRID_W).astype(jnp.float32)
    n_freq = QK_ROPE // 4
    inv = ROPE_BASE ** (-jnp.arange(n_freq, dtype=jnp.float32) / n_freq)
    return row[:, None] * inv[None, :], col[:, None] * inv[None, :]


def rotate_half(x, ang):
    n = x.shape[-1] // 2
    cos, sin = jnp.cos(ang).astype(x.dtype), jnp.sin(ang).astype(x.dtype)
    x1, x2 = x[..., :n], x[..., n:]
    return jnp.concatenate([x1 * cos - x2 * sin, x1 * sin + x2 * cos], axis=-1)


def rope_2d(x, ang_row, ang_col):
    h = QK_ROPE // 2
    return jnp.concatenate([rotate_half(x[..., :h], ang_row), rotate_half(x[..., h:], ang_col)], axis=-1)


def blocked_attention(q, k, v, scale):
    B, Tq, H, dq = q.shape
    nb = Tq // Q_BLOCK
    qb = jnp.moveaxis(q.reshape(B, nb, Q_BLOCK, H, dq), 1, 0)

    def one_block(qi):
        s = jnp.einsum("bqhd,bkhd->bhqk", qi, k).astype(jnp.float32) * scale
        p = jax.nn.softmax(s, axis=-1).astype(v.dtype)
        return jnp.einsum("bhqk,bkhd->bqhd", p, v)

    o = lax.map(one_block, qb)
    return jnp.moveaxis(o, 0, 1).reshape(B, Tq, H * v.shape[-1])


def dw_conv3(u, w):
    up = jnp.pad(u, ((0, 0), (1, 1), (0, 0)))
    return w[0] * up[:, :-2] + w[1] * up[:, 1:-1] + w[2] * up[:, 2:]


def heads_a(t):
    return t.reshape(*t.shape[:-1], A_HEADS, A_HEAD_DIM)


def rwkv_prepare(u, lp):
    B, T, _ = u.shape
    f32 = jnp.float32
    u = dw_conv3(u, lp["rwkv_conv"])
    r, k, v, wd, ad, gd = split_cols(u, (A_WIDTH, A_WIDTH, A_WIDTH, 2 * LORA_W, 2 * LORA_A, LORA_G))
    wd = wd.reshape(B, T, 2, LORA_W)
    ad = ad.reshape(B, T, 2, LORA_A)
    z = (lp["rwkv_w0"] + jnp.einsum("btzl,zlc->btzc", jnp.tanh(wd), lp["rwkv_w_up"])).astype(f32)
    decay = jnp.exp(-jnp.exp(-jax.nn.softplus(-z) - 0.5))
    a = jax.nn.sigmoid((lp["rwkv_a0"] + jnp.einsum("btzl,zlc->btzc", ad, lp["rwkv_a_up"])).astype(f32))
    kf = k.astype(f32)
    kk = heads_a(kf * lp["rwkv_k_k"].astype(f32))
    kk = kk / jnp.maximum(jnp.sqrt(jnp.sum(kk * kk, axis=-1, keepdims=True)), 1e-12)
    k_dir = kf[:, :, None, :] * (1.0 + (a - 1.0) * lp["rwkv_k_a"].astype(f32))
    return {"r": heads_a(r.astype(f32)), "k": heads_a(k_dir), "v": heads_a(v.astype(f32)),
            "kk": kk, "w": heads_a(decay), "a": heads_a(a), "gd": gd}


def rwkv7_scan(s0, p, d, reverse, with_outputs):
    w, k, a = p["w"][:, :, d], p["k"][:, :, d], p["a"][:, :, d]
    kk, v = p["kk"], p["v"]
    xs = (w, k, v, kk, kk * a) + ((p["r"],) if with_outputs else ())
    xs = tuple(jnp.moveaxis(t, 1, 0) for t in xs)

    def step(S, inp):
        w_t, k_t, v_t, kk_t, b_t = inp[:5]
        sa = jnp.einsum("bhvk,bhk->bhv", S, kk_t)
        S = S * w_t[:, :, None, :] - sa[..., None] * b_t[:, :, None, :] + v_t[..., None] * k_t[:, :, None, :]
        y = jnp.einsum("bhvk,bhk->bhv", S, inp[5]) if with_outputs else None
        return S, y

    s_fin, ys = lax.scan(step, s0, xs, reverse=reverse)
    return s_fin, (jnp.moveaxis(ys, 0, 1) if with_outputs else None)


def rwkv_readout(y, p, lp):
    B, T = y.shape[:2]
    mu = jnp.mean(y, axis=-1, keepdims=True)
    var = jnp.mean(jnp.square(y - mu), axis=-1, keepdims=True)
    yn = ((y - mu) * lax.rsqrt(var + GN_EPS)).reshape(B, T, A_WIDTH) * lp["rwkv_ln_g"] + lp["rwkv_ln_b"]
    k_mean = 0.5 * (p["k"][:, :, 0] + p["k"][:, :, 1])
    bonus = (jnp.sum(p["r"] * k_mean * lp["rwkv_r_k"], axis=-1, keepdims=True) * p["v"]).reshape(B, T, A_WIDTH)
    g = jax.nn.sigmoid(p["gd"]) @ lp["rwkv_g_up"]
    return (yn + bonus) * g


def rwkv_mixer(u, u_c, lp, need_ctx_out):
    p, pc = rwkv_prepare(u, lp), rwkv_prepare(u_c, lp)
    s0 = jnp.zeros((u.shape[0], A_HEADS, A_HEAD_DIM, A_HEAD_DIM), jnp.float32)
    ys, ys_c = [], []
    for d, reverse in enumerate((False, True)):
        s_ctx, y_c = rwkv7_scan(s0, pc, d, reverse, need_ctx_out)
        _, y_l = rwkv7_scan(s_ctx, p, d, reverse, True)
        ys.append(y_l)
        ys_c.append(y_c)
    out = rwkv_readout(ys[0] + ys[1], p, lp).astype(u.dtype)
    out_c = rwkv_readout(ys_c[0] + ys_c[1], pc, lp).astype(u.dtype) if need_ctx_out else None
    return out, out_c


def mla_queries(u, lp, ang_row, ang_col, rotate):
    B, T, _ = u.shape
    q = (rms_norm(u[..., :Q_LORA], lp["mla_q_norm_g"]) @ lp["mla_w_uq"]).reshape(B, T, B_HEADS, QK_NOPE + QK_ROPE)
    if rotate:
        q = jnp.concatenate([q[..., :QK_NOPE], rope_2d(q[..., QK_NOPE:], ang_row[:, None], ang_col[:, None])], axis=-1)
    return q


def mla_keys(u, lp, ang_row, ang_col, rotate):
    B, T, _ = u.shape
    _, kvd, kr = split_cols(u, (Q_LORA, KV_LORA, QK_ROPE))
    kv = (rms_norm(kvd, lp["mla_kv_norm_g"]) @ lp["mla_w_ukv"]).reshape(B, T, B_HEADS, QK_NOPE + V_DIM)
    k_pe = rope_2d(kr, ang_row, ang_col) if rotate else kr
    k = jnp.concatenate([kv[..., :QK_NOPE], jnp.broadcast_to(k_pe[:, :, None, :], (B, T, B_HEADS, QK_ROPE))], axis=-1)
    return k, kv[..., QK_NOPE:]


def mla_mixer(u, u_c, lp, ang_row, ang_col, need_ctx_out):
    k_c, v_c = mla_keys(u_c, lp, None, None, False)
    k_l, v_l = mla_keys(u, lp, ang_row, ang_col, True)
    q_l = mla_queries(u, lp, ang_row, ang_col, True)
    k_all = jnp.concatenate([k_c, k_l], axis=1)
    v_all = jnp.concatenate([v_c, v_l], axis=1)
    out = blocked_attention(q_l, k_all, v_all, MLA_SCALE)
    out_c = blocked_attention(mla_queries(u_c, lp, None, None, False), k_c, v_c, MLA_SCALE) if need_ctx_out else None
    return out, out_c


def na_mixer(u, u_c, rpb, need_ctx_out):
    B, T, _ = u.shape
    L = u_c.shape[1]
    rows = T // GRID_W
    wr = min(WIN_ROWS, rows)
    q, k, v = [t.reshape(B, rows, GRID_W, C_HEADS, C_HEAD_DIM) for t in split_cols(u, (C_WIDTH,) * 3)]
    qc, kc, vc = [t.reshape(B, L, C_HEADS, C_HEAD_DIM) for t in split_cols(u_c, (C_WIDTH,) * 3)]
    cols = jnp.arange(GRID_W)
    col_idx = jnp.clip(cols - WIN_COLS // 2, 0, GRID_W - WIN_COLS)[:, None] + jnp.arange(WIN_COLS)
    rpb_cols = rpb[:, :, col_idx - cols[:, None] + WIN_COLS - 1]
    n_win = wr * WIN_COLS

    def one_row(i):
        r0 = jnp.clip(i - wr // 2, 0, rows - wr)
        q_i = lax.dynamic_index_in_dim(q, i, axis=1, keepdims=False)
        k_win = lax.dynamic_slice_in_dim(k, r0, wr, axis=1)[:, :, col_idx]
        v_win = lax.dynamic_slice_in_dim(v, r0, wr, axis=1)[:, :, col_idx]
        bias = jnp.take(rpb_cols, r0 + jnp.arange(wr) - i + WIN_ROWS - 1, axis=1)
        s_win = (jnp.einsum("bqhd,brqchd->bhqrc", q_i, k_win).astype(jnp.float32) * NA_SCALE
                 + jnp.transpose(bias, (0, 2, 1, 3))[None].astype(jnp.float32))
        s_ctx = jnp.einsum("bqhd,bkhd->bhqk", q_i, kc).astype(jnp.float32) * NA_SCALE
        s = jnp.concatenate([s_win.reshape(B, C_HEADS, GRID_W, n_win), s_ctx], axis=-1)
        p = jax.nn.softmax(s, axis=-1).astype(v.dtype)
        p_win = p[..., :n_win].reshape(B, C_HEADS, GRID_W, wr, WIN_COLS)
        return (jnp.einsum("bhqrc,brqchd->bqhd", p_win, v_win)
                + jnp.einsum("bhqk,bkhd->bqhd", p[..., n_win:], vc))

    o = lax.map(one_row, jnp.arange(rows))
    out = jnp.moveaxis(o, 0, 1).reshape(B, T, C_WIDTH)
    out_c = blocked_attention(qc, kc, vc, NA_SCALE) if need_ctx_out else None
    return out, out_c


def merge_branches(ys, ug, w_branch, w_out):
    gates = jax.nn.sigmoid(ug.astype(jnp.float32)).astype(ug.dtype)
    merged = gates[..., :D_MODEL] * (ys[0] @ w_branch[0])
    for i in range(1, N_BRANCH):
        merged = merged + gates[..., i * D_MODEL:(i + 1) * D_MODEL] * (ys[i] @ w_branch[i])
    return merged @ w_out


def token_mixers(h, hc, lp, ang_row, ang_col, need_ctx_out):
    sizes = (RWKV_COLS, MLA_COLS, NA_COLS, GATE_COLS)
    ua, ub, uc, ug = split_cols(h @ lp["w_in"], sizes)
    ua_c, ub_c, uc_c, ug_c = split_cols(hc @ lp["w_in"], sizes)
    ya, ya_c = rwkv_mixer(ua, ua_c, lp, need_ctx_out)
    yb, yb_c = mla_mixer(ub, ub_c, lp, ang_row, ang_col, need_ctx_out)
    yc, yc_c = na_mixer(uc, uc_c, lp["na_rpb"], need_ctx_out)
    out = merge_branches((ya, yb, yc), ug, lp["w_branch"], lp["w_out"])
    out_c = merge_branches((ya_c, yb_c, yc_c), ug_c, lp["w_branch"], lp["w_out"]) if need_ctx_out else None
    return out, out_c


def trunk_layer(x, xc, mod_lat, mod_ctx, lp, ang_row, ang_col, need_ctx_out):
    sh_a, sc_a, g_a, sh_m, sc_m, g_m = jnp.split(mod_lat, 6, axis=-1)
    csh_a, csc_a, cg_a, csh_m, csc_m, cg_m = jnp.split(mod_ctx, 6, axis=-1)
    h = modulate(rms_norm(x, lp["norm_mix_g"]), sh_a, sc_a)
    hc = modulate(rms_norm(xc, lp["norm_mix_g"]), csh_a, csc_a)
    mix, mix_c = token_mixers(h, hc, lp, ang_row, ang_col, need_ctx_out)
    x = x + g_a * mix
    x = x + g_m * sq_relu_mlp(modulate(rms_norm(x, lp["norm_mlp_g"]), sh_m, sc_m), lp["mlp_w1"], lp["mlp_w2"])
    if not need_ctx_out:
        return x, None
    xc = xc + cg_a * mix_c
    xc = xc + cg_m * sq_relu_mlp(modulate(rms_norm(xc, lp["norm_mlp_g"]), csh_m, csc_m), lp["mlp_w1"], lp["mlp_w2"])
    return x, xc


def setup_inputs(seed: int = 0) -> dict:
    key = jax.random.key(seed)
    ks = iter(jax.random.split(key, 40))

    def nrm(shape, scale):
        return scale * jax.random.normal(next(ks), shape, jnp.float32)

    L = DEPTH
    return {
        "x": nrm((BATCH, SEQ, D_MODEL), 1.0),
        "c": nrm((BATCH, D_MODEL), 1.0),
        "ctx": nrm((BATCH, CTX_LEN, D_MODEL), 1.0),
        "c_ctx": nrm((D_MODEL,), 1.0),
        "ada_w": nrm((L, D_MODEL, 6 * D_MODEL), 0.5 * D_MODEL ** -0.5),
        "ada_b": nrm((L, 6 * D_MODEL), 0.02),
        "norm_mix_g": 1.0 + nrm((L, D_MODEL), 0.05),
        "norm_mlp_g": 1.0 + nrm((L, D_MODEL), 0.05),
        "w_in": nrm((L, D_MODEL, IN_COLS), D_MODEL ** -0.5),
        "rwkv_conv": jnp.array([0.25, 0.5, 0.25], jnp.float32)[None, :, None] + nrm((L, 3, RWKV_COLS), 0.05),
        "rwkv_w0": -2.5 + nrm((L, 2, A_WIDTH), 1.5),
        "rwkv_w_up": nrm((L, 2, LORA_W, A_WIDTH), 0.5 * LORA_W ** -0.5),
        "rwkv_a0": nrm((L, 2, A_WIDTH), 0.5),
        "rwkv_a_up": nrm((L, 2, LORA_A, A_WIDTH), 0.5 * LORA_A ** -0.5),
        "rwkv_g_up": nrm((L, LORA_G, A_WIDTH), LORA_G ** -0.5),
        "rwkv_k_k": 0.85 + nrm((L, A_WIDTH), 0.05),
        "rwkv_k_a": 1.0 + nrm((L, A_WIDTH), 0.05),
        "rwkv_r_k": nrm((L, A_HEADS, A_HEAD_DIM), 0.1),
        "rwkv_ln_g": 1.0 + nrm((L, A_WIDTH), 0.05),
        "rwkv_ln_b": nrm((L, A_WIDTH), 0.02),
        "mla_q_norm_g": 1.0 + nrm((L, Q_LORA), 0.05),
        "mla_w_uq": nrm((L, Q_LORA, B_HEADS * (QK_NOPE + QK_ROPE)), Q_LORA ** -0.5),
        "mla_kv_norm_g": 1.0 + nrm((L, KV_LORA), 0.05),
        "mla_w_ukv": nrm((L, KV_LORA, B_HEADS * (QK_NOPE + V_DIM)), KV_LORA ** -0.5),
        "na_rpb": nrm((L, C_HEADS, 2 * WIN_ROWS - 1, 2 * WIN_COLS - 1), 0.1),
        "w_branch": nrm((L, N_BRANCH, BRANCH_WIDTH, D_MODEL), BRANCH_WIDTH ** -0.5),
        "w_out": nrm((L, D_MODEL, D_MODEL), D_MODEL ** -0.5),
        "mlp_w1": nrm((L, D_MODEL, D_FF), D_MODEL ** -0.5),
        "mlp_w2": nrm((L, D_FF, D_MODEL), D_FF ** -0.5),
        "final_norm_g": 1.0 + nrm((D_MODEL,), 0.05),
    }


def reference(x, c, ctx, c_ctx, ada_w, ada_b, norm_mix_g, norm_mlp_g, w_in, rwkv_conv, rwkv_w0, rwkv_w_up,
              rwkv_a0, rwkv_a_up, rwkv_g_up, rwkv_k_k, rwkv_k_a, rwkv_r_k, rwkv_ln_g, rwkv_ln_b,
              mla_q_norm_g, mla_w_uq, mla_kv_norm_g, mla_w_ukv, na_rpb, w_branch, w_out, mlp_w1, mlp_w2,
              final_norm_g):
    ang_row, ang_col = axial_angles(x.shape[1])
    s_lat = jax.nn.silu(c)
    s_ctx = jax.nn.silu(c_ctx)
    xc = ctx
    for l in range(DEPTH):
        lp = {
            "norm_mix_g": norm_mix_g[l], "norm_mlp_g": norm_mlp_g[l], "w_in": w_in[l],
            "rwkv_conv": rwkv_conv[l], "rwkv_w0": rwkv_w0[l], "rwkv_w_up": rwkv_w_up[l],
            "rwkv_a0": rwkv_a0[l], "rwkv_a_up": rwkv_a_up[l], "rwkv_g_up": rwkv_g_up[l],
            "rwkv_k_k": rwkv_k_k[l], "rwkv_k_a": rwkv_k_a[l], "rwkv_r_k": rwkv_r_k[l],
            "rwkv_ln_g": rwkv_ln_g[l], "rwkv_ln_b": rwkv_ln_b[l],
            "mla_q_norm_g": mla_q_norm_g[l], "mla_w_uq": mla_w_uq[l],
            "mla_kv_norm_g": mla_kv_norm_g[l], "mla_w_ukv": mla_w_ukv[l],
            "na_rpb": na_rpb[l], "w_branch": w_branch[l], "w_out": w_out[l],
            "mlp_w1": mlp_w1[l], "mlp_w2": mlp_w2[l],
        }
        mod_lat = (s_lat @ ada_w[l] + ada_b[l])[:, None, :]
        mod_ctx = s_ctx @ ada_w[l] + ada_b[l]
        x, xc = trunk_layer(x, xc, mod_lat, mod_ctx, lp, ang_row, ang_col, need_ctx_out=(l < DEPTH - 1))
    return rms_norm(x, final_norm_g)
```

```python
import functools
import math

import jax
import jax.numpy as jnp
import numpy as np
from jax import lax
from jax.experimental import pallas as pl
from jax.experimental.pallas import tpu as pltpu

f32 = jnp.float32
bf16 = jnp.bfloat16
HIGHEST = lax.Precision.HIGHEST

GRID_W = 64
RMS_EPS = 1e-6
N_BRANCH = 3
A_HEAD_DIM = 64
LORA_W = 96
LORA_A = 96
LORA_G = 256
GN_EPS = 64e-5
QK_NOPE = 128
QK_ROPE = 64
V_DIM = 128
Q_LORA = 448
KV_LORA = 128
MLA_SCALE = (QK_NOPE + QK_ROPE) ** -0.5
ROPE_BASE = 10000.0
C_HEAD_DIM = 64
WIN_ROWS = 8
WIN_COLS = 16
NA_SCALE = C_HEAD_DIM ** -0.5
DECAY_SCALE = math.exp(-0.5)

LANES = 128
ROW_TILE = 256
CHUNK = 64
PREP_TILE = 128
PREC_A = "f32"
PREC_INV = "f32"
PREC_APPLY = "f32"
PREC_STATE = "f32"
VMEM_LIMIT = 48 * 1024 * 1024
NEG_BIG = -1e30


def _params(sem):
    return pltpu.CompilerParams(dimension_semantics=sem, vmem_limit_bytes=VMEM_LIMIT)


def _dot(a, b):
    return jnp.dot(a, b, preferred_element_type=f32)


def _dot_nt(a, b):
    return lax.dot_general(a, b, (((1,), (1,)), ((), ())), preferred_element_type=f32)


def _ada_body(s_ref, w_ref, b_ref, o_ref):
    s = s_ref[...]
    s = s * jax.nn.sigmoid(s)
    o_ref[...] = _dot(s, w_ref[...]) + b_ref[...]


def ada_modulation(cond, ada_w, ada_b):
    nl, d, n6 = ada_w.shape
    tn = 1024
    return pl.pallas_call(
        _ada_body,
        grid=(nl, n6 // tn),
        in_specs=[
            pl.BlockSpec((8, d), lambda l, j: (0, 0)),
            pl.BlockSpec((None, d, tn), lambda l, j: (l, 0, j)),
            pl.BlockSpec((None, 1, tn), lambda l, j: (l, 0, j)),
        ],
        out_specs=pl.BlockSpec((None, 8, tn), lambda l, j: (l, 0, j)),
        out_shape=jax.ShapeDtypeStruct((nl, 8, n6), f32),
        compiler_params=_params(("arbitrary", "arbitrary")),
        name="ada_modulation",
    )(cond, ada_w, ada_b.reshape(nl, 1, n6))


def _norm_mod_body(x_ref, g_ref, m_ref, o_ref, *, off, d):
    x = x_ref[...]
    y = x * lax.rsqrt(jnp.mean(x * x, axis=-1, keepdims=True) + RMS_EPS) * g_ref[...]
    shift = m_ref[:, off * d:(off + 1) * d]
    scale = m_ref[:, (off + 1) * d:(off + 2) * d]
    o_ref[...] = (y * (1.0 + scale) + shift).astype(o_ref.dtype)


def norm_modulate(x, g, mods2, off):
    n, d = x.shape
    return pl.pallas_call(
        functools.partial(_norm_mod_body, off=off, d=d),
        grid=(n // ROW_TILE,),
        in_specs=[
            pl.BlockSpec((ROW_TILE, d), lambda i: (i, 0)),
            pl.BlockSpec((1, d), lambda i: (0, 0)),
            pl.BlockSpec((None, 1, mods2.shape[-1]), lambda i: (jnp.where(i == 0, 1, 0), 0, 0)),
        ],
        out_specs=pl.BlockSpec((ROW_TILE, d), lambda i: (i, 0)),
        out_shape=jax.ShapeDtypeStruct((n, d), bf16),
        compiler_params=_params(("arbitrary",)),
        name="norm_modulate",
    )(x, g.reshape(1, d), mods2)


def _final_norm_body(x_ref, g_ref, o_ref):
    x = x_ref[...]
    o_ref[...] = x * lax.rsqrt(jnp.mean(x * x, axis=-1, keepdims=True) + RMS_EPS) * g_ref[...]


def final_norm(x, g, n_ctx_tiles):
    n, d = x.shape
    t = n - n_ctx_tiles * ROW_TILE
    return pl.pallas_call(
        _final_norm_body,
        grid=(t // ROW_TILE,),
        in_specs=[
            pl.BlockSpec((ROW_TILE, d), lambda i: (i + n_ctx_tiles, 0)),
            pl.BlockSpec((1, d), lambda i: (0, 0)),
        ],
        out_specs=pl.BlockSpec((ROW_TILE, d), lambda i: (i, 0)),
        out_shape=jax.ShapeDtypeStruct((t, d), f32),
        compiler_params=_params(("arbitrary",)),
        name="final_norm",
    )(x, g.reshape(1, d))


def _mm_body(a_ref, w_ref, o_ref):
    o_ref[...] = _dot(a_ref[...], w_ref[...]).astype(o_ref.dtype)


def _row_tile(n):
    for tm in (1056, 768, 512, 384, 256):
        if n % tm == 0:
            return tm
    raise ValueError(f"no row tile for {n} rows")


def matmul(a, w, out_dtype, tn):
    m, k = a.shape
    n = w.shape[1]
    tm = _row_tile(m)
    return pl.pallas_call(
        _mm_body,
        grid=(m // tm, n // tn),
        in_specs=[pl.BlockSpec((tm, k), lambda i, j: (i, 0)), pl.BlockSpec((k, tn), lambda i, j: (0, j))],
        out_specs=pl.BlockSpec((tm, tn), lambda i, j: (i, j)),
        out_shape=jax.ShapeDtypeStruct((m, n), out_dtype),
        compiler_params=_params(("arbitrary", "arbitrary")),
        name="matmul",
    )(a, w)


def _gate_rows(m_ref, tm, n_ctx):
    rows = pl.program_id(0) * tm + lax.broadcasted_iota(jnp.int32, (tm, 1), 0)
    return jnp.where(rows < n_ctx, m_ref[1:2, :], m_ref[0:1, :])


def _mm_res_body(a_ref, w_ref, x_ref, m_ref, o_ref, *, tm, n_ctx):
    acc = _dot(a_ref[...], w_ref[...])
    o_ref[...] = x_ref[...] + _gate_rows(m_ref, tm, n_ctx) * acc


def matmul_gated_residual(a, w, x, mods8, gate_off, n_ctx, tn=512):
    m, k = a.shape
    n = w.shape[1]
    tm = _row_tile(m)
    nj = n // tn
    return pl.pallas_call(
        functools.partial(_mm_res_body, tm=tm, n_ctx=n_ctx),
        grid=(m // tm, nj),
        in_specs=[
            pl.BlockSpec((tm, k), lambda i, j: (i, 0)),
            pl.BlockSpec((k, tn), lambda i, j: (0, j)),
            pl.BlockSpec((tm, tn), lambda i, j: (i, j)),
            pl.BlockSpec((8, tn), lambda i, j: (0, gate_off * nj + j)),
        ],
        out_specs=pl.BlockSpec((tm, tn), lambda i, j: (i, j)),
        out_shape=jax.ShapeDtypeStruct((m, n), f32),
        compiler_params=_params(("arbitrary", "arbitrary")),
        name="matmul_gated_residual",
    )(a, w, x, mods8)


def _merge_body(ya_ref, yb_ref, yc_ref, g0_ref, g1_ref, g2_ref, w_ref, o_ref):
    acc = jax.nn.sigmoid(g0_ref[...]) * _dot(ya_ref[...], w_ref[0])
    acc = acc + jax.nn.sigmoid(g1_ref[...]) * _dot(yb_ref[...], w_ref[1])
    acc = acc + jax.nn.sigmoid(g2_ref[...]) * _dot(yc_ref[...], w_ref[2])
    o_ref[...] = acc.astype(o_ref.dtype)


def merge_branches(ya, yb, yc, ug, w_branch, tn=512):
    m, k = ya.shape
    d = w_branch.shape[-1]
    tm = 528 if m % 528 == 0 else ROW_TILE
    nj = d // tn
    y_spec = pl.BlockSpec((tm, k), lambda i, j: (i, 0))
    return pl.pallas_call(
        _merge_body,
        grid=(m // tm, nj),
        in_specs=[
            y_spec, y_spec, y_spec,
            pl.BlockSpec((tm, tn), lambda i, j: (i, j)),
            pl.BlockSpec((tm, tn), lambda i, j: (i, nj + j)),
            pl.BlockSpec((tm, tn), lambda i, j: (i, 2 * nj + j)),
            pl.BlockSpec((N_BRANCH, k, tn), lambda i, j: (0, 0, j)),
        ],
        out_specs=pl.BlockSpec((tm, tn), lambda i, j: (i, j)),
        out_shape=jax.ShapeDtypeStruct((m, d), bf16),
        compiler_params=_params(("arbitrary", "arbitrary")),
        name="merge_branches",
    )(ya, yb, yc, ug, ug, ug, w_branch)


def _mlp_body(h_ref, w1_ref, w2_ref, x_ref, m_ref, o_ref, acc_ref, *, tm, n_ctx):
    f = pl.program_id(1)

    @pl.when(f == 0)
    def _():
        acc_ref[...] = jnp.zeros_like(acc_ref)

    h1 = jnp.maximum(_dot(h_ref[...], w1_ref[...]), 0.0)
    acc_ref[...] += _dot((h1 * h1).astype(bf16), w2_ref[...])

    @pl.when(f == pl.num_programs(1) - 1)
    def _():
        o_ref[...] = x_ref[...] + _gate_rows(m_ref, tm, n_ctx) * acc_ref[...]


def mlp_gated_residual(h, w1, w2, x, mods8, gate_off, n_ctx, tf=512):
    m, d = h.shape
    dff = w1.shape[1]
    tm = 528 if m % 528 == 0 else ROW_TILE
    return pl.pallas_call(
        functools.partial(_mlp_body, tm=tm, n_ctx=n_ctx),
        grid=(m // tm, dff // tf),
        in_specs=[
            pl.BlockSpec((tm, d), lambda i, f: (i, 0)),
            pl.BlockSpec((d, tf), lambda i, f: (0, f)),
            pl.BlockSpec((tf, d), lambda i, f: (f, 0)),
            pl.BlockSpec((tm, d), lambda i, f: (i, 0)),
            pl.BlockSpec((8, d), lambda i, f: (0, gate_off)),
        ],
        out_specs=pl.BlockSpec((tm, d), lambda i, f: (i, 0)),
        out_shape=jax.ShapeDtypeStruct((m, d), f32),
        scratch_shapes=[pltpu.VMEM((tm, d), f32)],
        compiler_params=_params(("arbitrary", "arbitrary")),
        name="mlp_gated_residual",
    )(h, w1, w2, x, mods8)


def _seg_sum(x, e_ref):
    hi = x.astype(bf16)
    lo = (x - hi.astype(f32)).astype(bf16)
    e = e_ref[...]
    return _dot(hi, e) + _dot(lo, e)


def _to_pairs(o_ref, val, lead=None):
    for p in range(val.shape[-1] // LANES):
        piece = val[:, p * LANES:(p + 1) * LANES]
        if lead is None:
            o_ref[p] = piece
        else:
            o_ref[lead, p] = piece


def _rwkv_prep_body(u_ref, up_ref, un_ref, l_ref, lp_ref, ln_ref, cw_ref, cwl_ref, wu_ref, au_ref, gu_ref,
                    w0_ref, a0_ref, kk_ref, ka_ref, rk_ref, e_ref,
                    r_o, v_o, kkn_o, bonus_o, g_o, lw_o, kd_o, bd_o, *, n_ctx_tiles, aw):
    i = pl.program_id(0)
    last = pl.num_programs(0) - 1
    left_zero = (i == 0) | (i == n_ctx_tiles)
    right_zero = (i == n_ctx_tiles - 1) | (i == last)
    rows = lax.broadcasted_iota(jnp.int32, (PREP_TILE, 1), 0)

    def conv(x_ref, xp_ref, xn_ref, w_ref):
        x = x_ref[...]
        prev_row = jnp.where(left_zero, 0.0, xp_ref[7:8, :])
        next_row = jnp.where(right_zero, 0.0, xn_ref[0:1, :])
        x_prev = jnp.where(rows == 0, prev_row, pltpu.roll(x, 1, 0))
        x_next = jnp.where(rows == PREP_TILE - 1, next_row, pltpu.roll(x, PREP_TILE - 1, 0))
        return w_ref[0:1, :] * x_prev + w_ref[1:2, :] * x + w_ref[2:3, :] * x_next

    y = conv(u_ref, up_ref, un_ref, cw_ref)
    r, k, v = y[:, :aw], y[:, aw:2 * aw], y[:, 2 * aw:]
    yl = conv(l_ref, lp_ref, ln_ref, cwl_ref)
    th = jnp.tanh(yl).astype(bf16)
    sg = jax.nn.sigmoid(yl).astype(bf16)
    ylb = yl.astype(bf16)

    _to_pairs(g_o, _dot(sg, gu_ref[...]))
    _to_pairs(r_o, r)
    _to_pairs(v_o, v)

    kkr = k * kk_ref[...]
    norm = jnp.sqrt(_seg_sum(kkr * kkr, e_ref))
    kkn = kkr / jnp.maximum(norm, 1e-12)
    _to_pairs(kkn_o, kkn)

    ksum = None
    for dr in range(2):
        z = w0_ref[dr:dr + 1, :] + _dot(th, wu_ref[dr])
        _to_pairs(lw_o, -DECAY_SCALE * jax.nn.sigmoid(z), lead=dr)
        a = jax.nn.sigmoid(a0_ref[dr:dr + 1, :] + _dot(ylb, au_ref[dr]))
        kd = k * (1.0 + (a - 1.0) * ka_ref[...])
        _to_pairs(kd_o, kd, lead=dr)
        _to_pairs(bd_o, kkn * a, lead=dr)
        ksum = kd if ksum is None else ksum + kd
    rk = _seg_sum(r * (0.5 * ksum) * rk_ref[...], e_ref)
    _to_pairs(bonus_o, rk * v)


def rwkv_prepare(u3, ub, pw, n_ctx):
    n = u3.shape[0]
    aw = u3.shape[1] // 3
    npair = aw // LANES
    lw = pw["conv_l"].shape[1]
    tpb = PREP_TILE // 8
    nb8 = n // 8

    def prev_map(i):
        return (jnp.maximum(i * tpb - 1, 0), 0)

    def next_map(i):
        return (jnp.minimum((i + 1) * tpb, nb8 - 1), 0)

    full = lambda a: pl.BlockSpec(a.shape, lambda i: (0,) * a.ndim)
    pm = jax.ShapeDtypeStruct((npair, n, LANES), f32)
    pm2 = jax.ShapeDtypeStruct((2, npair, n, LANES), f32)
    pm_spec = pl.BlockSpec((npair, PREP_TILE, LANES), lambda i: (0, i, 0))
    pm2_spec = pl.BlockSpec((2, npair, PREP_TILE, LANES), lambda i: (0, 0, i, 0))
    consts = [pw["conv_rkv"], pw["conv_l"], pw["w_up"], pw["a_up"], pw["g_up"], pw["w0"], pw["a0"],
              pw["k_k"], pw["k_a"], pw["r_k"], pw["seg"]]
    return pl.pallas_call(
        functools.partial(_rwkv_prep_body, n_ctx_tiles=n_ctx // PREP_TILE, aw=aw),
        grid=(n // PREP_TILE,),
        in_specs=[
            pl.BlockSpec((PREP_TILE, 3 * aw), lambda i: (i, 0)),
            pl.BlockSpec((8, 3 * aw), prev_map),
            pl.BlockSpec((8, 3 * aw), next_map),
            pl.BlockSpec((PREP_TILE, lw), lambda i: (i, 0)),
            pl.BlockSpec((8, lw), prev_map),
            pl.BlockSpec((8, lw), next_map),
        ] + [full(a) for a in consts],
        out_specs=[pm_spec] * 5 + [pm2_spec] * 3,
        out_shape=[pm] * 5 + [pm2] * 3,
        compiler_params=_params(("arbitrary",)),
        name="rwkv_prepare",
    )(u3, u3, u3, ub, ub, ub, *consts)


_DIMS = {"nn": (((1,), (0,)), ((), ())), "nt": (((1,), (1,)), ((), ())), "tn": (((0,), (0,)), ((), ()))}


def _split_bf16(x):
    hi = x.astype(bf16)
    return hi, (x - hi.astype(f32)).astype(bf16)


def _mm(a, b, dims, mode):
    dn = _DIMS[dims]
    if mode == "f32":
        return lax.dot_general(a, b, dn, precision=HIGHEST, preferred_element_type=f32)
    if mode == "bf16":
        return lax.dot_general(a.astype(bf16), b.astype(bf16), dn, preferred_element_type=f32)
    a_hi, a_lo = _split_bf16(a)
    b_hi, b_lo = _split_bf16(b)
    d = lambda x, y: lax.dot_general(x, y, dn, preferred_element_type=f32)
    return d(a_hi, b_hi) + (d(a_hi, b_lo) + d(a_lo, b_hi))


def _rwkv_chunk_body(r_ref, v_ref, kk_ref, lw_ref, k_ref, b_ref, y_ref, s_ref, *, n_pairs):
    d = pl.program_id(0)
    j = pl.program_id(1)
    c = CHUNK
    c2 = 2 * c

    @pl.when(j == 0)
    def _():
        s_ref[...] = jnp.zeros_like(s_ref)

    sign = jnp.where(d == 0, 1, -1)
    ti = lax.broadcasted_iota(jnp.int32, (c, c), 0)
    si = lax.broadcasted_iota(jnp.int32, (c, c), 1)
    incl_c = ((ti - si) * sign >= 0).astype(f32)
    t2 = lax.broadcasted_iota(jnp.int32, (c2, c2), 0)
    s2 = lax.broadcasted_iota(jnp.int32, (c2, c2), 1)
    same = (t2 >= c) == (s2 >= c)
    ahead = (t2 - s2) * sign
    incl = same & (ahead >= 0)
    strict = same & (ahead > 0)
    eye = (t2 == s2).astype(f32)
    low = lax.broadcasted_iota(jnp.int32, (1, LANES), 1) < A_HEAD_DIM

    def stack(x):
        return jnp.concatenate([jnp.where(low, x, 0.0), jnp.where(low, 0.0, x)], axis=0)

    def pair_step(p, carry):
        r, v, kk = r_ref[p], v_ref[p], kk_ref[p]
        lw, k, b = lw_ref[p], k_ref[p], b_ref[p]
        cum = _mm(incl_c, lw, "nn", "f32")
        tot = jnp.sum(lw, axis=0, keepdims=True)
        e_neg = jnp.exp(-cum)
        e_end = jnp.exp(tot - cum)
        r_s = stack(r * jnp.exp(cum))
        kk_s = stack(kk * jnp.exp(cum - lw))
        k_s = stack(k * e_neg)
        b_s = stack(b * e_neg)
        kd_s = stack(k * e_end)
        bd_s = stack(b * e_end)
        v_s = stack(v)
        a_kb = jnp.where(strict, _mm(kk_s, b_s, "nt", PREC_A), 0.0)
        a_kk = jnp.where(strict, _mm(kk_s, k_s, "nt", PREC_A), 0.0)
        a_rk = jnp.where(incl, _mm(r_s, k_s, "nt", PREC_A), 0.0)
        a_rb = jnp.where(incl, _mm(r_s, b_s, "nt", PREC_A), 0.0)
        t_inv = eye - a_kb
        pw = a_kb
        for _ in range(int(math.log2(c)) - 1):
            pw = _mm(pw, pw, "nn", PREC_INV)
            t_inv = t_inv + _mm(t_inv, pw, "nn", PREC_INV)
        k_g = _mm(t_inv, kk_s, "nn", PREC_APPLY)
        u_v = _mm(t_inv, _mm(a_kk, v_s, "nn", PREC_APPLY), "nn", PREC_APPLY)
        s0 = s_ref[p]
        u = _mm(k_g, s0, "nt", PREC_STATE) + u_v
        y = _mm(r_s, s0, "nt", PREC_STATE) + _mm(a_rk, v_s, "nn", PREC_APPLY) - _mm(a_rb, u, "nn", PREC_APPLY)
        y_ref[p] = y[:c] + y[c:]
        s_ref[p] = s0 * jnp.exp(tot) + _mm(v_s, kd_s, "tn", PREC_STATE) - _mm(u, bd_s, "tn", PREC_STATE)
        return carry

    lax.fori_loop(0, n_pairs, pair_step, 0)


def rwkv_chunk_scan(r, v, kk, lw, kd, bd, n_ctx):
    npair, n, _ = r.shape
    nc = n // CHUNK
    ncc = n_ctx // CHUNK

    def cidx(d, j):
        bwd = jnp.where(j < ncc, ncc - 1 - j, nc - 1 - (j - ncc))
        return jnp.where(d == 0, j, bwd)

    shared = pl.BlockSpec((npair, CHUNK, LANES), lambda d, j: (0, cidx(d, j), 0))
    per_dir = pl.BlockSpec((None, npair, CHUNK, LANES), lambda d, j: (d, 0, cidx(d, j), 0))
    return pl.pallas_call(
        functools.partial(_rwkv_chunk_body, n_pairs=npair),
        grid=(2, nc),
        in_specs=[shared, shared, shared, per_dir, per_dir, per_dir],
        out_specs=per_dir,
        out_shape=jax.ShapeDtypeStruct((2, npair, n, LANES), f32),
        scratch_shapes=[pltpu.VMEM((npair, LANES, LANES), f32)],
        compiler_params=_params(("arbitrary", "arbitrary")),
        name="rwkv_chunk_scan",
    )(r, v, kk, lw, kd, bd)


def _rwkv_readout_body(y_ref, bonus_ref, g_ref, lng_ref, lnb_ref, e_ref, o_ref, *, n_pairs):
    inv = 1.0 / A_HEAD_DIM
    for p in range(n_pairs):
        y = y_ref[0, p] + y_ref[1, p]
        yc = y - _seg_sum(y, e_ref) * inv
        var = _seg_sum(yc * yc, e_ref) * inv
        yn = yc * lax.rsqrt(var + GN_EPS) * lng_ref[p] + lnb_ref[p]
        o_ref[:, p * LANES:(p + 1) * LANES] = ((yn + bonus_ref[p]) * g_ref[p]).astype(o_ref.dtype)


def rwkv_readout(y, bonus, g, ln_g, ln_b, seg_pair):
    _, npair, n, _ = y.shape
    pm_spec = pl.BlockSpec((npair, ROW_TILE, LANES), lambda i: (0, i, 0))
    vec_spec = pl.BlockSpec((npair, 1, LANES), lambda i: (0, 0, 0))
    return pl.pallas_call(
        functools.partial(_rwkv_readout_body, n_pairs=npair),
        grid=(n // ROW_TILE,),
        in_specs=[pl.BlockSpec((2, npair, ROW_TILE, LANES), lambda i: (0, 0, i, 0)), pm_spec, pm_spec,
                  vec_spec, vec_spec, pl.BlockSpec((LANES, LANES), lambda i: (0, 0))],
        out_specs=pl.BlockSpec((ROW_TILE, npair * LANES), lambda i: (i, 0)),
        out_shape=jax.ShapeDtypeStruct((n, npair * LANES), bf16),
        compiler_params=_params(("arbitrary",)),
        name="rwkv_readout",
    )(y, bonus, g, ln_g.reshape(npair, 1, LANES), ln_b.reshape(npair, 1, LANES), seg_pair)


def _mla_prep_body(u_ref, cs_ref, gq_ref, gkv_ref, wqn_ref, wqp_ref, wqr_ref, wkv_ref, q_o, k_o, v_o,
                   *, n_heads, c_kvd, c_q, c_kr):
    ql = u_ref[:, c_q:c_q + 512]
    qn = ql * lax.rsqrt(jnp.sum(ql * ql, axis=-1, keepdims=True) * (1.0 / Q_LORA) + RMS_EPS) * gq_ref[...]
    qn = qn.astype(bf16)
    kvd = u_ref[:, c_kvd:c_kvd + KV_LORA]
    kvn = kvd * lax.rsqrt(jnp.mean(kvd * kvd, axis=-1, keepdims=True) + RMS_EPS) * gkv_ref[...]
    kv = _dot(kvn.astype(bf16), wkv_ref[...])
    cs = cs_ref[...]
    lane = lax.broadcasted_iota(jnp.int32, (1, LANES), 1)
    low = lane < QK_ROPE
    cos_t = jnp.where(low, cs, 0.0)
    sin_t = jnp.where(low, pltpu.roll(cs, QK_ROPE, 1), 0.0)
    kr = u_ref[:, c_kr:c_kr + LANES]
    prod = kr * cs
    k_pe = jnp.where(low, prod + pltpu.roll(prod, QK_ROPE, 1), 0.0).astype(bf16)
    q_nope = _dot(qn, wqn_ref[...])
    q_pe = _dot(qn, wqp_ref[...])
    q_pr = _dot(qn, wqr_ref[...])
    for h in range(n_heads):
        sl = slice(h * LANES, (h + 1) * LANES)
        q_o[h, :, 0:LANES] = q_nope[:, sl].astype(bf16)
        q_o[h, :, LANES:2 * LANES] = (q_pe[:, sl] * cos_t + q_pr[:, sl] * sin_t).astype(bf16)
        k_o[h, :, 0:LANES] = kv[:, sl].astype(bf16)
        k_o[h, :, LANES:2 * LANES] = k_pe
        v_o[h] = kv[:, n_heads * LANES + h * LANES:n_heads * LANES + (h + 1) * LANES].astype(bf16)


def mla_prepare(ub, cs, pw, cols):
    n = ub.shape[0]
    nh = pw["wq_nope"].shape[1] // LANES
    full = lambda a: pl.BlockSpec(a.shape, lambda i: (0,) * a.ndim)
    consts = [pw["gq"], pw["gkv"], pw["wq_nope"], pw["wq_pe"], pw["wq_pr"], pw["wkv"]]
    return pl.pallas_call(
        functools.partial(_mla_prep_body, n_heads=nh, **cols),
        grid=(n // ROW_TILE,),
        in_specs=[pl.BlockSpec((ROW_TILE, ub.shape[1]), lambda i: (i, 0)),
                  pl.BlockSpec((ROW_TILE, LANES), lambda i: (i, 0))] + [full(a) for a in consts],
        out_specs=[pl.BlockSpec((nh, ROW_TILE, 2 * LANES), lambda i: (0, i, 0)),
                   pl.BlockSpec((nh, ROW_TILE, 2 * LANES), lambda i: (0, i, 0)),
                   pl.BlockSpec((nh, ROW_TILE, LANES), lambda i: (0, i, 0))],
        out_shape=[jax.ShapeDtypeStruct((nh, n, 2 * LANES), bf16),
                   jax.ShapeDtypeStruct((nh, n, 2 * LANES), bf16),
                   jax.ShapeDtypeStruct((nh, n, LANES), bf16)],
        compiler_params=_params(("arbitrary",)),
        name="mla_prepare",
    )(ub, cs, *consts)


def _mla_attn_body(q_ref, k_ref, v_ref, o_ref, *, n_ctx, n_tok):
    i = pl.program_id(1)
    tk = ROW_TILE
    n_kv = jnp.where(i == 0, n_ctx // tk, n_tok // tk)
    q = q_ref[...]

    def step(c, carry):
        m, l, acc = carry
        off = pl.multiple_of(c * tk, tk)
        s = _dot_nt(q, k_ref[pl.ds(off, tk), :]) * MLA_SCALE
        m_new = jnp.maximum(m, jnp.max(s, axis=-1, keepdims=True))
        alpha = jnp.exp(m - m_new)
        p = jnp.exp(s - m_new)
        l = alpha * l + jnp.sum(p, axis=-1, keepdims=True)
        acc = alpha * acc + _dot(p.astype(bf16), v_ref[pl.ds(off, tk), :])
        return m_new, l, acc

    m0 = jnp.full((ROW_TILE, 1), NEG_BIG, f32)
    l0 = jnp.zeros((ROW_TILE, 1), f32)
    a0 = jnp.zeros((ROW_TILE, V_DIM), f32)
    _, l, acc = lax.fori_loop(0, n_kv, step, (m0, l0, a0))
    o_ref[...] = (acc / l).astype(o_ref.dtype)


def mla_attention(q, k, v, n_ctx):
    nh, n, dq = q.shape
    return pl.pallas_call(
        functools.partial(_mla_attn_body, n_ctx=n_ctx, n_tok=n),
        grid=(nh, n // ROW_TILE),
        in_specs=[
            pl.BlockSpec((None, ROW_TILE, dq), lambda h, i: (h, i, 0)),
            pl.BlockSpec((None, n, dq), lambda h, i: (h, 0, 0)),
            pl.BlockSpec((None, n, V_DIM), lambda h, i: (h, 0, 0)),
        ],
        out_specs=pl.BlockSpec((ROW_TILE, V_DIM), lambda h, i: (i, h)),
        out_shape=jax.ShapeDtypeStruct((n, nh * V_DIM), bf16),
        compiler_params=_params(("arbitrary", "arbitrary")),
        name="mla_attention",
    )(q, k, v)


def _na_body(q_ref, k_ref, v_ref, bias_ref, o_ref, *, n_ctx, n_rows):
    j = pl.program_id(1)
    lane = lax.broadcasted_iota(jnp.int32, (1, LANES), 1)
    low = lane < C_HEAD_DIM
    kc = k_ref[0:n_ctx, :]
    vc = v_ref[0:n_ctx, :]
    win = WIN_ROWS * GRID_W

    @pl.when(j == 0)
    def _():
        q2 = q_ref[...]
        outs = []
        for hh in range(2):
            qm = jnp.where(low if hh == 0 else ~low, q2, jnp.zeros_like(q2))
            s = _dot_nt(qm, kc) * NA_SCALE
            p = jnp.exp(s - jnp.max(s, axis=-1, keepdims=True))
            outs.append(_dot(p.astype(bf16), vc) / jnp.sum(p, axis=-1, keepdims=True))
        o_ref[...] = jnp.where(low, outs[0], outs[1]).astype(o_ref.dtype)

    @pl.when(j > 0)
    def _():
        rows_per_tile = ROW_TILE // GRID_W
        for il in range(rows_per_tile):
            i = (j - 1) * rows_per_tile + il
            r0 = jnp.clip(i - WIN_ROWS // 2, 0, n_rows - WIN_ROWS)
            d0 = r0 - i + WIN_ROWS - 1
            q2 = q_ref[il * GRID_W:(il + 1) * GRID_W, :]
            off = pl.multiple_of(n_ctx + r0 * GRID_W, GRID_W)
            kw = k_ref[pl.ds(off, win), :]
            vw = v_ref[pl.ds(off, win), :]
            outs = []
            for hh in range(2):
                qm = jnp.where(low if hh == 0 else ~low, q2, jnp.zeros_like(q2))
                s_w = _dot_nt(qm, kw) * NA_SCALE + bias_ref[hh, d0]
                s_c = _dot_nt(qm, kc) * NA_SCALE
                m = jnp.maximum(jnp.max(s_w, axis=-1, keepdims=True), jnp.max(s_c, axis=-1, keepdims=True))
                p_w = jnp.exp(s_w - m)
                p_c = jnp.exp(s_c - m)
                l = jnp.sum(p_w, axis=-1, keepdims=True) + jnp.sum(p_c, axis=-1, keepdims=True)
                outs.append((_dot(p_w.astype(bf16), vw) + _dot(p_c.astype(bf16), vc)) / l)
            o_ref[il * GRID_W:(il + 1) * GRID_W, :] = jnp.where(low, outs[0], outs[1]).astype(o_ref.dtype)


def na_attention(uc, bias, n_ctx):
    n = uc.shape[0]
    width = uc.shape[1] // 3
    npair = width // LANES
    n_rows = (n - n_ctx) // GRID_W
    return pl.pallas_call(
        functools.partial(_na_body, n_ctx=n_ctx, n_rows=n_rows),
        grid=(npair, n // ROW_TILE),
        in_specs=[
            pl.BlockSpec((ROW_TILE, LANES), lambda p, j: (j, p)),
            pl.BlockSpec((n, LANES), lambda p, j: (0, npair + p)),
            pl.BlockSpec((n, LANES), lambda p, j: (0, 2 * npair + p)),
            pl.BlockSpec((2, WIN_ROWS, GRID_W, WIN_ROWS * GRID_W), lambda p, j: (p, 0, 0, 0)),
        ],
        out_specs=pl.BlockSpec((ROW_TILE, LANES), lambda p, j: (j, p)),
        out_shape=jax.ShapeDtypeStruct((n, width), bf16),
        compiler_params=_params(("arbitrary", "arbitrary")),
        name="na_attention",
    )(uc, uc, uc, bias)


def _rope_tables(n_ctx, seq):
    t = jnp.arange(seq)
    row = (t // GRID_W).astype(f32)
    col = (t % GRID_W).astype(f32)
    n_freq = QK_ROPE // 4
    inv = ROPE_BASE ** (-jnp.arange(n_freq, dtype=f32) / n_freq)
    ar, ac = row[:, None] * inv[None, :], col[:, None] * inv[None, :]
    cos = jnp.concatenate([jnp.cos(ar), jnp.cos(ar), jnp.cos(ac), jnp.cos(ac)], axis=1)
    sin = jnp.concatenate([-jnp.sin(ar), jnp.sin(ar), -jnp.sin(ac), jnp.sin(ac)], axis=1)
    lat = jnp.concatenate([cos, sin], axis=1)
    ctx = jnp.concatenate([jnp.ones((n_ctx, QK_ROPE), f32), jnp.zeros((n_ctx, QK_ROPE), f32)], axis=1)
    return jnp.concatenate([ctx, lat], axis=0)


def _pair_swap_perm():
    q = QK_ROPE // 4
    return np.concatenate([np.arange(q, 2 * q), np.arange(0, q), np.arange(3 * q, 4 * q), np.arange(2 * q, 3 * q)])


def _na_bias_tables(rpb):
    cols = np.arange(GRID_W)
    c0 = np.clip(cols - WIN_COLS // 2, 0, GRID_W - WIN_COLS)
    inside = (cols[None, :] >= c0[:, None]) & (cols[None, :] < c0[:, None] + WIN_COLS)
    rel = np.clip(cols[None, :] - cols[:, None] + WIN_COLS - 1, 0, 2 * WIN_COLS - 2)
    full = jnp.where(inside[None, None], rpb[:, :, rel], NEG_BIG)
    tabs = [jnp.concatenate([full[:, d0 + jj] for jj in range(WIN_ROWS)], axis=-1) for d0 in range(WIN_ROWS)]
    return jnp.stack(tabs, axis=1).astype(f32)


def _layer_weights(l, w_in, rwkv_conv, rwkv_w0, rwkv_w_up, rwkv_a0, rwkv_a_up, rwkv_g_up, rwkv_k_k, rwkv_k_a,
                   rwkv_r_k, mla_q_norm_g, mla_w_uq, mla_kv_norm_g, mla_w_ukv, w_branch, w_out, mlp_w1, mlp_w2):
    d = w_in.shape[1]
    aw = rwkv_k_k.shape[1]
    nh_b = mla_w_ukv.shape[2] // (QK_NOPE + V_DIM)
    n_lora = 2 * LORA_W + 2 * LORA_A + LORA_G
    c_lora = 3 * aw
    c_mla = c_lora + n_lora
    c_na = c_mla + Q_LORA + KV_LORA + QK_ROPE
    c_gate = c_na + 3 * aw
    wi = w_in[l]
    perm = _pair_swap_perm()
    kr = wi[:, c_mla + Q_LORA + KV_LORA:c_na]
    w_b = jnp.concatenate([
        wi[:, c_lora:c_mla],
        wi[:, c_mla + Q_LORA:c_mla + Q_LORA + KV_LORA],
        wi[:, c_mla:c_mla + Q_LORA], jnp.zeros((d, 512 - Q_LORA), f32),
        kr, kr[:, perm]], axis=1).astype(bf16)
    cols = dict(c_kvd=n_lora, c_q=n_lora + KV_LORA, c_kr=n_lora + KV_LORA + 512)

    def lora_pad(w, start):
        r = w.shape[1]
        out = jnp.zeros((2, n_lora, aw), f32)
        for z in range(2):
            out = out.at[z, start + z * r:start + (z + 1) * r].set(w[z])
        return out.astype(bf16)

    g_up = jnp.zeros((n_lora, aw), f32).at[2 * LORA_W + 2 * LORA_A:].set(rwkv_g_up[l]).astype(bf16)
    head_id = np.arange(aw) // A_HEAD_DIM
    seg = jnp.asarray(head_id[:, None] == head_id[None, :], bf16)
    rwkv = dict(conv_rkv=rwkv_conv[l][:, :c_lora], conv_l=rwkv_conv[l][:, c_lora:c_mla],
                w_up=lora_pad(rwkv_w_up[l], 0), a_up=lora_pad(rwkv_a_up[l], 2 * LORA_W), g_up=g_up,
                w0=rwkv_w0[l], a0=rwkv_a0[l], k_k=rwkv_k_k[l][None], k_a=rwkv_k_a[l][None],
                r_k=rwkv_r_k[l].reshape(1, aw), seg=seg)

    uq = mla_w_uq[l].reshape(Q_LORA, nh_b, QK_NOPE + QK_ROPE)
    uq = jnp.concatenate([uq, jnp.zeros((512 - Q_LORA, nh_b, QK_NOPE + QK_ROPE), f32)], axis=0)
    zpad = jnp.zeros((512, nh_b, LANES - QK_ROPE), f32)
    pe = uq[:, :, QK_NOPE:]
    ukv = mla_w_ukv[l].reshape(KV_LORA, nh_b, QK_NOPE + V_DIM)
    mla = dict(
        gq=jnp.concatenate([mla_q_norm_g[l], jnp.zeros((512 - Q_LORA,), f32)])[None],
        gkv=mla_kv_norm_g[l][None],
        wq_nope=uq[:, :, :QK_NOPE].reshape(512, nh_b * QK_NOPE).astype(bf16),
        wq_pe=jnp.concatenate([pe, zpad], axis=2).reshape(512, nh_b * LANES).astype(bf16),
        wq_pr=jnp.concatenate([pe[:, :, perm], zpad], axis=2).reshape(512, nh_b * LANES).astype(bf16),
        wkv=jnp.concatenate([ukv[:, :, :QK_NOPE].reshape(KV_LORA, -1), ukv[:, :, QK_NOPE:].reshape(KV_LORA, -1)],
                            axis=1).astype(bf16))
    return dict(w_rkv=wi[:, :c_lora].astype(bf16), w_b=w_b, w_na=wi[:, c_na:c_gate].astype(bf16),
                w_gate=wi[:, c_gate:].astype(bf16), cols=cols, rwkv=rwkv, mla=mla,
                w_branch=w_branch[l].astype(bf16), w_out=w_out[l].astype(bf16),
                mlp_w1=mlp_w1[l].astype(bf16), mlp_w2=mlp_w2[l].astype(bf16))


def kernel(x, c, ctx, c_ctx, ada_w, ada_b, norm_mix_g, norm_mlp_g, w_in, rwkv_conv, rwkv_w0, rwkv_w_up, rwkv_a0, rwkv_a_up, rwkv_g_up, rwkv_k_k, rwkv_k_a, rwkv_r_k, rwkv_ln_g, rwkv_ln_b, mla_q_norm_g, mla_w_uq, mla_kv_norm_g, mla_w_ukv, na_rpb, w_branch, w_out, mlp_w1, mlp_w2, final_norm_g):
    batch, seq, d = x.shape
    n_ctx = ctx.shape[1]
    depth = ada_w.shape[0]
    assert batch == 1 and n_ctx == ROW_TILE and seq % ROW_TILE == 0 and seq // GRID_W >= WIN_ROWS
    n_ctx_tiles = n_ctx // ROW_TILE

    xs = jnp.concatenate([ctx[0], x[0]], axis=0)
    cond = jnp.concatenate([c, c_ctx[None], jnp.zeros((6, d), f32)], axis=0)
    mods = ada_modulation(cond, ada_w, ada_b)
    cs = _rope_tables(n_ctx, seq)

    for l in range(depth):
        lw = _layer_weights(l, w_in, rwkv_conv, rwkv_w0, rwkv_w_up, rwkv_a0, rwkv_a_up, rwkv_g_up, rwkv_k_k,
                            rwkv_k_a, rwkv_r_k, mla_q_norm_g, mla_w_uq, mla_kv_norm_g, mla_w_ukv, w_branch,
                            w_out, mlp_w1, mlp_w2)
        mods8 = mods[l]
        mods2 = mods8[:2, None, :]
        h = norm_modulate(xs, norm_mix_g[l], mods2, 0)
        u3 = matmul(h, lw["w_rkv"], f32, 512)
        ub = matmul(h, lw["w_b"], f32, lw["w_b"].shape[1])
        uc = matmul(h, lw["w_na"], bf16, 512)
        ug = matmul(h, lw["w_gate"], f32, 512)

        r, v, kk, bonus, g, lwd, kd, bd = rwkv_prepare(u3, ub, lw["rwkv"], n_ctx)
        y = rwkv_chunk_scan(r, v, kk, lwd, kd, bd, n_ctx)
        ya = rwkv_readout(y, bonus, g, rwkv_ln_g[l], rwkv_ln_b[l], lw["rwkv"]["seg"][:LANES, :LANES])

        q_b, k_b, v_b = mla_prepare(ub, cs, lw["mla"], lw["cols"])
        yb = mla_attention(q_b, k_b, v_b, n_ctx)

        yc = na_attention(uc, _na_bias_tables(na_rpb[l]), n_ctx)

        merged = merge_branches(ya, yb, yc, ug, lw["w_branch"])
        xs = matmul_gated_residual(merged, lw["w_out"], xs, mods8, 2, n_ctx)
        h2 = norm_modulate(xs, norm_mlp_g[l], mods2, 3)
        xs = mlp_gated_residual(h2, lw["mlp_w1"], lw["mlp_w2"], xs, mods8, 5, n_ctx)

    return final_norm(xs, final_norm_g, n_ctx_tiles)[None]
```

```python
import functools
import math

import jax
import jax.numpy as jnp
import numpy as np
from jax import lax
from jax.experimental import pallas as pl
from jax.experimental.pallas import tpu as pltpu

f32 = jnp.float32
bf16 = jnp.bfloat16
HIGHEST = lax.Precision.HIGHEST

GRID_W = 64
RMS_EPS = 1e-6
N_BRANCH = 3
A_HEAD_DIM = 64
LORA_W = 96
LORA_A = 96
LORA_G = 256
GN_EPS = 64e-5
QK_NOPE = 128
QK_ROPE = 64
V_DIM = 128
Q_LORA = 448
KV_LORA = 128
MLA_SCALE = (QK_NOPE + QK_ROPE) ** -0.5
MLA_Q_SCALE = MLA_SCALE * math.log2(math.e)
ROPE_BASE = 10000.0
C_HEAD_DIM = 64
WIN_ROWS = 8
WIN_COLS = 16
NA_SCALE = C_HEAD_DIM ** -0.5
DECAY_SCALE = math.exp(-0.5)

LANES = 128
ROW_TILE = 256
CHUNK = 64
PREP_TILE = 128
MLA_HEADS_PER_STEP = 2
VMEM_LIMIT = 48 * 1024 * 1024
NEG_BIG = -1e30


def _params(sem):
    return pltpu.CompilerParams(dimension_semantics=sem, vmem_limit_bytes=VMEM_LIMIT)


def _dot(a, b):
    return jnp.dot(a, b, preferred_element_type=f32)


def _dot_nt(a, b):
    return lax.dot_general(a, b, (((1,), (1,)), ((), ())), preferred_element_type=f32)


def _ada_body(s_ref, w_ref, b_ref, o_ref):
    s = s_ref[...]
    s = s * jax.nn.sigmoid(s)
    o_ref[...] = _dot(s, w_ref[...]) + b_ref[...]


def ada_modulation(cond, ada_w, ada_b):
    nl, d, n6 = ada_w.shape
    tn = 1024
    return pl.pallas_call(
        _ada_body,
        grid=(nl, n6 // tn),
        in_specs=[
            pl.BlockSpec((8, d), lambda l, j: (0, 0)),
            pl.BlockSpec((None, d, tn), lambda l, j: (l, 0, j)),
            pl.BlockSpec((None, 1, tn), lambda l, j: (l, 0, j)),
        ],
        out_specs=pl.BlockSpec((None, 8, tn), lambda l, j: (l, 0, j)),
        out_shape=jax.ShapeDtypeStruct((nl, 8, n6), f32),
        compiler_params=_params(("arbitrary", "arbitrary")),
        name="ada_modulation",
    )(cond, ada_w, ada_b.reshape(nl, 1, n6))


def _norm_mod_body(x_ref, g_ref, m_ref, o_ref, *, off, d):
    x = x_ref[...]
    y = x * lax.rsqrt(jnp.mean(x * x, axis=-1, keepdims=True) + RMS_EPS) * g_ref[...]
    shift = m_ref[:, off * d:(off + 1) * d]
    scale = m_ref[:, (off + 1) * d:(off + 2) * d]
    o_ref[...] = (y * (1.0 + scale) + shift).astype(o_ref.dtype)


def norm_modulate(x, g, mods2, off):
    n, d = x.shape
    return pl.pallas_call(
        functools.partial(_norm_mod_body, off=off, d=d),
        grid=(n // ROW_TILE,),
        in_specs=[
            pl.BlockSpec((ROW_TILE, d), lambda i: (i, 0)),
            pl.BlockSpec((1, d), lambda i: (0, 0)),
            pl.BlockSpec((None, 1, mods2.shape[-1]), lambda i: (jnp.where(i == 0, 1, 0), 0, 0)),
        ],
        out_specs=pl.BlockSpec((ROW_TILE, d), lambda i: (i, 0)),
        out_shape=jax.ShapeDtypeStruct((n, d), bf16),
        compiler_params=_params(("arbitrary",)),
        name="norm_modulate",
    )(x, g.reshape(1, d), mods2)


def _final_norm_body(x_ref, g_ref, o_ref):
    x = x_ref[...]
    o_ref[...] = x * lax.rsqrt(jnp.mean(x * x, axis=-1, keepdims=True) + RMS_EPS) * g_ref[...]


def final_norm(x, g, n_ctx_tiles):
    n, d = x.shape
    t = n - n_ctx_tiles * ROW_TILE
    return pl.pallas_call(
        _final_norm_body,
        grid=(t // ROW_TILE,),
        in_specs=[
            pl.BlockSpec((ROW_TILE, d), lambda i: (i + n_ctx_tiles, 0)),
            pl.BlockSpec((1, d), lambda i: (0, 0)),
        ],
        out_specs=pl.BlockSpec((ROW_TILE, d), lambda i: (i, 0)),
        out_shape=jax.ShapeDtypeStruct((t, d), f32),
        compiler_params=_params(("arbitrary",)),
        name="final_norm",
    )(x, g.reshape(1, d))


def _mm_body(a_ref, w_ref, o_ref):
    o_ref[...] = _dot(a_ref[...], w_ref[...]).astype(o_ref.dtype)


def _row_tile(n):
    for tm in (1056, 768, 512, 384, 256):
        if n % tm == 0:
            return tm
    raise ValueError(f"no row tile for {n} rows")


def matmul(a, w, out_dtype, tn):
    m, k = a.shape
    n = w.shape[1]
    tm = _row_tile(m)
    return pl.pallas_call(
        _mm_body,
        grid=(m // tm, n // tn),
        in_specs=[pl.BlockSpec((tm, k), lambda i, j: (i, 0)), pl.BlockSpec((k, tn), lambda i, j: (0, j))],
        out_specs=pl.BlockSpec((tm, tn), lambda i, j: (i, j)),
        out_shape=jax.ShapeDtypeStruct((m, n), out_dtype),
        compiler_params=_params(("arbitrary", "arbitrary")),
        name="matmul",
    )(a, w)


def _gate_rows(m_ref, tm, n_ctx):
    rows = pl.program_id(0) * tm + lax.broadcasted_iota(jnp.int32, (tm, 1), 0)
    return jnp.where(rows < n_ctx, m_ref[1:2, :], m_ref[0:1, :])


def _mm_res_body(a_ref, w_ref, x_ref, m_ref, o_ref, *, tm, n_ctx):
    acc = _dot(a_ref[...], w_ref[...])
    o_ref[...] = x_ref[...] + _gate_rows(m_ref, tm, n_ctx) * acc


def matmul_gated_residual(a, w, x, mods8, gate_off, n_ctx, tn=512):
    m, k = a.shape
    n = w.shape[1]
    tm = _row_tile(m)
    nj = n // tn
    return pl.pallas_call(
        functools.partial(_mm_res_body, tm=tm, n_ctx=n_ctx),
        grid=(m // tm, nj),
        in_specs=[
            pl.BlockSpec((tm, k), lambda i, j: (i, 0)),
            pl.BlockSpec((k, tn), lambda i, j: (0, j)),
            pl.BlockSpec((tm, tn), lambda i, j: (i, j)),
            pl.BlockSpec((8, tn), lambda i, j: (0, gate_off * nj + j)),
        ],
        out_specs=pl.BlockSpec((tm, tn), lambda i, j: (i, j)),
        out_shape=jax.ShapeDtypeStruct((m, n), f32),
        compiler_params=_params(("arbitrary", "arbitrary")),
        name="matmul_gated_residual",
    )(a, w, x, mods8)


def _merge_body(ya_ref, yb_ref, yc_ref, g0_ref, g1_ref, g2_ref, w_ref, o_ref):
    acc = jax.nn.sigmoid(g0_ref[...]) * _dot(ya_ref[...], w_ref[0])
    acc = acc + jax.nn.sigmoid(g1_ref[...]) * _dot(yb_ref[...], w_ref[1])
    acc = acc + jax.nn.sigmoid(g2_ref[...]) * _dot(yc_ref[...], w_ref[2])
    o_ref[...] = acc.astype(o_ref.dtype)


def merge_branches(ya, yb, yc, ug, w_branch, tn=512):
    m, k = ya.shape
    d = w_branch.shape[-1]
    tm = 528 if m % 528 == 0 else ROW_TILE
    nj = d // tn
    y_spec = pl.BlockSpec((tm, k), lambda i, j: (i, 0))
    return pl.pallas_call(
        _merge_body,
        grid=(m // tm, nj),
        in_specs=[
            y_spec, y_spec, y_spec,
            pl.BlockSpec((tm, tn), lambda i, j: (i, j)),
            pl.BlockSpec((tm, tn), lambda i, j: (i, nj + j)),
            pl.BlockSpec((tm, tn), lambda i, j: (i, 2 * nj + j)),
            pl.BlockSpec((N_BRANCH, k, tn), lambda i, j: (0, 0, j)),
        ],
        out_specs=pl.BlockSpec((tm, tn), lambda i, j: (i, j)),
        out_shape=jax.ShapeDtypeStruct((m, d), bf16),
        compiler_params=_params(("arbitrary", "arbitrary")),
        name="merge_branches",
    )(ya, yb, yc, ug, ug, ug, w_branch)


def _mlp_body(h_ref, w1_ref, w2_ref, x_ref, m_ref, o_ref, acc_ref, *, tm, n_ctx):
    f = pl.program_id(1)

    @pl.when(f == 0)
    def _():
        acc_ref[...] = jnp.zeros_like(acc_ref)

    h1 = jnp.maximum(_dot(h_ref[...], w1_ref[...]), 0.0)
    acc_ref[...] += _dot((h1 * h1).astype(bf16), w2_ref[...])

    @pl.when(f == pl.num_programs(1) - 1)
    def _():
        o_ref[...] = x_ref[...] + _gate_rows(m_ref, tm, n_ctx) * acc_ref[...]


def mlp_gated_residual(h, w1, w2, x, mods8, gate_off, n_ctx, tf=512):
    m, d = h.shape
    dff = w1.shape[1]
    tm = 528 if m % 528 == 0 else ROW_TILE
    return pl.pallas_call(
        functools.partial(_mlp_body, tm=tm, n_ctx=n_ctx),
        grid=(m // tm, dff // tf),
        in_specs=[
            pl.BlockSpec((tm, d), lambda i, f: (i, 0)),
            pl.BlockSpec((d, tf), lambda i, f: (0, f)),
            pl.BlockSpec((tf, d), lambda i, f: (f, 0)),
            pl.BlockSpec((tm, d), lambda i, f: (i, 0)),
            pl.BlockSpec((8, d), lambda i, f: (0, gate_off)),
        ],
        out_specs=pl.BlockSpec((tm, d), lambda i, f: (i, 0)),
        out_shape=jax.ShapeDtypeStruct((m, d), f32),
        scratch_shapes=[pltpu.VMEM((tm, d), f32)],
        compiler_params=_params(("arbitrary", "arbitrary")),
        name="mlp_gated_residual",
    )(h, w1, w2, x, mods8)


def _seg_sum(x, e_ref):
    hi = x.astype(bf16)
    lo = (x - hi.astype(f32)).astype(bf16)
    e = e_ref[...]
    return _dot(hi, e) + _dot(lo, e)


def _to_pairs(o_ref, val, lead=None):
    for p in range(val.shape[-1] // LANES):
        piece = val[:, p * LANES:(p + 1) * LANES]
        if lead is None:
            o_ref[p] = piece
        else:
            o_ref[lead, p] = piece


def _rwkv_prep_body(u_ref, up_ref, un_ref, l_ref, lp_ref, ln_ref, cw_ref, cwl_ref, wu_ref, au_ref, gu_ref,
                    w0_ref, a0_ref, kk_ref, ka_ref, rk_ref, e_ref,
                    r_o, v_o, kkn_o, bonus_o, g_o, lw_o, kd_o, bd_o, *, n_ctx_tiles, aw):
    i = pl.program_id(0)
    last = pl.num_programs(0) - 1
    left_zero = (i == 0) | (i == n_ctx_tiles)
    right_zero = (i == n_ctx_tiles - 1) | (i == last)
    rows = lax.broadcasted_iota(jnp.int32, (PREP_TILE, 1), 0)

    def conv(x_ref, xp_ref, xn_ref, w_ref):
        x = x_ref[...]
        prev_row = jnp.where(left_zero, 0.0, xp_ref[7:8, :])
        next_row = jnp.where(right_zero, 0.0, xn_ref[0:1, :])
        x_prev = jnp.where(rows == 0, prev_row, pltpu.roll(x, 1, 0))
        x_next = jnp.where(rows == PREP_TILE - 1, next_row, pltpu.roll(x, PREP_TILE - 1, 0))
        return w_ref[0:1, :] * x_prev + w_ref[1:2, :] * x + w_ref[2:3, :] * x_next

    y = conv(u_ref, up_ref, un_ref, cw_ref)
    r, k, v = y[:, :aw], y[:, aw:2 * aw], y[:, 2 * aw:]
    yl = conv(l_ref, lp_ref, ln_ref, cwl_ref)
    th = jnp.tanh(yl).astype(bf16)
    sg = jax.nn.sigmoid(yl).astype(bf16)
    ylb = yl.astype(bf16)

    _to_pairs(g_o, _dot(sg, gu_ref[...]))
    _to_pairs(r_o, r)
    _to_pairs(v_o, v)

    kkr = k * kk_ref[...]
    norm = jnp.sqrt(_seg_sum(kkr * kkr, e_ref))
    kkn = kkr / jnp.maximum(norm, 1e-12)
    _to_pairs(kkn_o, kkn)

    ksum = None
    for dr in range(2):
        z = w0_ref[dr:dr + 1, :] + _dot(th, wu_ref[dr])
        _to_pairs(lw_o, -DECAY_SCALE * jax.nn.sigmoid(z), lead=dr)
        a = jax.nn.sigmoid(a0_ref[dr:dr + 1, :] + _dot(ylb, au_ref[dr]))
        kd = k * (1.0 + (a - 1.0) * ka_ref[...])
        _to_pairs(kd_o, kd, lead=dr)
        _to_pairs(bd_o, kkn * a, lead=dr)
        ksum = kd if ksum is None else ksum + kd
    rk = _seg_sum(r * (0.5 * ksum) * rk_ref[...], e_ref)
    _to_pairs(bonus_o, rk * v)


def rwkv_prepare(u3, ub, pw, n_ctx):
    n = u3.shape[0]
    aw = u3.shape[1] // 3
    npair = aw // LANES
    lw = pw["conv_l"].shape[1]
    tpb = PREP_TILE // 8
    nb8 = n // 8

    def prev_map(i):
        return (jnp.maximum(i * tpb - 1, 0), 0)

    def next_map(i):
        return (jnp.minimum((i + 1) * tpb, nb8 - 1), 0)

    full = lambda a: pl.BlockSpec(a.shape, lambda i: (0,) * a.ndim)
    pm = jax.ShapeDtypeStruct((npair, n, LANES), f32)
    pm2 = jax.ShapeDtypeStruct((2, npair, n, LANES), f32)
    pm_spec = pl.BlockSpec((npair, PREP_TILE, LANES), lambda i: (0, i, 0))
    pm2_spec = pl.BlockSpec((2, npair, PREP_TILE, LANES), lambda i: (0, 0, i, 0))
    consts = [pw["conv_rkv"], pw["conv_l"], pw["w_up"], pw["a_up"], pw["g_up"], pw["w0"], pw["a0"],
              pw["k_k"], pw["k_a"], pw["r_k"], pw["seg"]]
    return pl.pallas_call(
        functools.partial(_rwkv_prep_body, n_ctx_tiles=n_ctx // PREP_TILE, aw=aw),
        grid=(n // PREP_TILE,),
        in_specs=[
            pl.BlockSpec((PREP_TILE, 3 * aw), lambda i: (i, 0)),
            pl.BlockSpec((8, 3 * aw), prev_map),
            pl.BlockSpec((8, 3 * aw), next_map),
            pl.BlockSpec((PREP_TILE, lw), lambda i: (i, 0)),
            pl.BlockSpec((8, lw), prev_map),
            pl.BlockSpec((8, lw), next_map),
        ] + [full(a) for a in consts],
        out_specs=[pm_spec] * 5 + [pm2_spec] * 3,
        out_shape=[pm] * 5 + [pm2] * 3,
        compiler_params=_params(("arbitrary",)),
        name="rwkv_prepare",
    )(u3, u3, u3, ub, ub, ub, *consts)


def _dot_tn(a, b):
    return lax.dot_general(a, b, (((0,), (0,)), ((), ())), preferred_element_type=f32)


def _rwkv_chunk_body(r_ref, v_ref, kk_ref, lw_ref, k_ref, b_ref, y_ref, h_ref, *, n_pairs):
    d = pl.program_id(0)
    j = pl.program_id(1)
    c = CHUNK
    c2 = 2 * c

    @pl.when(j == 0)
    def _():
        h_ref[...] = jnp.zeros_like(h_ref)

    sign = jnp.where(d == 0, 1, -1)
    ti = lax.broadcasted_iota(jnp.int32, (c, c), 0)
    si = lax.broadcasted_iota(jnp.int32, (c, c), 1)
    incl_c = jnp.where((ti - si) * sign >= 0, 1.0, 0.0).astype(bf16)
    t2 = lax.broadcasted_iota(jnp.int32, (c2, c2), 0)
    s2 = lax.broadcasted_iota(jnp.int32, (c2, c2), 1)
    ahead = (t2 - s2) * sign
    blk = lambda n: (t2 // n) == (s2 // n)
    incl = blk(c) & (ahead >= 0)
    strict = blk(c) & (ahead > 0)
    eye = jnp.where(t2 == s2, 1.0, 0.0)
    inv_levels = []
    n = 2
    while n < c:
        inv_levels.append(strict & blk(2 * n) & ~blk(n))
        n *= 2
    low = lax.broadcasted_iota(jnp.int32, (1, LANES), 1) < A_HEAD_DIM

    def stack(x):
        return jnp.concatenate([jnp.where(low, x, 0.0), jnp.where(low, 0.0, x)], axis=0).astype(bf16)

    pairs = range(n_pairs)
    each = lambda f: [f(p) for p in pairs]
    lw = each(lambda p: lw_ref[p])
    lw_hi = each(lambda p: lw[p].astype(bf16))
    lw_lo = each(lambda p: (lw[p] - lw_hi[p].astype(f32)).astype(bf16))
    cum = each(lambda p: _dot(incl_c, lw_hi[p]) + _dot(incl_c, lw_lo[p]))
    tot = each(lambda p: jnp.sum(lw[p], axis=0, keepdims=True))
    e_neg = each(lambda p: jnp.exp(-cum[p]))
    e_end = each(lambda p: jnp.exp(tot[p] - cum[p]))
    r_s = each(lambda p: stack(r_ref[p] * jnp.exp(cum[p])))
    kk_s = each(lambda p: stack(kk_ref[p] * jnp.exp(cum[p] - lw[p])))
    kb_s = each(lambda p: jnp.concatenate([stack(b_ref[p] * e_neg[p]), stack(k_ref[p] * e_neg[p])], axis=0))
    kbd_s = each(lambda p: jnp.concatenate([stack(k_ref[p] * e_end[p]), stack(b_ref[p] * e_end[p])], axis=0))
    v_s = each(lambda p: stack(v_ref[p]))
    a1 = each(lambda p: _dot_nt(kk_s[p], kb_s[p]))
    a2 = each(lambda p: _dot_nt(r_s[p], kb_s[p]))
    a_kb = each(lambda p: jnp.where(strict, a1[p][:, :c2], 0.0))
    a_kk = each(lambda p: jnp.where(strict, a1[p][:, c2:], 0.0).astype(bf16))
    a_rb = each(lambda p: jnp.where(incl, a2[p][:, :c2], 0.0).astype(bf16))
    a_rk = each(lambda p: jnp.where(incl, a2[p][:, c2:], 0.0).astype(bf16))
    t_inv = each(lambda p: eye - jnp.where(blk(2), a_kb[p], 0.0))
    for m in inv_levels:
        t_b = each(lambda p: t_inv[p].astype(bf16))
        x = each(lambda p: _dot(t_b[p], jnp.where(m, a_kb[p], 0.0).astype(bf16)).astype(bf16))
        t_inv = each(lambda p: t_inv[p] - _dot(x[p], t_b[p]))
    g = each(lambda p: _dot(t_inv[p].astype(bf16), jnp.concatenate([kk_s[p], a_kk[p]], axis=1)).astype(bf16))
    h0 = each(lambda p: h_ref[p])
    h0_b = each(lambda p: h0[p].astype(bf16))
    u = each(lambda p: _dot(g[p], jnp.concatenate([h0_b[p], v_s[p]], axis=0)).astype(bf16))
    y = each(lambda p: _dot(jnp.concatenate([r_s[p], a_rk[p], -a_rb[p]], axis=1),
                            jnp.concatenate([h0_b[p], v_s[p], u[p]], axis=0)))
    h1 = each(lambda p: _dot_tn(kbd_s[p], jnp.concatenate([v_s[p], -u[p]], axis=0)))
    for p in pairs:
        y_ref[p] = y[p][:c] + y[p][c:]
        w_col = jnp.sum(eye * jnp.exp(tot[p]), axis=1, keepdims=True)
        h_ref[p] = h0[p] * w_col + h1[p]


def rwkv_chunk_scan(r, v, kk, lw, kd, bd, n_ctx):
    npair, n, _ = r.shape
    nc = n // CHUNK
    ncc = n_ctx // CHUNK

    def cidx(d, j):
        bwd = jnp.where(j < ncc, ncc - 1 - j, nc - 1 - (j - ncc))
        return jnp.where(d == 0, j, bwd)

    shared = pl.BlockSpec((npair, CHUNK, LANES), lambda d, j: (0, cidx(d, j), 0))
    per_dir = pl.BlockSpec((None, npair, CHUNK, LANES), lambda d, j: (d, 0, cidx(d, j), 0))
    return pl.pallas_call(
        functools.partial(_rwkv_chunk_body, n_pairs=npair),
        grid=(2, nc),
        in_specs=[shared, shared, shared, per_dir, per_dir, per_dir],
        out_specs=per_dir,
        out_shape=jax.ShapeDtypeStruct((2, npair, n, LANES), f32),
        scratch_shapes=[pltpu.VMEM((npair, LANES, LANES), f32)],
        compiler_params=_params(("arbitrary", "arbitrary")),
        name="rwkv_chunk_scan",
    )(r, v, kk, lw, kd, bd)


def _rwkv_readout_body(y_ref, bonus_ref, g_ref, lng_ref, lnb_ref, e_ref, o_ref, *, n_pairs):
    inv = 1.0 / A_HEAD_DIM
    for p in range(n_pairs):
        y = y_ref[0, p] + y_ref[1, p]
        yc = y - _seg_sum(y, e_ref) * inv
        var = _seg_sum(yc * yc, e_ref) * inv
        yn = yc * lax.rsqrt(var + GN_EPS) * lng_ref[p] + lnb_ref[p]
        o_ref[:, p * LANES:(p + 1) * LANES] = ((yn + bonus_ref[p]) * g_ref[p]).astype(o_ref.dtype)


def rwkv_readout(y, bonus, g, ln_g, ln_b, seg_pair):
    _, npair, n, _ = y.shape
    pm_spec = pl.BlockSpec((npair, ROW_TILE, LANES), lambda i: (0, i, 0))
    vec_spec = pl.BlockSpec((npair, 1, LANES), lambda i: (0, 0, 0))
    return pl.pallas_call(
        functools.partial(_rwkv_readout_body, n_pairs=npair),
        grid=(n // ROW_TILE,),
        in_specs=[pl.BlockSpec((2, npair, ROW_TILE, LANES), lambda i: (0, 0, i, 0)), pm_spec, pm_spec,
                  vec_spec, vec_spec, pl.BlockSpec((LANES, LANES), lambda i: (0, 0))],
        out_specs=pl.BlockSpec((ROW_TILE, npair * LANES), lambda i: (i, 0)),
        out_shape=jax.ShapeDtypeStruct((n, npair * LANES), bf16),
        compiler_params=_params(("arbitrary",)),
        name="rwkv_readout",
    )(y, bonus, g, ln_g.reshape(npair, 1, LANES), ln_b.reshape(npair, 1, LANES), seg_pair)


def _mla_prep_body(u_ref, cs_ref, gq_ref, gkv_ref, wqn_ref, wqp_ref, wqr_ref, wkv_ref, q_o, k_o, v_o,
                   *, n_heads, c_kvd, c_q, c_kr):
    ql = u_ref[:, c_q:c_q + 512]
    qn = ql * lax.rsqrt(jnp.sum(ql * ql, axis=-1, keepdims=True) * (1.0 / Q_LORA) + RMS_EPS) * gq_ref[...]
    qn = qn.astype(bf16)
    kvd = u_ref[:, c_kvd:c_kvd + KV_LORA]
    kvn = kvd * lax.rsqrt(jnp.mean(kvd * kvd, axis=-1, keepdims=True) + RMS_EPS) * gkv_ref[...]
    kv = _dot(kvn.astype(bf16), wkv_ref[...])
    cs = cs_ref[...]
    lane = lax.broadcasted_iota(jnp.int32, (1, LANES), 1)
    low = lane < QK_ROPE
    cos_t = jnp.where(low, cs, 0.0)
    sin_t = jnp.where(low, pltpu.roll(cs, QK_ROPE, 1), 0.0)
    kr = u_ref[:, c_kr:c_kr + LANES]
    prod = kr * cs
    k_pe = jnp.where(low, prod + pltpu.roll(prod, QK_ROPE, 1), 0.0).astype(bf16)
    q_nope = _dot(qn, wqn_ref[...])
    q_pe = _dot(qn, wqp_ref[...])
    q_pr = _dot(qn, wqr_ref[...])
    for h in range(n_heads):
        sl = slice(h * LANES, (h + 1) * LANES)
        q_o[h, :, 0:LANES] = (q_nope[:, sl] * MLA_Q_SCALE).astype(bf16)
        q_o[h, :, LANES:2 * LANES] = ((q_pe[:, sl] * cos_t + q_pr[:, sl] * sin_t) * MLA_Q_SCALE).astype(bf16)
        k_o[h, :, 0:LANES] = kv[:, sl].astype(bf16)
        k_o[h, :, LANES:2 * LANES] = k_pe
        v_o[h] = kv[:, n_heads * LANES + h * LANES:n_heads * LANES + (h + 1) * LANES].astype(bf16)


def mla_prepare(ub, cs, pw, cols):
    n = ub.shape[0]
    nh = pw["wq_nope"].shape[1] // LANES
    full = lambda a: pl.BlockSpec(a.shape, lambda i: (0,) * a.ndim)
    consts = [pw["gq"], pw["gkv"], pw["wq_nope"], pw["wq_pe"], pw["wq_pr"], pw["wkv"]]
    return pl.pallas_call(
        functools.partial(_mla_prep_body, n_heads=nh, **cols),
        grid=(n // ROW_TILE,),
        in_specs=[pl.BlockSpec((ROW_TILE, ub.shape[1]), lambda i: (i, 0)),
                  pl.BlockSpec((ROW_TILE, LANES), lambda i: (i, 0))] + [full(a) for a in consts],
        out_specs=[pl.BlockSpec((nh, ROW_TILE, 2 * LANES), lambda i: (0, i, 0)),
                   pl.BlockSpec((nh, ROW_TILE, 2 * LANES), lambda i: (0, i, 0)),
                   pl.BlockSpec((nh, ROW_TILE, LANES), lambda i: (0, i, 0))],
        out_shape=[jax.ShapeDtypeStruct((nh, n, 2 * LANES), bf16),
                   jax.ShapeDtypeStruct((nh, n, 2 * LANES), bf16),
                   jax.ShapeDtypeStruct((nh, n, LANES), bf16)],
        compiler_params=_params(("arbitrary",)),
        name="mla_prepare",
    )(ub, cs, *consts)


def _mla_attn_body(q_ref, k_ref, v_ref, o_ref, *, n_ctx, n_tok, tk):
    i = pl.program_id(1)
    hs = range(MLA_HEADS_PER_STEP)
    each = lambda f: [f(h) for h in hs]

    def write(o):
        for h in hs:
            o_ref[:, h * V_DIM:(h + 1) * V_DIM] = o[h].astype(o_ref.dtype)

    @pl.when(i == 0)
    def _():
        s = each(lambda h: _dot_nt(q_ref[h], k_ref[h, 0:n_ctx, :]))
        p = each(lambda h: jnp.exp2(s[h] - jnp.max(s[h], axis=-1, keepdims=True)))
        pv = each(lambda h: _dot(p[h].astype(bf16), v_ref[h, 0:n_ctx, :]))
        write(each(lambda h: pv[h] / jnp.sum(p[h], axis=-1, keepdims=True)))

    @pl.when(i > 0)
    def _():
        q = each(lambda h: q_ref[h])

        def step(c, carry):
            off = pl.multiple_of(c * tk, tk)
            s = each(lambda h: _dot_nt(q[h], k_ref[h, pl.ds(off, tk), :]))
            m_new = each(lambda h: jnp.maximum(carry[h][0], jnp.max(s[h], axis=-1, keepdims=True)))
            alpha = each(lambda h: jnp.exp2(carry[h][0] - m_new[h]))
            p = each(lambda h: jnp.exp2(s[h] - m_new[h]))
            l = each(lambda h: alpha[h] * carry[h][1] + jnp.sum(p[h], axis=-1, keepdims=True))
            pv = each(lambda h: _dot(p[h].astype(bf16), v_ref[h, pl.ds(off, tk), :]))
            return tuple((m_new[h], l[h], alpha[h] * carry[h][2] + pv[h]) for h in hs)

        init = tuple((jnp.full((ROW_TILE, 1), NEG_BIG, f32), jnp.zeros((ROW_TILE, 1), f32),
                      jnp.zeros((ROW_TILE, V_DIM), f32)) for _ in hs)
        fin = lax.fori_loop(0, n_tok // tk, step, init)
        write(each(lambda h: fin[h][2] / fin[h][1]))


def mla_attention(q, k, v, n_ctx):
    nh, n, dq = q.shape
    hps = MLA_HEADS_PER_STEP
    tk = next(t for t in (768, 512, 256) if n % t == 0)
    return pl.pallas_call(
        functools.partial(_mla_attn_body, n_ctx=n_ctx, n_tok=n, tk=tk),
        grid=(nh // hps, n // ROW_TILE),
        in_specs=[
            pl.BlockSpec((hps, ROW_TILE, dq), lambda h, i: (h, i, 0)),
            pl.BlockSpec((hps, n, dq), lambda h, i: (h, 0, 0)),
            pl.BlockSpec((hps, n, V_DIM), lambda h, i: (h, 0, 0)),
        ],
        out_specs=pl.BlockSpec((ROW_TILE, hps * V_DIM), lambda h, i: (i, h)),
        out_shape=jax.ShapeDtypeStruct((n, nh * V_DIM), bf16),
        compiler_params=_params(("arbitrary", "arbitrary")),
        name="mla_attention",
    )(q, k, v)


def _na_body(q_ref, k_ref, v_ref, bias_ref, o_ref, *, n_ctx, n_rows):
    j = pl.program_id(1)
    lane = lax.broadcasted_iota(jnp.int32, (1, LANES), 1)
    low = lane < C_HEAD_DIM
    kc = k_ref[0:n_ctx, :]
    vc = v_ref[0:n_ctx, :]
    win = WIN_ROWS * GRID_W

    @pl.when(j == 0)
    def _():
        q2 = q_ref[...]
        outs = []
        for hh in range(2):
            qm = jnp.where(low if hh == 0 else ~low, q2, jnp.zeros_like(q2))
            s = _dot_nt(qm, kc) * NA_SCALE
            p = jnp.exp(s - jnp.max(s, axis=-1, keepdims=True))
            outs.append(_dot(p.astype(bf16), vc) / jnp.sum(p, axis=-1, keepdims=True))
        o_ref[...] = jnp.where(low, outs[0], outs[1]).astype(o_ref.dtype)

    @pl.when(j > 0)
    def _():
        grid_rows = range(ROW_TILE // GRID_W)
        units = [(il, hh) for il in grid_rows for hh in range(2)]
        per_row = lambda f: [f(il) for il in grid_rows]
        per_unit = lambda f: [f(n, il, hh) for n, (il, hh) in enumerate(units)]
        i = per_row(lambda il: (j - 1) * (ROW_TILE // GRID_W) + il)
        r0 = per_row(lambda il: jnp.clip(i[il] - WIN_ROWS // 2, 0, n_rows - WIN_ROWS))
        d0 = per_row(lambda il: r0[il] - i[il] + WIN_ROWS - 1)
        off = per_row(lambda il: pl.multiple_of(n_ctx + r0[il] * GRID_W, GRID_W))
        q2 = per_row(lambda il: q_ref[il * GRID_W:(il + 1) * GRID_W, :] * NA_SCALE)
        kw = per_row(lambda il: k_ref[pl.ds(off[il], win), :])
        vw = per_row(lambda il: v_ref[pl.ds(off[il], win), :])
        qm = per_unit(lambda n, il, hh: jnp.where(low if hh == 0 else ~low, q2[il], jnp.zeros_like(q2[il])))
        s_w = per_unit(lambda n, il, hh: _dot_nt(qm[n], kw[il]) + bias_ref[hh, d0[il]])
        s_c = per_unit(lambda n, il, hh: _dot_nt(qm[n], kc))
        m = per_unit(lambda n, il, hh: jnp.maximum(jnp.max(s_w[n], axis=-1, keepdims=True),
                                                   jnp.max(s_c[n], axis=-1, keepdims=True)))
        p_w = per_unit(lambda n, il, hh: jnp.exp(s_w[n] - m[n]))
        p_c = per_unit(lambda n, il, hh: jnp.exp(s_c[n] - m[n]))
        l = per_unit(lambda n, il, hh: jnp.sum(p_w[n], axis=-1, keepdims=True)
                     + jnp.sum(p_c[n], axis=-1, keepdims=True))
        o = per_unit(lambda n, il, hh: (_dot(p_w[n].astype(bf16), vw[il]) + _dot(p_c[n].astype(bf16), vc)) / l[n])
        for il in grid_rows:
            o_ref[il * GRID_W:(il + 1) * GRID_W, :] = jnp.where(low, o[2 * il], o[2 * il + 1]).astype(o_ref.dtype)


def na_attention(uc, bias, n_ctx):
    n = uc.shape[0]
    width = uc.shape[1] // 3
    npair = width // LANES
    n_rows = (n - n_ctx) // GRID_W
    return pl.pallas_call(
        functools.partial(_na_body, n_ctx=n_ctx, n_rows=n_rows),
        grid=(npair, n // ROW_TILE),
        in_specs=[
            pl.BlockSpec((ROW_TILE, LANES), lambda p, j: (j, p)),
            pl.BlockSpec((n, LANES), lambda p, j: (0, npair + p)),
            pl.BlockSpec((n, LANES), lambda p, j: (0, 2 * npair + p)),
            pl.BlockSpec((2, WIN_ROWS, GRID_W, WIN_ROWS * GRID_W), lambda p, j: (p, 0, 0, 0)),
        ],
        out_specs=pl.BlockSpec((ROW_TILE, LANES), lambda p, j: (j, p)),
        out_shape=jax.ShapeDtypeStruct((n, width), bf16),
        compiler_params=_params(("arbitrary", "arbitrary")),
        name="na_attention",
    )(uc, uc, uc, bias)


def _rope_tables(n_ctx, seq):
    t = jnp.arange(seq)
    row = (t // GRID_W).astype(f32)
    col = (t % GRID_W).astype(f32)
    n_freq = QK_ROPE // 4
    inv = ROPE_BASE ** (-jnp.arange(n_freq, dtype=f32) / n_freq)
    ar, ac = row[:, None] * inv[None, :], col[:, None] * inv[None, :]
    cos = jnp.concatenate([jnp.cos(ar), jnp.cos(ar), jnp.cos(ac), jnp.cos(ac)], axis=1)
    sin = jnp.concatenate([-jnp.sin(ar), jnp.sin(ar), -jnp.sin(ac), jnp.sin(ac)], axis=1)
    lat = jnp.concatenate([cos, sin], axis=1)
    ctx = jnp.concatenate([jnp.ones((n_ctx, QK_ROPE), f32), jnp.zeros((n_ctx, QK_ROPE), f32)], axis=1)
    return jnp.concatenate([ctx, lat], axis=0)


def _pair_swap_perm():
    q = QK_ROPE // 4
    return np.concatenate([np.arange(q, 2 * q), np.arange(0, q), np.arange(3 * q, 4 * q), np.arange(2 * q, 3 * q)])


def _na_bias_tables(rpb):
    cols = np.arange(GRID_W)
    c0 = np.clip(cols - WIN_COLS // 2, 0, GRID_W - WIN_COLS)
    inside = (cols[None, :] >= c0[:, None]) & (cols[None, :] < c0[:, None] + WIN_COLS)
    rel = np.clip(cols[None, :] - cols[:, None] + WIN_COLS - 1, 0, 2 * WIN_COLS - 2)
    full = jnp.where(inside[None, None], rpb[:, :, rel], NEG_BIG)
    tabs = [jnp.concatenate([full[:, d0 + jj] for jj in range(WIN_ROWS)], axis=-1) for d0 in range(WIN_ROWS)]
    return jnp.stack(tabs, axis=1).astype(f32)


def _layer_weights(l, w_in, rwkv_conv, rwkv_w0, rwkv_w_up, rwkv_a0, rwkv_a_up, rwkv_g_up, rwkv_k_k, rwkv_k_a,
                   rwkv_r_k, mla_q_norm_g, mla_w_uq, mla_kv_norm_g, mla_w_ukv, w_branch, w_out, mlp_w1, mlp_w2):
    d = w_in.shape[1]
    aw = rwkv_k_k.shape[1]
    nh_b = mla_w_ukv.shape[2] // (QK_NOPE + V_DIM)
    n_lora = 2 * LORA_W + 2 * LORA_A + LORA_G
    c_lora = 3 * aw
    c_mla = c_lora + n_lora
    c_na = c_mla + Q_LORA + KV_LORA + QK_ROPE
    c_gate = c_na + 3 * aw
    wi = w_in[l]
    perm = _pair_swap_perm()
    kr = wi[:, c_mla + Q_LORA + KV_LORA:c_na]
    w_b = jnp.concatenate([
        wi[:, c_lora:c_mla],
        wi[:, c_mla + Q_LORA:c_mla + Q_LORA + KV_LORA],
        wi[:, c_mla:c_mla + Q_LORA], jnp.zeros((d, 512 - Q_LORA), f32),
        kr, kr[:, perm]], axis=1).astype(bf16)
    cols = dict(c_kvd=n_lora, c_q=n_lora + KV_LORA, c_kr=n_lora + KV_LORA + 512)

    def lora_pad(w, start):
        r = w.shape[1]
        out = jnp.zeros((2, n_lora, aw), f32)
        for z in range(2):
            out = out.at[z, start + z * r:start + (z + 1) * r].set(w[z])
        return out.astype(bf16)

    g_up = jnp.zeros((n_lora, aw), f32).at[2 * LORA_W + 2 * LORA_A:].set(rwkv_g_up[l]).astype(bf16)
    head_id = np.arange(aw) // A_HEAD_DIM
    seg = jnp.asarray(head_id[:, None] == head_id[None, :], bf16)
    rwkv = dict(conv_rkv=rwkv_conv[l][:, :c_lora], conv_l=rwkv_conv[l][:, c_lora:c_mla],
                w_up=lora_pad(rwkv_w_up[l], 0), a_up=lora_pad(rwkv_a_up[l], 2 * LORA_W), g_up=g_up,
                w0=rwkv_w0[l], a0=rwkv_a0[l], k_k=rwkv_k_k[l][None], k_a=rwkv_k_a[l][None],
                r_k=rwkv_r_k[l].reshape(1, aw), seg=seg)

    uq = mla_w_uq[l].reshape(Q_LORA, nh_b, QK_NOPE + QK_ROPE)
    uq = jnp.concatenate([uq, jnp.zeros((512 - Q_LORA, nh_b, QK_NOPE + QK_ROPE), f32)], axis=0)
    zpad = jnp.zeros((512, nh_b, LANES - QK_ROPE), f32)
    pe = uq[:, :, QK_NOPE:]
    ukv = mla_w_ukv[l].reshape(KV_LORA, nh_b, QK_NOPE + V_DIM)
    mla = dict(
        gq=jnp.concatenate([mla_q_norm_g[l], jnp.zeros((512 - Q_LORA,), f32)])[None],
        gkv=mla_kv_norm_g[l][None],
        wq_nope=uq[:, :, :QK_NOPE].reshape(512, nh_b * QK_NOPE).astype(bf16),
        wq_pe=jnp.concatenate([pe, zpad], axis=2).reshape(512, nh_b * LANES).astype(bf16),
        wq_pr=jnp.concatenate([pe[:, :, perm], zpad], axis=2).reshape(512, nh_b * LANES).astype(bf16),
        wkv=jnp.concatenate([ukv[:, :, :QK_NOPE].reshape(KV_LORA, -1), ukv[:, :, QK_NOPE:].reshape(KV_LORA, -1)],
                            axis=1).astype(bf16))
    return dict(w_rkv=wi[:, :c_lora].astype(bf16), w_b=w_b, w_na=wi[:, c_na:c_gate].astype(bf16),
                w_gate=wi[:, c_gate:].astype(bf16), cols=cols, rwkv=rwkv, mla=mla,
                w_branch=w_branch[l].astype(bf16), w_out=w_out[l].astype(bf16),
                mlp_w1=mlp_w1[l].astype(bf16), mlp_w2=mlp_w2[l].astype(bf16))


def kernel(x, c, ctx, c_ctx, ada_w, ada_b, norm_mix_g, norm_mlp_g, w_in, rwkv_conv, rwkv_w0, rwkv_w_up, rwkv_a0, rwkv_a_up, rwkv_g_up, rwkv_k_k, rwkv_k_a, rwkv_r_k, rwkv_ln_g, rwkv_ln_b, mla_q_norm_g, mla_w_uq, mla_kv_norm_g, mla_w_ukv, na_rpb, w_branch, w_out, mlp_w1, mlp_w2, final_norm_g):
    batch, seq, d = x.shape
    n_ctx = ctx.shape[1]
    depth = ada_w.shape[0]
    assert batch == 1 and n_ctx == ROW_TILE and seq % ROW_TILE == 0 and seq // GRID_W >= WIN_ROWS
    n_ctx_tiles = n_ctx // ROW_TILE

    xs = jnp.concatenate([ctx[0], x[0]], axis=0)
    cond = jnp.concatenate([c, c_ctx[None], jnp.zeros((6, d), f32)], axis=0)
    mods = ada_modulation(cond, ada_w, ada_b)
    cs = _rope_tables(n_ctx, seq)

    for l in range(depth):
        lw = _layer_weights(l, w_in, rwkv_conv, rwkv_w0, rwkv_w_up, rwkv_a0, rwkv_a_up, rwkv_g_up, rwkv_k_k,
                            rwkv_k_a, rwkv_r_k, mla_q_norm_g, mla_w_uq, mla_kv_norm_g, mla_w_ukv, w_branch,
                            w_out, mlp_w1, mlp_w2)
        mods8 = mods[l]
        mods2 = mods8[:2, None, :]
        h = norm_modulate(xs, norm_mix_g[l], mods2, 0)
        u3 = matmul(h, lw["w_rkv"], f32, 512)
        ub = matmul(h, lw["w_b"], f32, lw["w_b"].shape[1])
        uc = matmul(h, lw["w_na"], bf16, 512)
        ug = matmul(h, lw["w_gate"], f32, 512)

        r, v, kk, bonus, g, lwd, kd, bd = rwkv_prepare(u3, ub, lw["rwkv"], n_ctx)
        y = rwkv_chunk_scan(r, v, kk, lwd, kd, bd, n_ctx)
        ya = rwkv_readout(y, bonus, g, rwkv_ln_g[l], rwkv_ln_b[l], lw["rwkv"]["seg"][:LANES, :LANES])

        q_b, k_b, v_b = mla_prepare(ub, cs, lw["mla"], lw["cols"])
        yb = mla_attention(q_b, k_b, v_b, n_ctx)

        yc = na_attention(uc, _na_bias_tables(na_rpb[l]), n_ctx)

        merged = merge_branches(ya, yb, yc, ug, lw["w_branch"])
        xs = matmul_gated_residual(merged, lw["w_out"], xs, mods8, 2, n_ctx)
        h2 = norm_modulate(xs, norm_mlp_g[l], mods2, 3)
        xs = mlp_gated_residual(h2, lw["mlp_w1"], lw["mlp_w2"], xs, mods8, 5, n_ctx)

    return final_norm(xs, final_norm_g, n_ctx_tiles)[None]
```

```python
import functools
import math

import jax
import jax.numpy as jnp
import numpy as np
from jax import lax
from jax.experimental import pallas as pl
from jax.experimental.pallas import tpu as pltpu

f32 = jnp.float32
bf16 = jnp.bfloat16
HIGHEST = lax.Precision.HIGHEST

GRID_W = 64
RMS_EPS = 1e-6
N_BRANCH = 3
A_HEAD_DIM = 64
LORA_W = 96
LORA_A = 96
LORA_G = 256
GN_EPS = 64e-5
QK_NOPE = 128
QK_ROPE = 64
V_DIM = 128
Q_LORA = 448
KV_LORA = 128
MLA_SCALE = (QK_NOPE + QK_ROPE) ** -0.5
MLA_Q_SCALE = MLA_SCALE * math.log2(math.e)
ROPE_BASE = 10000.0
C_HEAD_DIM = 64
WIN_ROWS = 8
WIN_COLS = 16
NA_SCALE = C_HEAD_DIM ** -0.5
DECAY_SCALE = math.exp(-0.5)

LANES = 128
ROW_TILE = 256
CHUNK = 64
PREP_TILE = 128
MLA_HEADS_PER_STEP = 2
VMEM_LIMIT = 48 * 1024 * 1024
NEG_BIG = -1e30


def _params(sem):
    return pltpu.CompilerParams(dimension_semantics=sem, vmem_limit_bytes=VMEM_LIMIT)


def _dot(a, b):
    return jnp.dot(a, b, preferred_element_type=f32)


def _dot_nt(a, b):
    return lax.dot_general(a, b, (((1,), (1,)), ((), ())), preferred_element_type=f32)


def _ada_body(s_ref, w_ref, b_ref, o_ref):
    s = s_ref[...]
    s = s * jax.nn.sigmoid(s)
    o_ref[...] = _dot(s, w_ref[...]) + b_ref[...]


def ada_modulation(cond, ada_w, ada_b):
    nl, d, n6 = ada_w.shape
    tn = 1024
    return pl.pallas_call(
        _ada_body,
        grid=(nl, n6 // tn),
        in_specs=[
            pl.BlockSpec((8, d), lambda l, j: (0, 0)),
            pl.BlockSpec((None, d, tn), lambda l, j: (l, 0, j)),
            pl.BlockSpec((None, 1, tn), lambda l, j: (l, 0, j)),
        ],
        out_specs=pl.BlockSpec((None, 8, tn), lambda l, j: (l, 0, j)),
        out_shape=jax.ShapeDtypeStruct((nl, 8, n6), f32),
        compiler_params=_params(("arbitrary", "arbitrary")),
        name="ada_modulation",
    )(cond, ada_w, ada_b.reshape(nl, 1, n6))


def _norm_mod_body(x_ref, g_ref, m_ref, o_ref, *, off, d):
    x = x_ref[...]
    y = x * lax.rsqrt(jnp.mean(x * x, axis=-1, keepdims=True) + RMS_EPS) * g_ref[...]
    shift = m_ref[:, off * d:(off + 1) * d]
    scale = m_ref[:, (off + 1) * d:(off + 2) * d]
    o_ref[...] = (y * (1.0 + scale) + shift).astype(o_ref.dtype)


def norm_modulate(x, g, mods2, off):
    n, d = x.shape
    return pl.pallas_call(
        functools.partial(_norm_mod_body, off=off, d=d),
        grid=(n // ROW_TILE,),
        in_specs=[
            pl.BlockSpec((ROW_TILE, d), lambda i: (i, 0)),
            pl.BlockSpec((1, d), lambda i: (0, 0)),
            pl.BlockSpec((None, 1, mods2.shape[-1]), lambda i: (jnp.where(i == 0, 1, 0), 0, 0)),
        ],
        out_specs=pl.BlockSpec((ROW_TILE, d), lambda i: (i, 0)),
        out_shape=jax.ShapeDtypeStruct((n, d), bf16),
        compiler_params=_params(("arbitrary",)),
        name="norm_modulate",
    )(x, g.reshape(1, d), mods2)


def _final_norm_body(x_ref, g_ref, o_ref):
    x = x_ref[...]
    o_ref[...] = x * lax.rsqrt(jnp.mean(x * x, axis=-1, keepdims=True) + RMS_EPS) * g_ref[...]


def final_norm(x, g, n_ctx_tiles):
    n, d = x.shape
    t = n - n_ctx_tiles * ROW_TILE
    return pl.pallas_call(
        _final_norm_body,
        grid=(t // ROW_TILE,),
        in_specs=[
            pl.BlockSpec((ROW_TILE, d), lambda i: (i + n_ctx_tiles, 0)),
            pl.BlockSpec((1, d), lambda i: (0, 0)),
        ],
        out_specs=pl.BlockSpec((ROW_TILE, d), lambda i: (i, 0)),
        out_shape=jax.ShapeDtypeStruct((t, d), f32),
        compiler_params=_params(("arbitrary",)),
        name="final_norm",
    )(x, g.reshape(1, d))


def _mm_body(a_ref, w_ref, o_ref):
    o_ref[...] = _dot(a_ref[...], w_ref[...]).astype(o_ref.dtype)


def _row_tile(n):
    for tm in (1056, 768, 512, 384, 256):
        if n % tm == 0:
            return tm
    raise ValueError(f"no row tile for {n} rows")


def matmul(a, w, out_dtype, tn):
    m, k = a.shape
    n = w.shape[1]
    tm = _row_tile(m)
    return pl.pallas_call(
        _mm_body,
        grid=(m // tm, n // tn),
        in_specs=[pl.BlockSpec((tm, k), lambda i, j: (i, 0)), pl.BlockSpec((k, tn), lambda i, j: (0, j))],
        out_specs=pl.BlockSpec((tm, tn), lambda i, j: (i, j)),
        out_shape=jax.ShapeDtypeStruct((m, n), out_dtype),
        compiler_params=_params(("arbitrary", "arbitrary")),
        name="matmul",
    )(a, w)


def _gate_rows(m_ref, tm, n_ctx):
    rows = pl.program_id(0) * tm + lax.broadcasted_iota(jnp.int32, (tm, 1), 0)
    return jnp.where(rows < n_ctx, m_ref[1:2, :], m_ref[0:1, :])


def _mm_res_body(a_ref, w_ref, x_ref, m_ref, o_ref, *, tm, n_ctx):
    acc = _dot(a_ref[...], w_ref[...])
    o_ref[...] = x_ref[...] + _gate_rows(m_ref, tm, n_ctx) * acc


def matmul_gated_residual(a, w, x, mods8, gate_off, n_ctx, tn=512):
    m, k = a.shape
    n = w.shape[1]
    tm = _row_tile(m)
    nj = n // tn
    return pl.pallas_call(
        functools.partial(_mm_res_body, tm=tm, n_ctx=n_ctx),
        grid=(m // tm, nj),
        in_specs=[
            pl.BlockSpec((tm, k), lambda i, j: (i, 0)),
            pl.BlockSpec((k, tn), lambda i, j: (0, j)),
            pl.BlockSpec((tm, tn), lambda i, j: (i, j)),
            pl.BlockSpec((8, tn), lambda i, j: (0, gate_off * nj + j)),
        ],
        out_specs=pl.BlockSpec((tm, tn), lambda i, j: (i, j)),
        out_shape=jax.ShapeDtypeStruct((m, n), f32),
        compiler_params=_params(("arbitrary", "arbitrary")),
        name="matmul_gated_residual",
    )(a, w, x, mods8)


def _merge_body(ya_ref, yb_ref, yc_ref, g0_ref, g1_ref, g2_ref, w_ref, o_ref):
    acc = jax.nn.sigmoid(g0_ref[...]) * _dot(ya_ref[...], w_ref[0])
    acc = acc + jax.nn.sigmoid(g1_ref[...]) * _dot(yb_ref[...], w_ref[1])
    acc = acc + jax.nn.sigmoid(g2_ref[...]) * _dot(yc_ref[...], w_ref[2])
    o_ref[...] = acc.astype(o_ref.dtype)


def merge_branches(ya, yb, yc, ug, w_branch, tn=512):
    m, k = ya.shape
    d = w_branch.shape[-1]
    tm = 528 if m % 528 == 0 else ROW_TILE
    nj = d // tn
    y_spec = pl.BlockSpec((tm, k), lambda i, j: (i, 0))
    return pl.pallas_call(
        _merge_body,
        grid=(m // tm, nj),
        in_specs=[
            y_spec, y_spec, y_spec,
            pl.BlockSpec((tm, tn), lambda i, j: (i, j)),
            pl.BlockSpec((tm, tn), lambda i, j: (i, nj + j)),
            pl.BlockSpec((tm, tn), lambda i, j: (i, 2 * nj + j)),
            pl.BlockSpec((N_BRANCH, k, tn), lambda i, j: (0, 0, j)),
        ],
        out_specs=pl.BlockSpec((tm, tn), lambda i, j: (i, j)),
        out_shape=jax.ShapeDtypeStruct((m, d), bf16),
        compiler_params=_params(("arbitrary", "arbitrary")),
        name="merge_branches",
    )(ya, yb, yc, ug, ug, ug, w_branch)


def _mlp_body(h_ref, w1_ref, w2_ref, x_ref, m_ref, o_ref, acc_ref, *, tm, n_ctx):
    f = pl.program_id(1)

    @pl.when(f == 0)
    def _():
        acc_ref[...] = jnp.zeros_like(acc_ref)

    h1 = jnp.maximum(_dot(h_ref[...], w1_ref[...]), 0.0)
    acc_ref[...] += _dot((h1 * h1).astype(bf16), w2_ref[...])

    @pl.when(f == pl.num_programs(1) - 1)
    def _():
        o_ref[...] = x_ref[...] + _gate_rows(m_ref, tm, n_ctx) * acc_ref[...]


def mlp_gated_residual(h, w1, w2, x, mods8, gate_off, n_ctx, tf=512):
    m, d = h.shape
    dff = w1.shape[1]
    tm = 528 if m % 528 == 0 else ROW_TILE
    return pl.pallas_call(
        functools.partial(_mlp_body, tm=tm, n_ctx=n_ctx),
        grid=(m // tm, dff // tf),
        in_specs=[
            pl.BlockSpec((tm, d), lambda i, f: (i, 0)),
            pl.BlockSpec((d, tf), lambda i, f: (0, f)),
            pl.BlockSpec((tf, d), lambda i, f: (f, 0)),
            pl.BlockSpec((tm, d), lambda i, f: (i, 0)),
            pl.BlockSpec((8, d), lambda i, f: (0, gate_off)),
        ],
        out_specs=pl.BlockSpec((tm, d), lambda i, f: (i, 0)),
        out_shape=jax.ShapeDtypeStruct((m, d), f32),
        scratch_shapes=[pltpu.VMEM((tm, d), f32)],
        compiler_params=_params(("arbitrary", "arbitrary")),
        name="mlp_gated_residual",
    )(h, w1, w2, x, mods8)


def _seg_sum(x, e_ref):
    hi = x.astype(bf16)
    lo = (x - hi.astype(f32)).astype(bf16)
    e = e_ref[...]
    return _dot(hi, e) + _dot(lo, e)


def _to_pairs(o_ref, val, lead=None):
    for p in range(val.shape[-1] // LANES):
        piece = val[:, p * LANES:(p + 1) * LANES]
        if lead is None:
            o_ref[p] = piece
        else:
            o_ref[lead, p] = piece


def _rwkv_prep_body(u_ref, up_ref, un_ref, l_ref, lp_ref, ln_ref, cw_ref, cwl_ref, wu_ref, au_ref, gu_ref,
                    w0_ref, a0_ref, kk_ref, ka_ref, rk_ref, e_ref,
                    r_o, v_o, kkn_o, bonus_o, g_o, lw_o, kd_o, bd_o, *, n_ctx_tiles, aw):
    i = pl.program_id(0)
    last = pl.num_programs(0) - 1
    left_zero = (i == 0) | (i == n_ctx_tiles)
    right_zero = (i == n_ctx_tiles - 1) | (i == last)
    rows = lax.broadcasted_iota(jnp.int32, (PREP_TILE, 1), 0)

    def conv(x_ref, xp_ref, xn_ref, w_ref):
        x = x_ref[...]
        prev_row = jnp.where(left_zero, 0.0, xp_ref[7:8, :])
        next_row = jnp.where(right_zero, 0.0, xn_ref[0:1, :])
        x_prev = jnp.where(rows == 0, prev_row, pltpu.roll(x, 1, 0))
        x_next = jnp.where(rows == PREP_TILE - 1, next_row, pltpu.roll(x, PREP_TILE - 1, 0))
        return w_ref[0:1, :] * x_prev + w_ref[1:2, :] * x + w_ref[2:3, :] * x_next

    y = conv(u_ref, up_ref, un_ref, cw_ref)
    r, k, v = y[:, :aw], y[:, aw:2 * aw], y[:, 2 * aw:]
    yl = conv(l_ref, lp_ref, ln_ref, cwl_ref)
    th = jnp.tanh(yl).astype(bf16)
    sg = jax.nn.sigmoid(yl).astype(bf16)
    ylb = yl.astype(bf16)

    _to_pairs(g_o, _dot(sg, gu_ref[...]))
    _to_pairs(r_o, r)
    _to_pairs(v_o, v)

    kkr = k * kk_ref[...]
    norm = jnp.sqrt(_seg_sum(kkr * kkr, e_ref))
    kkn = kkr / jnp.maximum(norm, 1e-12)
    _to_pairs(kkn_o, kkn)

    ksum = None
    for dr in range(2):
        z = w0_ref[dr:dr + 1, :] + _dot(th, wu_ref[dr])
        _to_pairs(lw_o, -DECAY_SCALE * jax.nn.sigmoid(z), lead=dr)
        a = jax.nn.sigmoid(a0_ref[dr:dr + 1, :] + _dot(ylb, au_ref[dr]))
        kd = k * (1.0 + (a - 1.0) * ka_ref[...])
        _to_pairs(kd_o, kd, lead=dr)
        _to_pairs(bd_o, kkn * a, lead=dr)
        ksum = kd if ksum is None else ksum + kd
    rk = _seg_sum(r * (0.5 * ksum) * rk_ref[...], e_ref)
    _to_pairs(bonus_o, rk * v)


def rwkv_prepare(u3, ub, pw, n_ctx):
    n = u3.shape[0]
    aw = u3.shape[1] // 3
    npair = aw // LANES
    lw = pw["conv_l"].shape[1]
    tpb = PREP_TILE // 8
    nb8 = n // 8

    def prev_map(i):
        return (jnp.maximum(i * tpb - 1, 0), 0)

    def next_map(i):
        return (jnp.minimum((i + 1) * tpb, nb8 - 1), 0)

    full = lambda a: pl.BlockSpec(a.shape, lambda i: (0,) * a.ndim)
    pm = jax.ShapeDtypeStruct((npair, n, LANES), f32)
    pm2 = jax.ShapeDtypeStruct((2, npair, n, LANES), f32)
    pm_spec = pl.BlockSpec((npair, PREP_TILE, LANES), lambda i: (0, i, 0))
    pm2_spec = pl.BlockSpec((2, npair, PREP_TILE, LANES), lambda i: (0, 0, i, 0))
    consts = [pw["conv_rkv"], pw["conv_l"], pw["w_up"], pw["a_up"], pw["g_up"], pw["w0"], pw["a0"],
              pw["k_k"], pw["k_a"], pw["r_k"], pw["seg"]]
    return pl.pallas_call(
        functools.partial(_rwkv_prep_body, n_ctx_tiles=n_ctx // PREP_TILE, aw=aw),
        grid=(n // PREP_TILE,),
        in_specs=[
            pl.BlockSpec((PREP_TILE, 3 * aw), lambda i: (i, 0)),
            pl.BlockSpec((8, 3 * aw), prev_map),
            pl.BlockSpec((8, 3 * aw), next_map),
            pl.BlockSpec((PREP_TILE, lw), lambda i: (i, 0)),
            pl.BlockSpec((8, lw), prev_map),
            pl.BlockSpec((8, lw), next_map),
        ] + [full(a) for a in consts],
        out_specs=[pm_spec] * 5 + [pm2_spec] * 3,
        out_shape=[pm] * 5 + [pm2] * 3,
        compiler_params=_params(("arbitrary",)),
        name="rwkv_prepare",
    )(u3, u3, u3, ub, ub, ub, *consts)


def _dot_tn(a, b):
    return lax.dot_general(a, b, (((0,), (0,)), ((), ())), preferred_element_type=f32)


def _rwkv_chunk_body(r_ref, v_ref, kk_ref, lw_ref, k_ref, b_ref, y_ref, h_ref, *, n_pairs):
    d = pl.program_id(0)
    j = pl.program_id(1)
    c = CHUNK
    c2 = 2 * c

    @pl.when(j == 0)
    def _():
        h_ref[...] = jnp.zeros_like(h_ref)

    sign = jnp.where(d == 0, 1, -1)
    ti = lax.broadcasted_iota(jnp.int32, (c, c), 0)
    si = lax.broadcasted_iota(jnp.int32, (c, c), 1)
    incl_c = jnp.where((ti - si) * sign >= 0, 1.0, 0.0).astype(bf16)
    t2 = lax.broadcasted_iota(jnp.int32, (c2, c2), 0)
    s2 = lax.broadcasted_iota(jnp.int32, (c2, c2), 1)
    ahead = (t2 - s2) * sign
    blk = lambda n: (t2 // n) == (s2 // n)
    incl = blk(c) & (ahead >= 0)
    strict = blk(c) & (ahead > 0)
    eye = jnp.where(t2 == s2, 1.0, 0.0)
    inv_levels = []
    n = 2
    while n < c:
        inv_levels.append(strict & blk(2 * n) & ~blk(n))
        n *= 2
    low = lax.broadcasted_iota(jnp.int32, (1, LANES), 1) < A_HEAD_DIM

    def stack(x):
        return jnp.concatenate([jnp.where(low, x, 0.0), jnp.where(low, 0.0, x)], axis=0).astype(bf16)

    pairs = range(n_pairs)
    each = lambda f: [f(p) for p in pairs]
    lw = each(lambda p: lw_ref[p])
    lw_hi = each(lambda p: lw[p].astype(bf16))
    lw_lo = each(lambda p: (lw[p] - lw_hi[p].astype(f32)).astype(bf16))
    cum = each(lambda p: _dot(incl_c, lw_hi[p]) + _dot(incl_c, lw_lo[p]))
    tot = each(lambda p: jnp.sum(lw[p], axis=0, keepdims=True))
    e_neg = each(lambda p: jnp.exp(-cum[p]))
    e_end = each(lambda p: jnp.exp(tot[p] - cum[p]))
    r_s = each(lambda p: stack(r_ref[p] * jnp.exp(cum[p])))
    kk_s = each(lambda p: stack(kk_ref[p] * jnp.exp(cum[p] - lw[p])))
    kb_s = each(lambda p: jnp.concatenate([stack(b_ref[p] * e_neg[p]), stack(k_ref[p] * e_neg[p])], axis=0))
    kbd_s = each(lambda p: jnp.concatenate([stack(k_ref[p] * e_end[p]), stack(b_ref[p] * e_end[p])], axis=0))
    v_s = each(lambda p: stack(v_ref[p]))
    a1 = each(lambda p: _dot_nt(kk_s[p], kb_s[p]))
    a2 = each(lambda p: _dot_nt(r_s[p], kb_s[p]))
    a_kb = each(lambda p: jnp.where(strict, a1[p][:, :c2], 0.0))
    a_kk = each(lambda p: jnp.where(strict, a1[p][:, c2:], 0.0).astype(bf16))
    a_rb = each(lambda p: jnp.where(incl, a2[p][:, :c2], 0.0).astype(bf16))
    a_rk = each(lambda p: jnp.where(incl, a2[p][:, c2:], 0.0).astype(bf16))
    t_inv = each(lambda p: eye - jnp.where(blk(2), a_kb[p], 0.0))
    for m in inv_levels:
        t_b = each(lambda p: t_inv[p].astype(bf16))
        x = each(lambda p: _dot(t_b[p], jnp.where(m, a_kb[p], 0.0).astype(bf16)).astype(bf16))
        t_inv = each(lambda p: t_inv[p] - _dot(x[p], t_b[p]))
    g = each(lambda p: _dot(t_inv[p].astype(bf16), jnp.concatenate([kk_s[p], a_kk[p]], axis=1)).astype(bf16))
    h0 = each(lambda p: h_ref[p])
    h0_b = each(lambda p: h0[p].astype(bf16))
    u = each(lambda p: _dot(g[p], jnp.concatenate([h0_b[p], v_s[p]], axis=0)).astype(bf16))
    y = each(lambda p: _dot(jnp.concatenate([r_s[p], a_rk[p], -a_rb[p]], axis=1),
                            jnp.concatenate([h0_b[p], v_s[p], u[p]], axis=0)))
    h1 = each(lambda p: _dot_tn(kbd_s[p], jnp.concatenate([v_s[p], -u[p]], axis=0)))
    for p in pairs:
        y_ref[p] = y[p][:c] + y[p][c:]
        w_col = jnp.sum(eye * jnp.exp(tot[p]), axis=1, keepdims=True)
        h_ref[p] = h0[p] * w_col + h1[p]


def rwkv_chunk_scan(r, v, kk, lw, kd, bd, n_ctx):
    npair, n, _ = r.shape
    nc = n // CHUNK
    ncc = n_ctx // CHUNK

    def cidx(d, j):
        bwd = jnp.where(j < ncc, ncc - 1 - j, nc - 1 - (j - ncc))
        return jnp.where(d == 0, j, bwd)

    shared = pl.BlockSpec((npair, CHUNK, LANES), lambda d, j: (0, cidx(d, j), 0))
    per_dir = pl.BlockSpec((None, npair, CHUNK, LANES), lambda d, j: (d, 0, cidx(d, j), 0))
    return pl.pallas_call(
        functools.partial(_rwkv_chunk_body, n_pairs=npair),
        grid=(2, nc),
        in_specs=[shared, shared, shared, per_dir, per_dir, per_dir],
        out_specs=per_dir,
        out_shape=jax.ShapeDtypeStruct((2, npair, n, LANES), f32),
        scratch_shapes=[pltpu.VMEM((npair, LANES, LANES), f32)],
        compiler_params=_params(("arbitrary", "arbitrary")),
        name="rwkv_chunk_scan",
    )(r, v, kk, lw, kd, bd)


def _rwkv_readout_body(y_ref, bonus_ref, g_ref, lng_ref, lnb_ref, e_ref, o_ref, *, n_pairs):
    inv = 1.0 / A_HEAD_DIM
    for p in range(n_pairs):
        y = y_ref[0, p] + y_ref[1, p]
        yc = y - _seg_sum(y, e_ref) * inv
        var = _seg_sum(yc * yc, e_ref) * inv
        yn = yc * lax.rsqrt(var + GN_EPS) * lng_ref[p] + lnb_ref[p]
        o_ref[:, p * LANES:(p + 1) * LANES] = ((yn + bonus_ref[p]) * g_ref[p]).astype(o_ref.dtype)


def rwkv_readout(y, bonus, g, ln_g, ln_b, seg_pair):
    _, npair, n, _ = y.shape
    pm_spec = pl.BlockSpec((npair, ROW_TILE, LANES), lambda i: (0, i, 0))
    vec_spec = pl.BlockSpec((npair, 1, LANES), lambda i: (0, 0, 0))
    return pl.pallas_call(
        functools.partial(_rwkv_readout_body, n_pairs=npair),
        grid=(n // ROW_TILE,),
        in_specs=[pl.BlockSpec((2, npair, ROW_TILE, LANES), lambda i: (0, 0, i, 0)), pm_spec, pm_spec,
                  vec_spec, vec_spec, pl.BlockSpec((LANES, LANES), lambda i: (0, 0))],
        out_specs=pl.BlockSpec((ROW_TILE, npair * LANES), lambda i: (i, 0)),
        out_shape=jax.ShapeDtypeStruct((n, npair * LANES), bf16),
        compiler_params=_params(("arbitrary",)),
        name="rwkv_readout",
    )(y, bonus, g, ln_g.reshape(npair, 1, LANES), ln_b.reshape(npair, 1, LANES), seg_pair)


def _mla_prep_body(u_ref, cs_ref, gq_ref, gkv_ref, wqn_ref, wqp_ref, wqr_ref, wkv_ref, q_o, k_o, v_o,
                   *, n_heads, c_kvd, c_q, c_kr):
    ql = u_ref[:, c_q:c_q + 512]
    qn = ql * lax.rsqrt(jnp.sum(ql * ql, axis=-1, keepdims=True) * (1.0 / Q_LORA) + RMS_EPS) * gq_ref[...]
    qn = qn.astype(bf16)
    kvd = u_ref[:, c_kvd:c_kvd + KV_LORA]
    kvn = kvd * lax.rsqrt(jnp.mean(kvd * kvd, axis=-1, keepdims=True) + RMS_EPS) * gkv_ref[...]
    kv = _dot(kvn.astype(bf16), wkv_ref[...])
    cs = cs_ref[...]
    lane = lax.broadcasted_iota(jnp.int32, (1, LANES), 1)
    low = lane < QK_ROPE
    cos_t = jnp.where(low, cs, 0.0)
    sin_t = jnp.where(low, pltpu.roll(cs, QK_ROPE, 1), 0.0)
    kr = u_ref[:, c_kr:c_kr + LANES]
    prod = kr * cs
    k_pe = jnp.where(low, prod + pltpu.roll(prod, QK_ROPE, 1), 0.0).astype(bf16)
    q_nope = _dot(qn, wqn_ref[...])
    q_pe = _dot(qn, wqp_ref[...])
    q_pr = _dot(qn, wqr_ref[...])
    for h in range(n_heads):
        sl = slice(h * LANES, (h + 1) * LANES)
        q_o[h, :, 0:LANES] = (q_nope[:, sl] * MLA_Q_SCALE).astype(bf16)
        q_o[h, :, LANES:2 * LANES] = ((q_pe[:, sl] * cos_t + q_pr[:, sl] * sin_t) * MLA_Q_SCALE).astype(bf16)
        k_o[h, :, 0:LANES] = kv[:, sl].astype(bf16)
        k_o[h, :, LANES:2 * LANES] = k_pe
        v_o[h] = kv[:, n_heads * LANES + h * LANES:n_heads * LANES + (h + 1) * LANES].astype(bf16)


def mla_prepare(ub, cs, pw, cols):
    n = ub.shape[0]
    nh = pw["wq_nope"].shape[1] // LANES
    full = lambda a: pl.BlockSpec(a.shape, lambda i: (0,) * a.ndim)
    consts = [pw["gq"], pw["gkv"], pw["wq_nope"], pw["wq_pe"], pw["wq_pr"], pw["wkv"]]
    return pl.pallas_call(
        functools.partial(_mla_prep_body, n_heads=nh, **cols),
        grid=(n // ROW_TILE,),
        in_specs=[pl.BlockSpec((ROW_TILE, ub.shape[1]), lambda i: (i, 0)),
                  pl.BlockSpec((ROW_TILE, LANES), lambda i: (i, 0))] + [full(a) for a in consts],
        out_specs=[pl.BlockSpec((nh, ROW_TILE, 2 * LANES), lambda i: (0, i, 0)),
                   pl.BlockSpec((nh, ROW_TILE, 2 * LANES), lambda i: (0, i, 0)),
                   pl.BlockSpec((nh, ROW_TILE, LANES), lambda i: (0, i, 0))],
        out_shape=[jax.ShapeDtypeStruct((nh, n, 2 * LANES), bf16),
                   jax.ShapeDtypeStruct((nh, n, 2 * LANES), bf16),
                   jax.ShapeDtypeStruct((nh, n, LANES), bf16)],
        compiler_params=_params(("arbitrary",)),
        name="mla_prepare",
    )(ub, cs, *consts)


def _mla_attn_body(q_ref, k_ref, v_ref, o_ref, *, n_ctx, n_tok, tk, tq):
    i = pl.program_id(1)
    hs = range(MLA_HEADS_PER_STEP)
    subs = range(tq // ROW_TILE)

    def rows(sb):
        return slice(sb * ROW_TILE, (sb + 1) * ROW_TILE)

    def write(chains, o):
        for (h, sb), val in zip(chains, o):
            o_ref[rows(sb), h * V_DIM:(h + 1) * V_DIM] = val.astype(o_ref.dtype)

    def context_queries():
        chains = [(h, 0) for h in hs]
        each = lambda f: [f(n, h) for n, (h, _) in enumerate(chains)]
        s = each(lambda n, h: _dot_nt(q_ref[h, rows(0), :], k_ref[h, 0:n_ctx, :]))
        p = each(lambda n, h: jnp.exp2(s[n] - jnp.max(s[n], axis=-1, keepdims=True)))
        pv = each(lambda n, h: _dot(p[n].astype(bf16), v_ref[h, 0:n_ctx, :]))
        write(chains, each(lambda n, h: pv[n] / jnp.sum(p[n], axis=-1, keepdims=True)))

    def latent_queries(chains):
        each = lambda f: [f(n, h, sb) for n, (h, sb) in enumerate(chains)]
        q = each(lambda n, h, sb: q_ref[h, rows(sb), :])

        def step(c, carry):
            off = pl.multiple_of(c * tk, tk)
            s = each(lambda n, h, sb: _dot_nt(q[n], k_ref[h, pl.ds(off, tk), :]))
            m_new = each(lambda n, h, sb: jnp.maximum(carry[n][0], jnp.max(s[n], axis=-1, keepdims=True)))
            alpha = each(lambda n, h, sb: jnp.exp2(carry[n][0] - m_new[n]))
            p = each(lambda n, h, sb: jnp.exp2(s[n] - m_new[n]))
            l = each(lambda n, h, sb: alpha[n] * carry[n][1] + jnp.sum(p[n], axis=-1, keepdims=True))
            pv = each(lambda n, h, sb: _dot(p[n].astype(bf16), v_ref[h, pl.ds(off, tk), :]))
            return tuple((m_new[n], l[n], alpha[n] * carry[n][2] + pv[n]) for n in range(len(chains)))

        init = tuple((jnp.full((ROW_TILE, 1), NEG_BIG, f32), jnp.zeros((ROW_TILE, 1), f32),
                      jnp.zeros((ROW_TILE, V_DIM), f32)) for _ in chains)
        fin = lax.fori_loop(0, n_tok // tk, step, init)
        write(chains, [f[2] / f[1] for f in fin])

    @pl.when(i == 0)
    def _():
        context_queries()
        if len(subs) > 1:
            latent_queries([(h, sb) for h in hs for sb in subs[1:]])

    @pl.when(i > 0)
    def _():
        latent_queries([(h, sb) for h in hs for sb in subs])


def mla_attention(q, k, v, n_ctx):
    nh, n, dq = q.shape
    hps = MLA_HEADS_PER_STEP
    assert n_ctx == ROW_TILE
    tk = next(t for t in (768, 512, 256) if n % t == 0)
    tq = tk
    return pl.pallas_call(
        functools.partial(_mla_attn_body, n_ctx=n_ctx, n_tok=n, tk=tk, tq=tq),
        grid=(nh // hps, n // tq),
        in_specs=[
            pl.BlockSpec((hps, tq, dq), lambda h, i: (h, i, 0)),
            pl.BlockSpec((hps, n, dq), lambda h, i: (h, 0, 0)),
            pl.BlockSpec((hps, n, V_DIM), lambda h, i: (h, 0, 0)),
        ],
        out_specs=pl.BlockSpec((tq, hps * V_DIM), lambda h, i: (i, h)),
        out_shape=jax.ShapeDtypeStruct((n, nh * V_DIM), bf16),
        compiler_params=_params(("arbitrary", "arbitrary")),
        name="mla_attention",
    )(q, k, v)


def _na_body(q_ref, k_ref, v_ref, bias_ref, o_ref, *, n_ctx, n_rows):
    j = pl.program_id(1)
    lane = lax.broadcasted_iota(jnp.int32, (1, LANES), 1)
    low = lane < C_HEAD_DIM
    kc = k_ref[0:n_ctx, :]
    vc = v_ref[0:n_ctx, :]
    win = WIN_ROWS * GRID_W

    @pl.when(j == 0)
    def _():
        q2 = q_ref[...]
        outs = []
        for hh in range(2):
            qm = jnp.where(low if hh == 0 else ~low, q2, jnp.zeros_like(q2))
            s = _dot_nt(qm, kc) * NA_SCALE
            p = jnp.exp(s - jnp.max(s, axis=-1, keepdims=True))
            outs.append(_dot(p.astype(bf16), vc) / jnp.sum(p, axis=-1, keepdims=True))
        o_ref[...] = jnp.where(low, outs[0], outs[1]).astype(o_ref.dtype)

    @pl.when(j > 0)
    def _():
        grid_rows = range(ROW_TILE // GRID_W)
        units = [(il, hh) for il in grid_rows for hh in range(2)]
        per_row = lambda f: [f(il) for il in grid_rows]
        per_unit = lambda f: [f(n, il, hh) for n, (il, hh) in enumerate(units)]
        i = per_row(lambda il: (j - 1) * (ROW_TILE // GRID_W) + il)
        r0 = per_row(lambda il: jnp.clip(i[il] - WIN_ROWS // 2, 0, n_rows - WIN_ROWS))
        d0 = per_row(lambda il: r0[il] - i[il] + WIN_ROWS - 1)
        off = per_row(lambda il: pl.multiple_of(n_ctx + r0[il] * GRID_W, GRID_W))
        q2 = per_row(lambda il: q_ref[il * GRID_W:(il + 1) * GRID_W, :] * NA_SCALE)
        kw = per_row(lambda il: k_ref[pl.ds(off[il], win), :])
        vw = per_row(lambda il: v_ref[pl.ds(off[il], win), :])
        qm = per_unit(lambda n, il, hh: jnp.where(low if hh == 0 else ~low, q2[il], jnp.zeros_like(q2[il])))
        s_w = per_unit(lambda n, il, hh: _dot_nt(qm[n], kw[il]) + bias_ref[hh, d0[il]])
        s_c = per_unit(lambda n, il, hh: _dot_nt(qm[n], kc))
        m = per_unit(lambda n, il, hh: jnp.maximum(jnp.max(s_w[n], axis=-1, keepdims=True),
                                                   jnp.max(s_c[n], axis=-1, keepdims=True)))
        p_w = per_unit(lambda n, il, hh: jnp.exp(s_w[n] - m[n]))
        p_c = per_unit(lambda n, il, hh: jnp.exp(s_c[n] - m[n]))
        l = per_unit(lambda n, il, hh: jnp.sum(p_w[n], axis=-1, keepdims=True)
                     + jnp.sum(p_c[n], axis=-1, keepdims=True))
        o = per_unit(lambda n, il, hh: (_dot(p_w[n].astype(bf16), vw[il]) + _dot(p_c[n].astype(bf16), vc)) / l[n])
        for il in grid_rows:
            o_ref[il * GRID_W:(il + 1) * GRID_W, :] = jnp.where(low, o[2 * il], o[2 * il + 1]).astype(o_ref.dtype)


def na_attention(uc, bias, n_ctx):
    n = uc.shape[0]
    width = uc.shape[1] // 3
    npair = width // LANES
    n_rows = (n - n_ctx) // GRID_W
    return pl.pallas_call(
        functools.partial(_na_body, n_ctx=n_ctx, n_rows=n_rows),
        grid=(npair, n // ROW_TILE),
        in_specs=[
            pl.BlockSpec((ROW_TILE, LANES), lambda p, j: (j, p)),
            pl.BlockSpec((n, LANES), lambda p, j: (0, npair + p)),
            pl.BlockSpec((n, LANES), lambda p, j: (0, 2 * npair + p)),
            pl.BlockSpec((2, WIN_ROWS, GRID_W, WIN_ROWS * GRID_W), lambda p, j: (p, 0, 0, 0)),
        ],
        out_specs=pl.BlockSpec((ROW_TILE, LANES), lambda p, j: (j, p)),
        out_shape=jax.ShapeDtypeStruct((n, width), bf16),
        compiler_params=_params(("arbitrary", "arbitrary")),
        name="na_attention",
    )(uc, uc, uc, bias)


def _rope_tables(n_ctx, seq):
    t = jnp.arange(seq)
    row = (t // GRID_W).astype(f32)
    col = (t % GRID_W).astype(f32)
    n_freq = QK_ROPE // 4
    inv = ROPE_BASE ** (-jnp.arange(n_freq, dtype=f32) / n_freq)
    ar, ac = row[:, None] * inv[None, :], col[:, None] * inv[None, :]
    cos = jnp.concatenate([jnp.cos(ar), jnp.cos(ar), jnp.cos(ac), jnp.cos(ac)], axis=1)
    sin = jnp.concatenate([-jnp.sin(ar), jnp.sin(ar), -jnp.sin(ac), jnp.sin(ac)], axis=1)
    lat = jnp.concatenate([cos, sin], axis=1)
    ctx = jnp.concatenate([jnp.ones((n_ctx, QK_ROPE), f32), jnp.zeros((n_ctx, QK_ROPE), f32)], axis=1)
    return jnp.concatenate([ctx, lat], axis=0)


def _pair_swap_perm():
    q = QK_ROPE // 4
    return np.concatenate([np.arange(q, 2 * q), np.arange(0, q), np.arange(3 * q, 4 * q), np.arange(2 * q, 3 * q)])


def _na_bias_tables(rpb):
    cols = np.arange(GRID_W)
    c0 = np.clip(cols - WIN_COLS // 2, 0, GRID_W - WIN_COLS)
    inside = (cols[None, :] >= c0[:, None]) & (cols[None, :] < c0[:, None] + WIN_COLS)
    rel = np.clip(cols[None, :] - cols[:, None] + WIN_COLS - 1, 0, 2 * WIN_COLS - 2)
    full = jnp.where(inside[None, None], rpb[:, :, rel], NEG_BIG)
    tabs = [jnp.concatenate([full[:, d0 + jj] for jj in range(WIN_ROWS)], axis=-1) for d0 in range(WIN_ROWS)]
    return jnp.stack(tabs, axis=1).astype(f32)


def _layer_weights(l, w_in, rwkv_conv, rwkv_w0, rwkv_w_up, rwkv_a0, rwkv_a_up, rwkv_g_up, rwkv_k_k, rwkv_k_a,
                   rwkv_r_k, mla_q_norm_g, mla_w_uq, mla_kv_norm_g, mla_w_ukv, w_branch, w_out, mlp_w1, mlp_w2):
    d = w_in.shape[1]
    aw = rwkv_k_k.shape[1]
    nh_b = mla_w_ukv.shape[2] // (QK_NOPE + V_DIM)
    n_lora = 2 * LORA_W + 2 * LORA_A + LORA_G
    c_lora = 3 * aw
    c_mla = c_lora + n_lora
    c_na = c_mla + Q_LORA + KV_LORA + QK_ROPE
    c_gate = c_na + 3 * aw
    wi = w_in[l]
    perm = _pair_swap_perm()
    kr = wi[:, c_mla + Q_LORA + KV_LORA:c_na]
    w_b = jnp.concatenate([
        wi[:, c_lora:c_mla],
        wi[:, c_mla + Q_LORA:c_mla + Q_LORA + KV_LORA],
        wi[:, c_mla:c_mla + Q_LORA], jnp.zeros((d, 512 - Q_LORA), f32),
        kr, kr[:, perm]], axis=1).astype(bf16)
    cols = dict(c_kvd=n_lora, c_q=n_lora + KV_LORA, c_kr=n_lora + KV_LORA + 512)

    def lora_pad(w, start):
        r = w.shape[1]
        out = jnp.zeros((2, n_lora, aw), f32)
        for z in range(2):
            out = out.at[z, start + z * r:start + (z + 1) * r].set(w[z])
        return out.astype(bf16)

    g_up = jnp.zeros((n_lora, aw), f32).at[2 * LORA_W + 2 * LORA_A:].set(rwkv_g_up[l]).astype(bf16)
    head_id = np.arange(aw) // A_HEAD_DIM
    seg = jnp.asarray(head_id[:, None] == head_id[None, :], bf16)
    rwkv = dict(conv_rkv=rwkv_conv[l][:, :c_lora], conv_l=rwkv_conv[l][:, c_lora:c_mla],
                w_up=lora_pad(rwkv_w_up[l], 0), a_up=lora_pad(rwkv_a_up[l], 2 * LORA_W), g_up=g_up,
                w0=rwkv_w0[l], a0=rwkv_a0[l], k_k=rwkv_k_k[l][None], k_a=rwkv_k_a[l][None],
                r_k=rwkv_r_k[l].reshape(1, aw), seg=seg)

    uq = mla_w_uq[l].reshape(Q_LORA, nh_b, QK_NOPE + QK_ROPE)
    uq = jnp.concatenate([uq, jnp.zeros((512 - Q_LORA, nh_b, QK_NOPE + QK_ROPE), f32)], axis=0)
    zpad = jnp.zeros((512, nh_b, LANES - QK_ROPE), f32)
    pe = uq[:, :, QK_NOPE:]
    ukv = mla_w_ukv[l].reshape(KV_LORA, nh_b, QK_NOPE + V_DIM)
    mla = dict(
        gq=jnp.concatenate([mla_q_norm_g[l], jnp.zeros((512 - Q_LORA,), f32)])[None],
        gkv=mla_kv_norm_g[l][None],
        wq_nope=uq[:, :, :QK_NOPE].reshape(512, nh_b * QK_NOPE).astype(bf16),
        wq_pe=jnp.concatenate([pe, zpad], axis=2).reshape(512, nh_b * LANES).astype(bf16),
        wq_pr=jnp.concatenate([pe[:, :, perm], zpad], axis=2).reshape(512, nh_b * LANES).astype(bf16),
        wkv=jnp.concatenate([ukv[:, :, :QK_NOPE].reshape(KV_LORA, -1), ukv[:, :, QK_NOPE:].reshape(KV_LORA, -1)],
                            axis=1).astype(bf16))
    return dict(w_rkv=wi[:, :c_lora].astype(bf16), w_b=w_b, w_na=wi[:, c_na:c_gate].astype(bf16),
                w_gate=wi[:, c_gate:].astype(bf16), cols=cols, rwkv=rwkv, mla=mla,
                w_branch=w_branch[l].astype(bf16), w_out=w_out[l].astype(bf16),
                mlp_w1=mlp_w1[l].astype(bf16), mlp_w2=mlp_w2[l].astype(bf16))


def kernel(x, c, ctx, c_ctx, ada_w, ada_b, norm_mix_g, norm_mlp_g, w_in, rwkv_conv, rwkv_w0, rwkv_w_up, rwkv_a0, rwkv_a_up, rwkv_g_up, rwkv_k_k, rwkv_k_a, rwkv_r_k, rwkv_ln_g, rwkv_ln_b, mla_q_norm_g, mla_w_uq, mla_kv_norm_g, mla_w_ukv, na_rpb, w_branch, w_out, mlp_w1, mlp_w2, final_norm_g):
    batch, seq, d = x.shape
    n_ctx = ctx.shape[1]
    depth = ada_w.shape[0]
    assert batch == 1 and n_ctx == ROW_TILE and seq % ROW_TILE == 0 and seq // GRID_W >= WIN_ROWS
    n_ctx_tiles = n_ctx // ROW_TILE

    xs = jnp.concatenate([ctx[0], x[0]], axis=0)
    cond = jnp.concatenate([c, c_ctx[None], jnp.zeros((6, d), f32)], axis=0)
    mods = ada_modulation(cond, ada_w, ada_b)
    cs = _rope_tables(n_ctx, seq)

    for l in range(depth):
        lw = _layer_weights(l, w_in, rwkv_conv, rwkv_w0, rwkv_w_up, rwkv_a0, rwkv_a_up, rwkv_g_up, rwkv_k_k,
                            rwkv_k_a, rwkv_r_k, mla_q_norm_g, mla_w_uq, mla_kv_norm_g, mla_w_ukv, w_branch,
                            w_out, mlp_w1, mlp_w2)
        mods8 = mods[l]
        mods2 = mods8[:2, None, :]
        h = norm_modulate(xs, norm_mix_g[l], mods2, 0)
        u3 = matmul(h, lw["w_rkv"], f32, 512)
        ub = matmul(h, lw["w_b"], f32, lw["w_b"].shape[1])
        uc = matmul(h, lw["w_na"], bf16, 512)
        ug = matmul(h, lw["w_gate"], f32, 512)

        r, v, kk, bonus, g, lwd, kd, bd = rwkv_prepare(u3, ub, lw["rwkv"], n_ctx)
        y = rwkv_chunk_scan(r, v, kk, lwd, kd, bd, n_ctx)
        ya = rwkv_readout(y, bonus, g, rwkv_ln_g[l], rwkv_ln_b[l], lw["rwkv"]["seg"][:LANES, :LANES])

        q_b, k_b, v_b = mla_prepare(ub, cs, lw["mla"], lw["cols"])
        yb = mla_attention(q_b, k_b, v_b, n_ctx)

        yc = na_attention(uc, _na_bias_tables(na_rpb[l]), n_ctx)

        merged = merge_branches(ya, yb, yc, ug, lw["w_branch"])
        xs = matmul_gated_residual(merged, lw["w_out"], xs, mods8, 2, n_ctx)
        h2 = norm_modulate(xs, norm_mlp_g[l], mods2, 3)
        xs = mlp_gated_residual(h2, lw["mlp_w1"], lw["mlp_w2"], xs, mods8, 5, n_ctx)

    return final_norm(xs, final_norm_g, n_ctx_tiles)[None]
```

```python
import functools
import math

import jax
import jax.numpy as jnp
import numpy as np
from jax import lax
from jax.experimental import pallas as pl
from jax.experimental.pallas import tpu as pltpu

f32 = jnp.float32
bf16 = jnp.bfloat16

GRID_W = 64
RMS_EPS = 1e-6
N_BRANCH = 3
A_HEAD_DIM = 64
LORA_W = 96
LORA_A = 96
LORA_G = 256
GN_EPS = 64e-5
QK_NOPE = 128
QK_ROPE = 64
V_DIM = 128
Q_LORA = 448
KV_LORA = 128
MLA_SCALE = (QK_NOPE + QK_ROPE) ** -0.5
MLA_Q_SCALE = MLA_SCALE * math.log2(math.e)
ROPE_BASE = 10000.0
C_HEAD_DIM = 64
WIN_ROWS = 8
WIN_COLS = 16
NA_SCALE = C_HEAD_DIM ** -0.5
DECAY_SCALE = math.exp(-0.5)

LANES = 128
ROW_TILE = 256
CHUNK = 64
CHUNKS_PER_STEP = 2
PREP_TILE = 128
MLA_HEADS_PER_STEP = 2
VMEM_LIMIT = 48 * 1024 * 1024
NEG_BIG = -1e30


def _params(sem):
    return pltpu.CompilerParams(dimension_semantics=sem, vmem_limit_bytes=VMEM_LIMIT)


def _dot(a, b):
    return jnp.dot(a, b, preferred_element_type=f32)


def _dot_nt(a, b):
    return lax.dot_general(a, b, (((1,), (1,)), ((), ())), preferred_element_type=f32)


def _ada_body(s_ref, w_ref, b_ref, o_ref):
    s = s_ref[...]
    s = s * jax.nn.sigmoid(s)
    o_ref[...] = _dot(s, w_ref[...]) + b_ref[...]


def ada_modulation(cond, ada_w, ada_b):
    nl, d, n6 = ada_w.shape
    tn = 1024
    return pl.pallas_call(
        _ada_body,
        grid=(nl, n6 // tn),
        in_specs=[
            pl.BlockSpec((8, d), lambda l, j: (0, 0)),
            pl.BlockSpec((None, d, tn), lambda l, j: (l, 0, j)),
            pl.BlockSpec((None, 1, tn), lambda l, j: (l, 0, j)),
        ],
        out_specs=pl.BlockSpec((None, 8, tn), lambda l, j: (l, 0, j)),
        out_shape=jax.ShapeDtypeStruct((nl, 8, n6), f32),
        compiler_params=_params(("arbitrary", "arbitrary")),
        name="ada_modulation",
    )(cond, ada_w, ada_b.reshape(nl, 1, n6))


def _norm_mod_body(x_ref, g_ref, m_ref, o_ref, *, off, d):
    x = x_ref[...]
    y = x * lax.rsqrt(jnp.mean(x * x, axis=-1, keepdims=True) + RMS_EPS) * g_ref[...]
    shift = m_ref[:, off * d:(off + 1) * d]
    scale = m_ref[:, (off + 1) * d:(off + 2) * d]
    o_ref[...] = (y * (1.0 + scale) + shift).astype(o_ref.dtype)


def norm_modulate(x, g, mods2, off):
    n, d = x.shape
    return pl.pallas_call(
        functools.partial(_norm_mod_body, off=off, d=d),
        grid=(n // ROW_TILE,),
        in_specs=[
            pl.BlockSpec((ROW_TILE, d), lambda i: (i, 0)),
            pl.BlockSpec((1, d), lambda i: (0, 0)),
            pl.BlockSpec((None, 1, mods2.shape[-1]), lambda i: (jnp.where(i == 0, 1, 0), 0, 0)),
        ],
        out_specs=pl.BlockSpec((ROW_TILE, d), lambda i: (i, 0)),
        out_shape=jax.ShapeDtypeStruct((n, d), bf16),
        compiler_params=_params(("arbitrary",)),
        name="norm_modulate",
    )(x, g.reshape(1, d), mods2)


def _final_norm_body(x_ref, g_ref, o_ref):
    x = x_ref[...]
    o_ref[...] = x * lax.rsqrt(jnp.mean(x * x, axis=-1, keepdims=True) + RMS_EPS) * g_ref[...]


def final_norm(x, g, n_ctx_tiles):
    n, d = x.shape
    t = n - n_ctx_tiles * ROW_TILE
    return pl.pallas_call(
        _final_norm_body,
        grid=(t // ROW_TILE,),
        in_specs=[
            pl.BlockSpec((ROW_TILE, d), lambda i: (i + n_ctx_tiles, 0)),
            pl.BlockSpec((1, d), lambda i: (0, 0)),
        ],
        out_specs=pl.BlockSpec((ROW_TILE, d), lambda i: (i, 0)),
        out_shape=jax.ShapeDtypeStruct((t, d), f32),
        compiler_params=_params(("arbitrary",)),
        name="final_norm",
    )(x, g.reshape(1, d))


def _mm_body(a_ref, w_ref, o_ref):
    o_ref[...] = _dot(a_ref[...], w_ref[...]).astype(o_ref.dtype)


def _row_tile(n):
    for tm in (1056, 768, 512, 384, 256):
        if n % tm == 0:
            return tm
    raise ValueError(f"no row tile for {n} rows")


def matmul(a, w, out_dtype, tn):
    m, k = a.shape
    n = w.shape[1]
    tm = _row_tile(m)
    return pl.pallas_call(
        _mm_body,
        grid=(m // tm, n // tn),
        in_specs=[pl.BlockSpec((tm, k), lambda i, j: (i, 0)), pl.BlockSpec((k, tn), lambda i, j: (0, j))],
        out_specs=pl.BlockSpec((tm, tn), lambda i, j: (i, j)),
        out_shape=jax.ShapeDtypeStruct((m, n), out_dtype),
        compiler_params=_params(("arbitrary", "arbitrary")),
        name="matmul",
    )(a, w)


def _gate_rows(m_ref, tm, n_ctx):
    rows = pl.program_id(0) * tm + lax.broadcasted_iota(jnp.int32, (tm, 1), 0)
    return jnp.where(rows < n_ctx, m_ref[1:2, :], m_ref[0:1, :])


def _mm_res_body(a_ref, w_ref, x_ref, m_ref, o_ref, *, tm, n_ctx):
    acc = _dot(a_ref[...], w_ref[...])
    o_ref[...] = x_ref[...] + _gate_rows(m_ref, tm, n_ctx) * acc


def matmul_gated_residual(a, w, x, mods8, gate_off, n_ctx, tn=512):
    m, k = a.shape
    n = w.shape[1]
    tm = _row_tile(m)
    nj = n // tn
    return pl.pallas_call(
        functools.partial(_mm_res_body, tm=tm, n_ctx=n_ctx),
        grid=(m // tm, nj),
        in_specs=[
            pl.BlockSpec((tm, k), lambda i, j: (i, 0)),
            pl.BlockSpec((k, tn), lambda i, j: (0, j)),
            pl.BlockSpec((tm, tn), lambda i, j: (i, j)),
            pl.BlockSpec((8, tn), lambda i, j: (0, gate_off * nj + j)),
        ],
        out_specs=pl.BlockSpec((tm, tn), lambda i, j: (i, j)),
        out_shape=jax.ShapeDtypeStruct((m, n), f32),
        compiler_params=_params(("arbitrary", "arbitrary")),
        name="matmul_gated_residual",
    )(a, w, x, mods8)


def _merge_body(ya_ref, yb_ref, yc_ref, g0_ref, g1_ref, g2_ref, w_ref, o_ref):
    acc = jax.nn.sigmoid(g0_ref[...]) * _dot(ya_ref[...], w_ref[0])
    acc = acc + jax.nn.sigmoid(g1_ref[...]) * _dot(yb_ref[...], w_ref[1])
    acc = acc + jax.nn.sigmoid(g2_ref[...]) * _dot(yc_ref[...], w_ref[2])
    o_ref[...] = acc.astype(o_ref.dtype)


def merge_branches(ya, yb, yc, ug, w_branch, tn=512):
    m, k = ya.shape
    d = w_branch.shape[-1]
    tm = 528 if m % 528 == 0 else ROW_TILE
    nj = d // tn
    y_spec = pl.BlockSpec((tm, k), lambda i, j: (i, 0))
    return pl.pallas_call(
        _merge_body,
        grid=(m // tm, nj),
        in_specs=[
            y_spec, y_spec, y_spec,
            pl.BlockSpec((tm, tn), lambda i, j: (i, j)),
            pl.BlockSpec((tm, tn), lambda i, j: (i, nj + j)),
            pl.BlockSpec((tm, tn), lambda i, j: (i, 2 * nj + j)),
            pl.BlockSpec((N_BRANCH, k, tn), lambda i, j: (0, 0, j)),
        ],
        out_specs=pl.BlockSpec((tm, tn), lambda i, j: (i, j)),
        out_shape=jax.ShapeDtypeStruct((m, d), bf16),
        compiler_params=_params(("arbitrary", "arbitrary")),
        name="merge_branches",
    )(ya, yb, yc, ug, ug, ug, w_branch)


def _mlp_body(h_ref, w1_ref, w2_ref, x_ref, m_ref, o_ref, acc_ref, *, tm, n_ctx):
    f = pl.program_id(1)

    @pl.when(f == 0)
    def _():
        acc_ref[...] = jnp.zeros_like(acc_ref)

    h1 = jnp.maximum(_dot(h_ref[...], w1_ref[...]), 0.0)
    acc_ref[...] += _dot((h1 * h1).astype(bf16), w2_ref[...])

    @pl.when(f == pl.num_programs(1) - 1)
    def _():
        o_ref[...] = x_ref[...] + _gate_rows(m_ref, tm, n_ctx) * acc_ref[...]


def mlp_gated_residual(h, w1, w2, x, mods8, gate_off, n_ctx, tf=512):
    m, d = h.shape
    dff = w1.shape[1]
    tm = 528 if m % 528 == 0 else ROW_TILE
    return pl.pallas_call(
        functools.partial(_mlp_body, tm=tm, n_ctx=n_ctx),
        grid=(m // tm, dff // tf),
        in_specs=[
            pl.BlockSpec((tm, d), lambda i, f: (i, 0)),
            pl.BlockSpec((d, tf), lambda i, f: (0, f)),
            pl.BlockSpec((tf, d), lambda i, f: (f, 0)),
            pl.BlockSpec((tm, d), lambda i, f: (i, 0)),
            pl.BlockSpec((8, d), lambda i, f: (0, gate_off)),
        ],
        out_specs=pl.BlockSpec((tm, d), lambda i, f: (i, 0)),
        out_shape=jax.ShapeDtypeStruct((m, d), f32),
        scratch_shapes=[pltpu.VMEM((tm, d), f32)],
        compiler_params=_params(("arbitrary", "arbitrary")),
        name="mlp_gated_residual",
    )(h, w1, w2, x, mods8)


def _seg_sum(x, e_ref):
    hi = x.astype(bf16)
    lo = (x - hi.astype(f32)).astype(bf16)
    e = e_ref[...]
    w = e.shape[0]
    parts = [_dot(hi[:, g:g + w], e) + _dot(lo[:, g:g + w], e) for g in range(0, x.shape[1], w)]
    return parts[0] if len(parts) == 1 else jnp.concatenate(parts, axis=1)


def _to_pairs(o_ref, val, lead=None):
    for p in range(val.shape[-1] // LANES):
        piece = val[:, p * LANES:(p + 1) * LANES]
        if lead is None:
            o_ref[p] = piece
        else:
            o_ref[lead, p] = piece


def _rwkv_prep_body(u_ref, up_ref, un_ref, l_ref, lp_ref, ln_ref, cw_ref, cwl_ref, wu_ref, au_ref, gu_ref,
                    w0_ref, a0_ref, kk_ref, ka_ref, rk_ref, e_ref,
                    r_o, v_o, kkn_o, bonus_o, g_o, lw_o, kd_o, bd_o, *, n_ctx_tiles, aw):
    i = pl.program_id(0)
    last = pl.num_programs(0) - 1
    left_zero = (i == 0) | (i == n_ctx_tiles)
    right_zero = (i == n_ctx_tiles - 1) | (i == last)
    rows = lax.broadcasted_iota(jnp.int32, (PREP_TILE, 1), 0)

    def conv(x_ref, xp_ref, xn_ref, w_ref):
        x = x_ref[...]
        prev_row = jnp.where(left_zero, 0.0, xp_ref[7:8, :])
        next_row = jnp.where(right_zero, 0.0, xn_ref[0:1, :])
        x_prev = jnp.where(rows == 0, prev_row, pltpu.roll(x, 1, 0))
        x_next = jnp.where(rows == PREP_TILE - 1, next_row, pltpu.roll(x, PREP_TILE - 1, 0))
        return w_ref[0:1, :] * x_prev + w_ref[1:2, :] * x + w_ref[2:3, :] * x_next

    y = conv(u_ref, up_ref, un_ref, cw_ref)
    r, k, v = y[:, :aw], y[:, aw:2 * aw], y[:, 2 * aw:]
    yl = conv(l_ref, lp_ref, ln_ref, cwl_ref)
    th = jnp.tanh(yl).astype(bf16)
    sg = jax.nn.sigmoid(yl).astype(bf16)
    ylb = yl.astype(bf16)

    _to_pairs(g_o, _dot(sg, gu_ref[...]))
    _to_pairs(r_o, r)
    _to_pairs(v_o, v)

    kkr = k * kk_ref[...]
    norm = jnp.sqrt(_seg_sum(kkr * kkr, e_ref))
    kkn = kkr / jnp.maximum(norm, 1e-12)
    _to_pairs(kkn_o, kkn)

    ksum = None
    for dr in range(2):
        z = w0_ref[dr:dr + 1, :] + _dot(th, wu_ref[dr])
        _to_pairs(lw_o, -DECAY_SCALE * jax.nn.sigmoid(z), lead=dr)
        a = jax.nn.sigmoid(a0_ref[dr:dr + 1, :] + _dot(ylb, au_ref[dr]))
        kd = k * (1.0 + (a - 1.0) * ka_ref[...])
        _to_pairs(kd_o, kd, lead=dr)
        _to_pairs(bd_o, kkn * a, lead=dr)
        ksum = kd if ksum is None else ksum + kd
    rk = _seg_sum(r * (0.5 * ksum) * rk_ref[...], e_ref)
    _to_pairs(bonus_o, rk * v)


def rwkv_prepare(u3, ub, pw, n_ctx):
    n = u3.shape[0]
    aw = u3.shape[1] // 3
    npair = aw // LANES
    lw = pw["conv_l"].shape[1]
    tpb = PREP_TILE // 8
    nb8 = n // 8

    def prev_map(i):
        return (jnp.maximum(i * tpb - 1, 0), 0)

    def next_map(i):
        return (jnp.minimum((i + 1) * tpb, nb8 - 1), 0)

    full = lambda a: pl.BlockSpec(a.shape, lambda i: (0,) * a.ndim)
    pm = jax.ShapeDtypeStruct((npair, n, LANES), f32)
    pm2 = jax.ShapeDtypeStruct((2, npair, n, LANES), f32)
    pm_spec = pl.BlockSpec((npair, PREP_TILE, LANES), lambda i: (0, i, 0))
    pm2_spec = pl.BlockSpec((2, npair, PREP_TILE, LANES), lambda i: (0, 0, i, 0))
    consts = [pw["conv_rkv"], pw["conv_l"], pw["w_up"], pw["a_up"], pw["g_up"], pw["w0"], pw["a0"],
              pw["k_k"], pw["k_a"], pw["r_k"], pw["seg"]]
    return pl.pallas_call(
        functools.partial(_rwkv_prep_body, n_ctx_tiles=n_ctx // PREP_TILE, aw=aw),
        grid=(n // PREP_TILE,),
        in_specs=[
            pl.BlockSpec((PREP_TILE, 3 * aw), lambda i: (i, 0)),
            pl.BlockSpec((8, 3 * aw), prev_map),
            pl.BlockSpec((8, 3 * aw), next_map),
            pl.BlockSpec((PREP_TILE, lw), lambda i: (i, 0)),
            pl.BlockSpec((8, lw), prev_map),
            pl.BlockSpec((8, lw), next_map),
        ] + [full(a) for a in consts],
        out_specs=[pm_spec] * 5 + [pm2_spec] * 3,
        out_shape=[pm] * 5 + [pm2] * 3,
        compiler_params=_params(("arbitrary",)),
        name="rwkv_prepare",
    )(u3, u3, u3, ub, ub, ub, *consts)


def _dot_tn(a, b):
    return lax.dot_general(a, b, (((0,), (0,)), ((), ())), preferred_element_type=f32)


def _rwkv_chunk_body(r_ref, v_ref, kk_ref, lw_ref, k_ref, b_ref, y_ref, h_ref, *, n_pairs):
    d = pl.program_id(0)
    j = pl.program_id(1)
    c = CHUNK
    c2 = 2 * c

    @pl.when(j == 0)
    def _():
        h_ref[...] = jnp.zeros_like(h_ref)

    sign = jnp.where(d == 0, 1, -1)
    ti = lax.broadcasted_iota(jnp.int32, (c, c), 0)
    si = lax.broadcasted_iota(jnp.int32, (c, c), 1)
    incl_c = jnp.where((ti - si) * sign >= 0, 1.0, 0.0).astype(bf16)
    t2 = lax.broadcasted_iota(jnp.int32, (c2, c2), 0)
    s2 = lax.broadcasted_iota(jnp.int32, (c2, c2), 1)
    ahead = (t2 - s2) * sign
    blk = lambda n: (t2 // n) == (s2 // n)
    incl = blk(c) & (ahead >= 0)
    strict = blk(c) & (ahead > 0)
    eye = jnp.where(t2 == s2, 1.0, 0.0)
    inv_levels = []
    n = 2
    while n < c:
        inv_levels.append(strict & blk(2 * n) & ~blk(n))
        n *= 2
    low = lax.broadcasted_iota(jnp.int32, (1, LANES), 1) < A_HEAD_DIM

    def stack(x):
        return jnp.concatenate([jnp.where(low, x, 0.0), jnp.where(low, 0.0, x)], axis=0).astype(bf16)

    pairs = range(n_pairs)
    first = jnp.where(d == 0, 0, CHUNKS_PER_STEP - 1)
    offs = [pl.multiple_of((first + sign * sc) * c, c) for sc in range(CHUNKS_PER_STEP)]
    units = [(sc, p) for sc in range(CHUNKS_PER_STEP) for p in pairs]
    each = lambda f: [f(sc, p) for sc, p in units]
    at = lambda sc, p: sc * n_pairs + p
    ld = lambda ref, sc, p: ref[p, pl.ds(offs[sc], c), :]
    lw = each(lambda sc, p: ld(lw_ref, sc, p))
    lw_hi = [x.astype(bf16) for x in lw]
    lw_lo = [(x - hi.astype(f32)).astype(bf16) for x, hi in zip(lw, lw_hi)]
    cum = [_dot(incl_c, hi) + _dot(incl_c, lo) for hi, lo in zip(lw_hi, lw_lo)]
    tot = [jnp.sum(x, axis=0, keepdims=True) for x in lw]
    e_neg = [jnp.exp(-x) for x in cum]
    e_end = [jnp.exp(t - x) for t, x in zip(tot, cum)]
    r_s = each(lambda sc, p: stack(ld(r_ref, sc, p) * jnp.exp(cum[at(sc, p)])))
    kk_s = each(lambda sc, p: stack(ld(kk_ref, sc, p) * jnp.exp(cum[at(sc, p)] - lw[at(sc, p)])))
    kb_s = each(lambda sc, p: jnp.concatenate([stack(ld(b_ref, sc, p) * e_neg[at(sc, p)]),
                                               stack(ld(k_ref, sc, p) * e_neg[at(sc, p)])], axis=0))
    kbd_s = each(lambda sc, p: jnp.concatenate([stack(ld(k_ref, sc, p) * e_end[at(sc, p)]),
                                                stack(ld(b_ref, sc, p) * e_end[at(sc, p)])], axis=0))
    v_s = each(lambda sc, p: stack(ld(v_ref, sc, p)))
    a1 = [_dot_nt(a, b) for a, b in zip(kk_s, kb_s)]
    a2 = [_dot_nt(a, b) for a, b in zip(r_s, kb_s)]
    a_kb = [jnp.where(strict, a[:, :c2], 0.0) for a in a1]
    a_kk = [jnp.where(strict, a[:, c2:], 0.0).astype(bf16) for a in a1]
    a_rb = [jnp.where(incl, a[:, :c2], 0.0).astype(bf16) for a in a2]
    a_rk = [jnp.where(incl, a[:, c2:], 0.0).astype(bf16) for a in a2]
    t_inv = [eye - jnp.where(blk(2), a, 0.0) for a in a_kb]
    for m in inv_levels:
        t_b = [t.astype(bf16) for t in t_inv]
        x = [_dot(t, jnp.where(m, a, 0.0).astype(bf16)).astype(bf16) for t, a in zip(t_b, a_kb)]
        t_inv = [t - _dot(xx, tb) for t, xx, tb in zip(t_inv, x, t_b)]
    g = [_dot(t.astype(bf16), jnp.concatenate([kk, akk], axis=1)).astype(bf16)
         for t, kk, akk in zip(t_inv, kk_s, a_kk)]
    w_col = [jnp.sum(eye * jnp.exp(t), axis=1, keepdims=True) for t in tot]
    h = [h_ref[p] for p in pairs]
    for sc in range(CHUNKS_PER_STEP):
        ix = [at(sc, p) for p in pairs]
        h_b = [x.astype(bf16) for x in h]
        u = [_dot(g[i], jnp.concatenate([h_b[p], v_s[i]], axis=0)).astype(bf16) for p, i in enumerate(ix)]
        y = [_dot(jnp.concatenate([r_s[i], a_rk[i], -a_rb[i]], axis=1), jnp.concatenate([h_b[p], v_s[i], u[p]], axis=0))
             for p, i in enumerate(ix)]
        h1 = [_dot_tn(kbd_s[i], jnp.concatenate([v_s[i], -u[p]], axis=0)) for p, i in enumerate(ix)]
        for p in pairs:
            y_ref[p, pl.ds(offs[sc], c), :] = y[p][:c] + y[p][c:]
        h = [h[p] * w_col[i] + h1[p] for p, i in enumerate(ix)]
    for p in pairs:
        h_ref[p] = h[p]


def rwkv_chunk_scan(r, v, kk, lw, kd, bd, n_ctx):
    npair, n, _ = r.shape
    blk_rows = CHUNKS_PER_STEP * CHUNK
    nc = n // blk_rows
    ncc = n_ctx // blk_rows

    def cidx(d, j):
        bwd = jnp.where(j < ncc, ncc - 1 - j, nc - 1 - (j - ncc))
        return jnp.where(d == 0, j, bwd)

    shared = pl.BlockSpec((npair, blk_rows, LANES), lambda d, j: (0, cidx(d, j), 0))
    per_dir = pl.BlockSpec((None, npair, blk_rows, LANES), lambda d, j: (d, 0, cidx(d, j), 0))
    return pl.pallas_call(
        functools.partial(_rwkv_chunk_body, n_pairs=npair),
        grid=(2, nc),
        in_specs=[shared, shared, shared, per_dir, per_dir, per_dir],
        out_specs=per_dir,
        out_shape=jax.ShapeDtypeStruct((2, npair, n, LANES), f32),
        scratch_shapes=[pltpu.VMEM((npair, LANES, LANES), f32)],
        compiler_params=_params(("arbitrary", "arbitrary")),
        name="rwkv_chunk_scan",
    )(r, v, kk, lw, kd, bd)


def _rwkv_readout_body(y_ref, bonus_ref, g_ref, lng_ref, lnb_ref, e_ref, o_ref, *, n_pairs):
    inv = 1.0 / A_HEAD_DIM
    for p in range(n_pairs):
        y = y_ref[0, p] + y_ref[1, p]
        yc = y - _seg_sum(y, e_ref) * inv
        var = _seg_sum(yc * yc, e_ref) * inv
        yn = yc * lax.rsqrt(var + GN_EPS) * lng_ref[p] + lnb_ref[p]
        o_ref[:, p * LANES:(p + 1) * LANES] = ((yn + bonus_ref[p]) * g_ref[p]).astype(o_ref.dtype)


def rwkv_readout(y, bonus, g, ln_g, ln_b, seg_pair):
    _, npair, n, _ = y.shape
    pm_spec = pl.BlockSpec((npair, ROW_TILE, LANES), lambda i: (0, i, 0))
    vec_spec = pl.BlockSpec((npair, 1, LANES), lambda i: (0, 0, 0))
    return pl.pallas_call(
        functools.partial(_rwkv_readout_body, n_pairs=npair),
        grid=(n // ROW_TILE,),
        in_specs=[pl.BlockSpec((2, npair, ROW_TILE, LANES), lambda i: (0, 0, i, 0)), pm_spec, pm_spec,
                  vec_spec, vec_spec, pl.BlockSpec((LANES, LANES), lambda i: (0, 0))],
        out_specs=pl.BlockSpec((ROW_TILE, npair * LANES), lambda i: (i, 0)),
        out_shape=jax.ShapeDtypeStruct((n, npair * LANES), bf16),
        compiler_params=_params(("arbitrary",)),
        name="rwkv_readout",
    )(y, bonus, g, ln_g.reshape(npair, 1, LANES), ln_b.reshape(npair, 1, LANES), seg_pair)


def _mla_prep_body(u_ref, cs_ref, gq_ref, gkv_ref, wqn_ref, wqp_ref, wqr_ref, wkv_ref, q_o, k_o, v_o,
                   *, n_heads, c_kvd, c_q, c_kr):
    ql = u_ref[:, c_q:c_q + 512]
    qn = ql * lax.rsqrt(jnp.sum(ql * ql, axis=-1, keepdims=True) * (1.0 / Q_LORA) + RMS_EPS) * gq_ref[...]
    qn = qn.astype(bf16)
    kvd = u_ref[:, c_kvd:c_kvd + KV_LORA]
    kvn = kvd * lax.rsqrt(jnp.mean(kvd * kvd, axis=-1, keepdims=True) + RMS_EPS) * gkv_ref[...]
    kv = _dot(kvn.astype(bf16), wkv_ref[...])
    cs = cs_ref[...]
    lane = lax.broadcasted_iota(jnp.int32, (1, LANES), 1)
    low = lane < QK_ROPE
    cos_t = jnp.where(low, cs, 0.0)
    sin_t = jnp.where(low, pltpu.roll(cs, QK_ROPE, 1), 0.0)
    kr = u_ref[:, c_kr:c_kr + LANES]
    prod = kr * cs
    k_pe = jnp.where(low, prod + pltpu.roll(prod, QK_ROPE, 1), 0.0).astype(bf16)
    q_nope = _dot(qn, wqn_ref[...])
    q_pe = _dot(qn, wqp_ref[...])
    q_pr = _dot(qn, wqr_ref[...])
    for h in range(n_heads):
        sl = slice(h * LANES, (h + 1) * LANES)
        q_o[h, :, 0:LANES] = (q_nope[:, sl] * MLA_Q_SCALE).astype(bf16)
        q_o[h, :, LANES:2 * LANES] = ((q_pe[:, sl] * cos_t + q_pr[:, sl] * sin_t) * MLA_Q_SCALE).astype(bf16)
        k_o[h, :, 0:LANES] = kv[:, sl].astype(bf16)
        k_o[h, :, LANES:2 * LANES] = k_pe
        v_o[h] = kv[:, n_heads * LANES + h * LANES:n_heads * LANES + (h + 1) * LANES].astype(bf16)


def mla_prepare(ub, cs, pw, cols):
    n = ub.shape[0]
    nh = pw["wq_nope"].shape[1] // LANES
    full = lambda a: pl.BlockSpec(a.shape, lambda i: (0,) * a.ndim)
    consts = [pw["gq"], pw["gkv"], pw["wq_nope"], pw["wq_pe"], pw["wq_pr"], pw["wkv"]]
    return pl.pallas_call(
        functools.partial(_mla_prep_body, n_heads=nh, **cols),
        grid=(n // ROW_TILE,),
        in_specs=[pl.BlockSpec((ROW_TILE, ub.shape[1]), lambda i: (i, 0)),
                  pl.BlockSpec((ROW_TILE, LANES), lambda i: (i, 0))] + [full(a) for a in consts],
        out_specs=[pl.BlockSpec((nh, ROW_TILE, 2 * LANES), lambda i: (0, i, 0)),
                   pl.BlockSpec((nh, ROW_TILE, 2 * LANES), lambda i: (0, i, 0)),
                   pl.BlockSpec((nh, ROW_TILE, LANES), lambda i: (0, i, 0))],
        out_shape=[jax.ShapeDtypeStruct((nh, n, 2 * LANES), bf16),
                   jax.ShapeDtypeStruct((nh, n, 2 * LANES), bf16),
                   jax.ShapeDtypeStruct((nh, n, LANES), bf16)],
        compiler_params=_params(("arbitrary",)),
        name="mla_prepare",
    )(ub, cs, *consts)


def _mla_attn_body(q_ref, k_ref, v_ref, o_ref, *, n_ctx, n_tok, tk, tq):
    i = pl.program_id(1)
    hs = range(MLA_HEADS_PER_STEP)
    subs = range(tq // ROW_TILE)

    def rows(sb):
        return slice(sb * ROW_TILE, (sb + 1) * ROW_TILE)

    def write(chains, o):
        for (h, sb), val in zip(chains, o):
            o_ref[rows(sb), h * V_DIM:(h + 1) * V_DIM] = val.astype(o_ref.dtype)

    def context_queries():
        chains = [(h, 0) for h in hs]
        each = lambda f: [f(n, h) for n, (h, _) in enumerate(chains)]
        s = each(lambda n, h: _dot_nt(q_ref[h, rows(0), :], k_ref[h, 0:n_ctx, :]))
        p = each(lambda n, h: jnp.exp2(s[n] - jnp.max(s[n], axis=-1, keepdims=True)))
        pv = each(lambda n, h: _dot(p[n].astype(bf16), v_ref[h, 0:n_ctx, :]))
        write(chains, each(lambda n, h: pv[n] / jnp.sum(p[n], axis=-1, keepdims=True)))

    def latent_queries(chains):
        each = lambda f: [f(n, h, sb) for n, (h, sb) in enumerate(chains)]
        q = each(lambda n, h, sb: q_ref[h, rows(sb), :])

        def step(c, carry):
            off = pl.multiple_of(c * tk, tk)
            s = each(lambda n, h, sb: _dot_nt(q[n], k_ref[h, pl.ds(off, tk), :]))
            m_new = each(lambda n, h, sb: jnp.maximum(carry[n][0], jnp.max(s[n], axis=-1, keepdims=True)))
            alpha = each(lambda n, h, sb: jnp.exp2(carry[n][0] - m_new[n]))
            p = each(lambda n, h, sb: jnp.exp2(s[n] - m_new[n]))
            l = each(lambda n, h, sb: alpha[n] * carry[n][1] + jnp.sum(p[n], axis=-1, keepdims=True))
            pv = each(lambda n, h, sb: _dot(p[n].astype(bf16), v_ref[h, pl.ds(off, tk), :]))
            return tuple((m_new[n], l[n], alpha[n] * carry[n][2] + pv[n]) for n in range(len(chains)))

        init = tuple((jnp.full((ROW_TILE, 1), NEG_BIG, f32), jnp.zeros((ROW_TILE, 1), f32),
                      jnp.zeros((ROW_TILE, V_DIM), f32)) for _ in chains)
        fin = lax.fori_loop(0, n_tok // tk, step, init)
        write(chains, [f[2] / f[1] for f in fin])

    @pl.when(i == 0)
    def _():
        context_queries()
        if len(subs) > 1:
            latent_queries([(h, sb) for h in hs for sb in subs[1:]])

    @pl.when(i > 0)
    def _():
        latent_queries([(h, sb) for h in hs for sb in subs])


def mla_attention(q, k, v, n_ctx):
    nh, n, dq = q.shape
    hps = MLA_HEADS_PER_STEP
    assert n_ctx == ROW_TILE
    tk = next(t for t in (768, 512, 256) if n % t == 0)
    tq = tk
    return pl.pallas_call(
        functools.partial(_mla_attn_body, n_ctx=n_ctx, n_tok=n, tk=tk, tq=tq),
        grid=(nh // hps, n // tq),
        in_specs=[
            pl.BlockSpec((hps, tq, dq), lambda h, i: (h, i, 0)),
            pl.BlockSpec((hps, n, dq), lambda h, i: (h, 0, 0)),
            pl.BlockSpec((hps, n, V_DIM), lambda h, i: (h, 0, 0)),
        ],
        out_specs=pl.BlockSpec((tq, hps * V_DIM), lambda h, i: (i, h)),
        out_shape=jax.ShapeDtypeStruct((n, nh * V_DIM), bf16),
        compiler_params=_params(("arbitrary", "arbitrary")),
        name="mla_attention",
    )(q, k, v)


def _na_body(q_ref, k_ref, v_ref, bias_ref, o_ref, *, n_ctx, n_rows):
    j = pl.program_id(1)
    lane = lax.broadcasted_iota(jnp.int32, (1, LANES), 1)
    low = lane < C_HEAD_DIM
    kc = k_ref[0:n_ctx, :]
    vc = v_ref[0:n_ctx, :]
    win = WIN_ROWS * GRID_W

    @pl.when(j == 0)
    def _():
        q2 = q_ref[...]
        outs = []
        for hh in range(2):
            qm = jnp.where(low if hh == 0 else ~low, q2, jnp.zeros_like(q2))
            s = _dot_nt(qm, kc) * NA_SCALE
            p = jnp.exp(s - jnp.max(s, axis=-1, keepdims=True))
            outs.append(_dot(p.astype(bf16), vc) / jnp.sum(p, axis=-1, keepdims=True))
        o_ref[...] = jnp.where(low, outs[0], outs[1]).astype(o_ref.dtype)

    @pl.when(j > 0)
    def _():
        grid_rows = range(ROW_TILE // GRID_W)
        units = [(il, hh) for il in grid_rows for hh in range(2)]
        per_row = lambda f: [f(il) for il in grid_rows]
        per_unit = lambda f: [f(n, il, hh) for n, (il, hh) in enumerate(units)]
        i = per_row(lambda il: (j - 1) * (ROW_TILE // GRID_W) + il)
        r0 = per_row(lambda il: jnp.clip(i[il] - WIN_ROWS // 2, 0, n_rows - WIN_ROWS))
        d0 = per_row(lambda il: r0[il] - i[il] + WIN_ROWS - 1)
        off = per_row(lambda il: pl.multiple_of(n_ctx + r0[il] * GRID_W, GRID_W))
        q2 = per_row(lambda il: q_ref[il * GRID_W:(il + 1) * GRID_W, :] * NA_SCALE)
        kw = per_row(lambda il: k_ref[pl.ds(off[il], win), :])
        vw = per_row(lambda il: v_ref[pl.ds(off[il], win), :])
        qm = per_unit(lambda n, il, hh: jnp.where(low if hh == 0 else ~low, q2[il], jnp.zeros_like(q2[il])))
        s_w = per_unit(lambda n, il, hh: _dot_nt(qm[n], kw[il]) + bias_ref[hh, d0[il]])
        s_c = per_unit(lambda n, il, hh: _dot_nt(qm[n], kc))
        m = per_unit(lambda n, il, hh: jnp.maximum(jnp.max(s_w[n], axis=-1, keepdims=True),
                                                   jnp.max(s_c[n], axis=-1, keepdims=True)))
        p_w = per_unit(lambda n, il, hh: jnp.exp(s_w[n] - m[n]))
        p_c = per_unit(lambda n, il, hh: jnp.exp(s_c[n] - m[n]))
        l = per_unit(lambda n, il, hh: jnp.sum(p_w[n], axis=-1, keepdims=True)
                     + jnp.sum(p_c[n], axis=-1, keepdims=True))
        o = per_unit(lambda n, il, hh: (_dot(p_w[n].astype(bf16), vw[il]) + _dot(p_c[n].astype(bf16), vc)) / l[n])
        for il in grid_rows:
            o_ref[il * GRID_W:(il + 1) * GRID_W, :] = jnp.where(low, o[2 * il], o[2 * il + 1]).astype(o_ref.dtype)


def na_attention(uc, bias, n_ctx):
    n = uc.shape[0]
    width = uc.shape[1] // 3
    npair = width // LANES
    n_rows = (n - n_ctx) // GRID_W
    return pl.pallas_call(
        functools.partial(_na_body, n_ctx=n_ctx, n_rows=n_rows),
        grid=(npair, n // ROW_TILE),
        in_specs=[
            pl.BlockSpec((ROW_TILE, LANES), lambda p, j: (j, p)),
            pl.BlockSpec((n, LANES), lambda p, j: (0, npair + p)),
            pl.BlockSpec((n, LANES), lambda p, j: (0, 2 * npair + p)),
            pl.BlockSpec((2, WIN_ROWS, GRID_W, WIN_ROWS * GRID_W), lambda p, j: (p, 0, 0, 0)),
        ],
        out_specs=pl.BlockSpec((ROW_TILE, LANES), lambda p, j: (j, p)),
        out_shape=jax.ShapeDtypeStruct((n, width), bf16),
        compiler_params=_params(("arbitrary", "arbitrary")),
        name="na_attention",
    )(uc, uc, uc, bias)


def _rope_tables(n_ctx, seq):
    t = jnp.arange(seq)
    row = (t // GRID_W).astype(f32)
    col = (t % GRID_W).astype(f32)
    n_freq = QK_ROPE // 4
    inv = ROPE_BASE ** (-jnp.arange(n_freq, dtype=f32) / n_freq)
    ar, ac = row[:, None] * inv[None, :], col[:, None] * inv[None, :]
    cos = jnp.concatenate([jnp.cos(ar), jnp.cos(ar), jnp.cos(ac), jnp.cos(ac)], axis=1)
    sin = jnp.concatenate([-jnp.sin(ar), jnp.sin(ar), -jnp.sin(ac), jnp.sin(ac)], axis=1)
    lat = jnp.concatenate([cos, sin], axis=1)
    ctx = jnp.concatenate([jnp.ones((n_ctx, QK_ROPE), f32), jnp.zeros((n_ctx, QK_ROPE), f32)], axis=1)
    return jnp.concatenate([ctx, lat], axis=0)


def _pair_swap_perm():
    q = QK_ROPE // 4
    return np.concatenate([np.arange(q, 2 * q), np.arange(0, q), np.arange(3 * q, 4 * q), np.arange(2 * q, 3 * q)])


def _na_bias_tables(rpb):
    cols = np.arange(GRID_W)
    c0 = np.clip(cols - WIN_COLS // 2, 0, GRID_W - WIN_COLS)
    inside = (cols[None, :] >= c0[:, None]) & (cols[None, :] < c0[:, None] + WIN_COLS)
    rel = np.clip(cols[None, :] - cols[:, None] + WIN_COLS - 1, 0, 2 * WIN_COLS - 2)
    full = jnp.where(inside[None, None], rpb[:, :, rel], NEG_BIG)
    tabs = [jnp.concatenate([full[:, d0 + jj] for jj in range(WIN_ROWS)], axis=-1) for d0 in range(WIN_ROWS)]
    return jnp.stack(tabs, axis=1).astype(f32)


def _layer_weights(l, w_in, rwkv_conv, rwkv_w0, rwkv_w_up, rwkv_a0, rwkv_a_up, rwkv_g_up, rwkv_k_k, rwkv_k_a,
                   rwkv_r_k, mla_q_norm_g, mla_w_uq, mla_kv_norm_g, mla_w_ukv, w_branch, w_out, mlp_w1, mlp_w2):
    d = w_in.shape[1]
    aw = rwkv_k_k.shape[1]
    nh_b = mla_w_ukv.shape[2] // (QK_NOPE + V_DIM)
    n_lora = 2 * LORA_W + 2 * LORA_A + LORA_G
    c_lora = 3 * aw
    c_mla = c_lora + n_lora
    c_na = c_mla + Q_LORA + KV_LORA + QK_ROPE
    c_gate = c_na + 3 * aw
    wi = w_in[l]
    perm = _pair_swap_perm()
    kr = wi[:, c_mla + Q_LORA + KV_LORA:c_na]
    w_b = jnp.concatenate([
        wi[:, c_lora:c_mla],
        wi[:, c_mla + Q_LORA:c_mla + Q_LORA + KV_LORA],
        wi[:, c_mla:c_mla + Q_LORA], jnp.zeros((d, 512 - Q_LORA), f32),
        kr, kr[:, perm]], axis=1).astype(bf16)
    cols = dict(c_kvd=n_lora, c_q=n_lora + KV_LORA, c_kr=n_lora + KV_LORA + 512)

    def lora_pad(w, start):
        r = w.shape[1]
        out = jnp.zeros((2, n_lora, aw), f32)
        for z in range(2):
            out = out.at[z, start + z * r:start + (z + 1) * r].set(w[z])
        return out.astype(bf16)

    g_up = jnp.zeros((n_lora, aw), f32).at[2 * LORA_W + 2 * LORA_A:].set(rwkv_g_up[l]).astype(bf16)
    head_id = np.arange(2 * LANES) // A_HEAD_DIM
    seg = jnp.asarray(head_id[:, None] == head_id[None, :], bf16)
    rwkv = dict(conv_rkv=rwkv_conv[l][:, :c_lora], conv_l=rwkv_conv[l][:, c_lora:c_mla],
                w_up=lora_pad(rwkv_w_up[l], 0), a_up=lora_pad(rwkv_a_up[l], 2 * LORA_W), g_up=g_up,
                w0=rwkv_w0[l], a0=rwkv_a0[l], k_k=rwkv_k_k[l][None], k_a=rwkv_k_a[l][None],
                r_k=rwkv_r_k[l].reshape(1, aw), seg=seg)

    uq = mla_w_uq[l].reshape(Q_LORA, nh_b, QK_NOPE + QK_ROPE)
    uq = jnp.concatenate([uq, jnp.zeros((512 - Q_LORA, nh_b, QK_NOPE + QK_ROPE), f32)], axis=0)
    zpad = jnp.zeros((512, nh_b, LANES - QK_ROPE), f32)
    pe = uq[:, :, QK_NOPE:]
    ukv = mla_w_ukv[l].reshape(KV_LORA, nh_b, QK_NOPE + V_DIM)
    mla = dict(
        gq=jnp.concatenate([mla_q_norm_g[l], jnp.zeros((512 - Q_LORA,), f32)])[None],
        gkv=mla_kv_norm_g[l][None],
        wq_nope=uq[:, :, :QK_NOPE].reshape(512, nh_b * QK_NOPE).astype(bf16),
        wq_pe=jnp.concatenate([pe, zpad], axis=2).reshape(512, nh_b * LANES).astype(bf16),
        wq_pr=jnp.concatenate([pe[:, :, perm], zpad], axis=2).reshape(512, nh_b * LANES).astype(bf16),
        wkv=jnp.concatenate([ukv[:, :, :QK_NOPE].reshape(KV_LORA, -1), ukv[:, :, QK_NOPE:].reshape(KV_LORA, -1)],
                            axis=1).astype(bf16))
    return dict(w_rkv=wi[:, :c_lora].astype(bf16), w_b=w_b, w_na=wi[:, c_na:c_gate].astype(bf16),
                w_gate=wi[:, c_gate:].astype(bf16), cols=cols, rwkv=rwkv, mla=mla,
                w_branch=w_branch[l].astype(bf16), w_out=w_out[l].astype(bf16),
                mlp_w1=mlp_w1[l].astype(bf16), mlp_w2=mlp_w2[l].astype(bf16))


def kernel(x, c, ctx, c_ctx, ada_w, ada_b, norm_mix_g, norm_mlp_g, w_in, rwkv_conv, rwkv_w0, rwkv_w_up, rwkv_a0, rwkv_a_up, rwkv_g_up, rwkv_k_k, rwkv_k_a, rwkv_r_k, rwkv_ln_g, rwkv_ln_b, mla_q_norm_g, mla_w_uq, mla_kv_norm_g, mla_w_ukv, na_rpb, w_branch, w_out, mlp_w1, mlp_w2, final_norm_g):
    batch, seq, d = x.shape
    n_ctx = ctx.shape[1]
    depth = ada_w.shape[0]
    assert batch == 1 and n_ctx == ROW_TILE and seq % ROW_TILE == 0 and seq // GRID_W >= WIN_ROWS
    n_ctx_tiles = n_ctx // ROW_TILE

    xs = jnp.concatenate([ctx[0], x[0]], axis=0)
    cond = jnp.concatenate([c, c_ctx[None], jnp.zeros((6, d), f32)], axis=0)
    mods = ada_modulation(cond, ada_w, ada_b)
    cs = _rope_tables(n_ctx, seq)

    for l in range(depth):
        lw = _layer_weights(l, w_in, rwkv_conv, rwkv_w0, rwkv_w_up, rwkv_a0, rwkv_a_up, rwkv_g_up, rwkv_k_k,
                            rwkv_k_a, rwkv_r_k, mla_q_norm_g, mla_w_uq, mla_kv_norm_g, mla_w_ukv, w_branch,
                            w_out, mlp_w1, mlp_w2)
        mods8 = mods[l]
        mods2 = mods8[:2, None, :]
        h = norm_modulate(xs, norm_mix_g[l], mods2, 0)
        u3 = matmul(h, lw["w_rkv"], f32, 512)
        ub = matmul(h, lw["w_b"], f32, lw["w_b"].shape[1])
        uc = matmul(h, lw["w_na"], bf16, 512)
        ug = matmul(h, lw["w_gate"], f32, 512)

        r, v, kk, bonus, g, lwd, kd, bd = rwkv_prepare(u3, ub, lw["rwkv"], n_ctx)
        y = rwkv_chunk_scan(r, v, kk, lwd, kd, bd, n_ctx)
        ya = rwkv_readout(y, bonus, g, rwkv_ln_g[l], rwkv_ln_b[l], lw["rwkv"]["seg"][:LANES, :LANES])

        q_b, k_b, v_b = mla_prepare(ub, cs, lw["mla"], lw["cols"])
        yb = mla_attention(q_b, k_b, v_b, n_ctx)

        yc = na_attention(uc, _na_bias_tables(na_rpb[l]), n_ctx)

        merged = merge_branches(ya, yb, yc, ug, lw["w_branch"])
        xs = matmul_gated_residual(merged, lw["w_out"], xs, mods8, 2, n_ctx)
        h2 = norm_modulate(xs, norm_mlp_g[l], mods2, 3)
        xs = mlp_gated_residual(h2, lw["mlp_w1"], lw["mlp_w2"], xs, mods8, 5, n_ctx)

    return final_norm(xs, final_norm_g, n_ctx_tiles)[None]
```

```python
import functools
import math

import jax
import jax.numpy as jnp
import numpy as np
from jax import lax
from jax.experimental import pallas as pl
from jax.experimental.pallas import tpu as pltpu

f32 = jnp.float32
bf16 = jnp.bfloat16

GRID_W = 64
RMS_EPS = 1e-6
N_BRANCH = 3
A_HEAD_DIM = 64
LORA_W = 96
LORA_A = 96
LORA_G = 256
GN_EPS = 64e-5
QK_NOPE = 128
QK_ROPE = 64
V_DIM = 128
Q_LORA = 448
KV_LORA = 128
MLA_SCALE = (QK_NOPE + QK_ROPE) ** -0.5
MLA_Q_SCALE = MLA_SCALE * math.log2(math.e)
ROPE_BASE = 10000.0
C_HEAD_DIM = 64
WIN_ROWS = 8
WIN_COLS = 16
NA_SCALE = C_HEAD_DIM ** -0.5
DECAY_SCALE = math.exp(-0.5)

LANES = 128
ROW_TILE = 256
CHUNK = 64
CHUNKS_PER_STEP = 2
PREP_TILE = 128
NA_PAIRS_PER_STEP = 2
MLA_HEADS_PER_STEP = 2
VMEM_LIMIT = 48 * 1024 * 1024
NEG_BIG = -1e30


def _params(sem):
    return pltpu.CompilerParams(dimension_semantics=sem, vmem_limit_bytes=VMEM_LIMIT)


def _dot(a, b):
    return jnp.dot(a, b, preferred_element_type=f32)


def _dot_nt(a, b):
    return lax.dot_general(a, b, (((1,), (1,)), ((), ())), preferred_element_type=f32)


def _ada_body(s_ref, w_ref, b_ref, o_ref):
    s = s_ref[...]
    s = s * jax.nn.sigmoid(s)
    o_ref[...] = _dot(s, w_ref[...]) + b_ref[...]


def ada_modulation(cond, ada_w, ada_b):
    nl, d, n6 = ada_w.shape
    tn = 1024
    return pl.pallas_call(
        _ada_body,
        grid=(nl, n6 // tn),
        in_specs=[
            pl.BlockSpec((8, d), lambda l, j: (0, 0)),
            pl.BlockSpec((None, d, tn), lambda l, j: (l, 0, j)),
            pl.BlockSpec((None, 1, tn), lambda l, j: (l, 0, j)),
        ],
        out_specs=pl.BlockSpec((None, 8, tn), lambda l, j: (l, 0, j)),
        out_shape=jax.ShapeDtypeStruct((nl, 8, n6), f32),
        compiler_params=_params(("arbitrary", "arbitrary")),
        name="ada_modulation",
    )(cond, ada_w, ada_b.reshape(nl, 1, n6))


def _norm_mod_body(x_ref, g_ref, m_ref, o_ref, *, off, d):
    x = x_ref[...]
    y = x * lax.rsqrt(jnp.mean(x * x, axis=-1, keepdims=True) + RMS_EPS) * g_ref[...]
    shift = m_ref[:, off * d:(off + 1) * d]
    scale = m_ref[:, (off + 1) * d:(off + 2) * d]
    o_ref[...] = (y * (1.0 + scale) + shift).astype(o_ref.dtype)


def norm_modulate(x, g, mods2, off):
    n, d = x.shape
    return pl.pallas_call(
        functools.partial(_norm_mod_body, off=off, d=d),
        grid=(n // ROW_TILE,),
        in_specs=[
            pl.BlockSpec((ROW_TILE, d), lambda i: (i, 0)),
            pl.BlockSpec((1, d), lambda i: (0, 0)),
            pl.BlockSpec((None, 1, mods2.shape[-1]), lambda i: (jnp.where(i == 0, 1, 0), 0, 0)),
        ],
        out_specs=pl.BlockSpec((ROW_TILE, d), lambda i: (i, 0)),
        out_shape=jax.ShapeDtypeStruct((n, d), bf16),
        compiler_params=_params(("arbitrary",)),
        name="norm_modulate",
    )(x, g.reshape(1, d), mods2)


def _final_norm_body(x_ref, g_ref, o_ref):
    x = x_ref[...]
    o_ref[...] = x * lax.rsqrt(jnp.mean(x * x, axis=-1, keepdims=True) + RMS_EPS) * g_ref[...]


def final_norm(x, g, n_ctx_tiles):
    n, d = x.shape
    t = n - n_ctx_tiles * ROW_TILE
    return pl.pallas_call(
        _final_norm_body,
        grid=(t // ROW_TILE,),
        in_specs=[
            pl.BlockSpec((ROW_TILE, d), lambda i: (i + n_ctx_tiles, 0)),
            pl.BlockSpec((1, d), lambda i: (0, 0)),
        ],
        out_specs=pl.BlockSpec((ROW_TILE, d), lambda i: (i, 0)),
        out_shape=jax.ShapeDtypeStruct((t, d), f32),
        compiler_params=_params(("arbitrary",)),
        name="final_norm",
    )(x, g.reshape(1, d))


def _mm_body(a_ref, w_ref, o_ref, *, sigmoid):
    acc = _dot(a_ref[...], w_ref[...])
    o_ref[...] = (jax.nn.sigmoid(acc) if sigmoid else acc).astype(o_ref.dtype)


def _row_tile(n):
    for tm in (1056, 768, 512, 384, 256):
        if n % tm == 0:
            return tm
    raise ValueError(f"no row tile for {n} rows")


def matmul(a, w, out_dtype, tn, sigmoid=False):
    m, k = a.shape
    n = w.shape[1]
    tm = _row_tile(m)
    return pl.pallas_call(
        functools.partial(_mm_body, sigmoid=sigmoid),
        grid=(m // tm, n // tn),
        in_specs=[pl.BlockSpec((tm, k), lambda i, j: (i, 0)), pl.BlockSpec((k, tn), lambda i, j: (0, j))],
        out_specs=pl.BlockSpec((tm, tn), lambda i, j: (i, j)),
        out_shape=jax.ShapeDtypeStruct((m, n), out_dtype),
        compiler_params=_params(("arbitrary", "arbitrary")),
        name="matmul",
    )(a, w)


def _gate_rows(m_ref, tm, n_ctx):
    rows = pl.program_id(0) * tm + lax.broadcasted_iota(jnp.int32, (tm, 1), 0)
    return jnp.where(rows < n_ctx, m_ref[1:2, :], m_ref[0:1, :])


def _mm_res_body(a_ref, w_ref, x_ref, m_ref, o_ref, *, tm, n_ctx):
    acc = _dot(a_ref[...], w_ref[...])
    o_ref[...] = x_ref[...] + _gate_rows(m_ref, tm, n_ctx) * acc


def matmul_gated_residual(a, w, x, mods8, gate_off, n_ctx, tn=512):
    m, k = a.shape
    n = w.shape[1]
    tm = _row_tile(m)
    nj = n // tn
    return pl.pallas_call(
        functools.partial(_mm_res_body, tm=tm, n_ctx=n_ctx),
        grid=(m // tm, nj),
        in_specs=[
            pl.BlockSpec((tm, k), lambda i, j: (i, 0)),
            pl.BlockSpec((k, tn), lambda i, j: (0, j)),
            pl.BlockSpec((tm, tn), lambda i, j: (i, j)),
            pl.BlockSpec((8, tn), lambda i, j: (0, gate_off * nj + j)),
        ],
        out_specs=pl.BlockSpec((tm, tn), lambda i, j: (i, j)),
        out_shape=jax.ShapeDtypeStruct((m, n), f32),
        compiler_params=_params(("arbitrary", "arbitrary")),
        name="matmul_gated_residual",
    )(a, w, x, mods8)


def _merge_body(ya_ref, yb_ref, yc_ref, g0_ref, g1_ref, g2_ref, w_ref, o_ref):
    acc = g0_ref[...] * _dot(ya_ref[...], w_ref[0])
    acc = acc + g1_ref[...] * _dot(yb_ref[...], w_ref[1])
    acc = acc + g2_ref[...] * _dot(yc_ref[...], w_ref[2])
    o_ref[...] = acc.astype(o_ref.dtype)


def merge_branches(ya, yb, yc, gates, w_branch, tn=512):
    m, k = ya.shape
    d = w_branch.shape[-1]
    tm = 528 if m % 528 == 0 else ROW_TILE
    nj = d // tn
    y_spec = pl.BlockSpec((tm, k), lambda i, j: (i, 0))
    return pl.pallas_call(
        _merge_body,
        grid=(m // tm, nj),
        in_specs=[
            y_spec, y_spec, y_spec,
            pl.BlockSpec((tm, tn), lambda i, j: (i, j)),
            pl.BlockSpec((tm, tn), lambda i, j: (i, nj + j)),
            pl.BlockSpec((tm, tn), lambda i, j: (i, 2 * nj + j)),
            pl.BlockSpec((N_BRANCH, k, tn), lambda i, j: (0, 0, j)),
        ],
        out_specs=pl.BlockSpec((tm, tn), lambda i, j: (i, j)),
        out_shape=jax.ShapeDtypeStruct((m, d), bf16),
        compiler_params=_params(("arbitrary", "arbitrary")),
        name="merge_branches",
    )(ya, yb, yc, gates, gates, gates, w_branch)


def _mlp_body(h_ref, w1_ref, w2_ref, x_ref, m_ref, o_ref, acc_ref, *, tm, n_ctx):
    f = pl.program_id(1)

    @pl.when(f == 0)
    def _():
        acc_ref[...] = jnp.zeros_like(acc_ref)

    h1 = jnp.maximum(_dot(h_ref[...], w1_ref[...]), 0.0)
    acc_ref[...] += _dot((h1 * h1).astype(bf16), w2_ref[...])

    @pl.when(f == pl.num_programs(1) - 1)
    def _():
        o_ref[...] = x_ref[...] + _gate_rows(m_ref, tm, n_ctx) * acc_ref[...]


def mlp_gated_residual(h, w1, w2, x, mods8, gate_off, n_ctx, tf=512):
    m, d = h.shape
    dff = w1.shape[1]
    tm = 528 if m % 528 == 0 else ROW_TILE
    return pl.pallas_call(
        functools.partial(_mlp_body, tm=tm, n_ctx=n_ctx),
        grid=(m // tm, dff // tf),
        in_specs=[
            pl.BlockSpec((tm, d), lambda i, f: (i, 0)),
            pl.BlockSpec((d, tf), lambda i, f: (0, f)),
            pl.BlockSpec((tf, d), lambda i, f: (f, 0)),
            pl.BlockSpec((tm, d), lambda i, f: (i, 0)),
            pl.BlockSpec((8, d), lambda i, f: (0, gate_off)),
        ],
        out_specs=pl.BlockSpec((tm, d), lambda i, f: (i, 0)),
        out_shape=jax.ShapeDtypeStruct((m, d), f32),
        scratch_shapes=[pltpu.VMEM((tm, d), f32)],
        compiler_params=_params(("arbitrary", "arbitrary")),
        name="mlp_gated_residual",
    )(h, w1, w2, x, mods8)


def _seg_sum(x, e_ref):
    hi = x.astype(bf16)
    lo = (x - hi.astype(f32)).astype(bf16)
    e = e_ref[...]
    w = e.shape[0]
    parts = [_dot(hi[:, g:g + w], e) + _dot(lo[:, g:g + w], e) for g in range(0, x.shape[1], w)]
    return parts[0] if len(parts) == 1 else jnp.concatenate(parts, axis=1)


def _to_pairs(o_ref, val, lead=None):
    for p in range(val.shape[-1] // LANES):
        piece = val[:, p * LANES:(p + 1) * LANES]
        if lead is None:
            o_ref[p] = piece
        else:
            o_ref[lead, p] = piece


def _rwkv_prep_body(u_ref, up_ref, un_ref, l_ref, lp_ref, ln_ref, cw_ref, cwl_ref, wu_ref, au_ref, gu_ref,
                    w0_ref, a0_ref, kk_ref, ka_ref, rk_ref, e_ref,
                    r_o, v_o, kkn_o, bonus_o, g_o, lw_o, kd_o, bd_o, *, n_ctx_tiles, aw):
    i = pl.program_id(0)
    last = pl.num_programs(0) - 1
    left_zero = (i == 0) | (i == n_ctx_tiles)
    right_zero = (i == n_ctx_tiles - 1) | (i == last)
    rows = lax.broadcasted_iota(jnp.int32, (PREP_TILE, 1), 0)

    def conv(x_ref, xp_ref, xn_ref, w_ref):
        x = x_ref[...]
        prev_row = jnp.where(left_zero, 0.0, xp_ref[7:8, :])
        next_row = jnp.where(right_zero, 0.0, xn_ref[0:1, :])
        x_prev = jnp.where(rows == 0, prev_row, pltpu.roll(x, 1, 0))
        x_next = jnp.where(rows == PREP_TILE - 1, next_row, pltpu.roll(x, PREP_TILE - 1, 0))
        return w_ref[0:1, :] * x_prev + w_ref[1:2, :] * x + w_ref[2:3, :] * x_next

    y = conv(u_ref, up_ref, un_ref, cw_ref)
    r, k, v = y[:, :aw], y[:, aw:2 * aw], y[:, 2 * aw:]
    yl = conv(l_ref, lp_ref, ln_ref, cwl_ref)
    th = jnp.tanh(yl).astype(bf16)
    sg = jax.nn.sigmoid(yl).astype(bf16)
    ylb = yl.astype(bf16)

    _to_pairs(g_o, _dot(sg, gu_ref[...]))
    _to_pairs(r_o, r)
    _to_pairs(v_o, v)

    kkr = k * kk_ref[...]
    norm = jnp.sqrt(_seg_sum(kkr * kkr, e_ref))
    kkn = kkr / jnp.maximum(norm, 1e-12)
    _to_pairs(kkn_o, kkn)

    ksum = None
    for dr in range(2):
        z = w0_ref[dr:dr + 1, :] + _dot(th, wu_ref[dr])
        _to_pairs(lw_o, -DECAY_SCALE * jax.nn.sigmoid(z), lead=dr)
        a = jax.nn.sigmoid(a0_ref[dr:dr + 1, :] + _dot(ylb, au_ref[dr]))
        kd = k * (1.0 + (a - 1.0) * ka_ref[...])
        _to_pairs(kd_o, kd, lead=dr)
        _to_pairs(bd_o, kkn * a, lead=dr)
        ksum = kd if ksum is None else ksum + kd
    rk = _seg_sum(r * (0.5 * ksum) * rk_ref[...], e_ref)
    _to_pairs(bonus_o, rk * v)


def rwkv_prepare(u3, ub, pw, n_ctx):
    n = u3.shape[0]
    aw = u3.shape[1] // 3
    npair = aw // LANES
    lw = pw["conv_l"].shape[1]
    tpb = PREP_TILE // 8
    nb8 = n // 8

    def prev_map(i):
        return (jnp.maximum(i * tpb - 1, 0), 0)

    def next_map(i):
        return (jnp.minimum((i + 1) * tpb, nb8 - 1), 0)

    full = lambda a: pl.BlockSpec(a.shape, lambda i: (0,) * a.ndim)
    pm = jax.ShapeDtypeStruct((npair, n, LANES), f32)
    pm2 = jax.ShapeDtypeStruct((2, npair, n, LANES), f32)
    pm_spec = pl.BlockSpec((npair, PREP_TILE, LANES), lambda i: (0, i, 0))
    pm2_spec = pl.BlockSpec((2, npair, PREP_TILE, LANES), lambda i: (0, 0, i, 0))
    consts = [pw["conv_rkv"], pw["conv_l"], pw["w_up"], pw["a_up"], pw["g_up"], pw["w0"], pw["a0"],
              pw["k_k"], pw["k_a"], pw["r_k"], pw["seg"]]
    return pl.pallas_call(
        functools.partial(_rwkv_prep_body, n_ctx_tiles=n_ctx // PREP_TILE, aw=aw),
        grid=(n // PREP_TILE,),
        in_specs=[
            pl.BlockSpec((PREP_TILE, 3 * aw), lambda i: (i, 0)),
            pl.BlockSpec((8, 3 * aw), prev_map),
            pl.BlockSpec((8, 3 * aw), next_map),
            pl.BlockSpec((PREP_TILE, lw), lambda i: (i, 0)),
            pl.BlockSpec((8, lw), prev_map),
            pl.BlockSpec((8, lw), next_map),
        ] + [full(a) for a in consts],
        out_specs=[pm_spec] * 5 + [pm2_spec] * 3,
        out_shape=[pm] * 5 + [pm2] * 3,
        compiler_params=_params(("arbitrary",)),
        name="rwkv_prepare",
    )(u3, u3, u3, ub, ub, ub, *consts)


def _dot_tn(a, b):
    return lax.dot_general(a, b, (((0,), (0,)), ((), ())), preferred_element_type=f32)


def _rwkv_chunk_body(r_ref, v_ref, kk_ref, lw_ref, k_ref, b_ref, y_ref, h_ref, *, n_pairs):
    d = pl.program_id(0)
    j = pl.program_id(1)
    c = CHUNK
    c2 = 2 * c

    @pl.when(j == 0)
    def _():
        h_ref[...] = jnp.zeros_like(h_ref)

    sign = jnp.where(d == 0, 1, -1)
    ti = lax.broadcasted_iota(jnp.int32, (c, c), 0)
    si = lax.broadcasted_iota(jnp.int32, (c, c), 1)
    incl_c = jnp.where((ti - si) * sign >= 0, 1.0, 0.0).astype(bf16)
    t2 = lax.broadcasted_iota(jnp.int32, (c2, c2), 0)
    s2 = lax.broadcasted_iota(jnp.int32, (c2, c2), 1)
    ahead = (t2 - s2) * sign
    blk = lambda n: (t2 // n) == (s2 // n)
    incl = blk(c) & (ahead >= 0)
    strict = blk(c) & (ahead > 0)
    eye = jnp.where(t2 == s2, 1.0, 0.0)
    inv_levels = []
    n = 2
    while n < c:
        inv_levels.append(strict & blk(2 * n) & ~blk(n))
        n *= 2
    low = lax.broadcasted_iota(jnp.int32, (1, LANES), 1) < A_HEAD_DIM

    def stack(x):
        return jnp.concatenate([jnp.where(low, x, 0.0), jnp.where(low, 0.0, x)], axis=0).astype(bf16)

    pairs = range(n_pairs)
    first = jnp.where(d == 0, 0, CHUNKS_PER_STEP - 1)
    offs = [pl.multiple_of((first + sign * sc) * c, c) for sc in range(CHUNKS_PER_STEP)]
    units = [(sc, p) for sc in range(CHUNKS_PER_STEP) for p in pairs]
    each = lambda f: [f(sc, p) for sc, p in units]
    at = lambda sc, p: sc * n_pairs + p
    ld = lambda ref, sc, p: ref[p, pl.ds(offs[sc], c), :]
    lw = each(lambda sc, p: ld(lw_ref, sc, p))
    lw_hi = [x.astype(bf16) for x in lw]
    lw_lo = [(x - hi.astype(f32)).astype(bf16) for x, hi in zip(lw, lw_hi)]
    cum = [_dot(incl_c, hi) + _dot(incl_c, lo) for hi, lo in zip(lw_hi, lw_lo)]
    tot = [jnp.sum(x, axis=0, keepdims=True) for x in lw]
    e_neg = [jnp.exp(-x) for x in cum]
    e_end = [jnp.exp(t - x) for t, x in zip(tot, cum)]
    r_s = each(lambda sc, p: stack(ld(r_ref, sc, p) * jnp.exp(cum[at(sc, p)])))
    kk_s = each(lambda sc, p: stack(ld(kk_ref, sc, p) * jnp.exp(cum[at(sc, p)] - lw[at(sc, p)])))
    kb_s = each(lambda sc, p: jnp.concatenate([stack(ld(b_ref, sc, p) * e_neg[at(sc, p)]),
                                               stack(ld(k_ref, sc, p) * e_neg[at(sc, p)])], axis=0))
    kbd_s = each(lambda sc, p: jnp.concatenate([stack(ld(k_ref, sc, p) * e_end[at(sc, p)]),
                                                stack(ld(b_ref, sc, p) * e_end[at(sc, p)])], axis=0))
    v_s = each(lambda sc, p: stack(ld(v_ref, sc, p)))
    a1 = [_dot_nt(a, b) for a, b in zip(kk_s, kb_s)]
    a2 = [_dot_nt(a, b) for a, b in zip(r_s, kb_s)]
    a_kb = [jnp.where(strict, a[:, :c2], 0.0) for a in a1]
    a_kk = [jnp.where(strict, a[:, c2:], 0.0).astype(bf16) for a in a1]
    a_rb = [jnp.where(incl, a[:, :c2], 0.0).astype(bf16) for a in a2]
    a_rk = [jnp.where(incl, a[:, c2:], 0.0).astype(bf16) for a in a2]
    t_inv = [eye - jnp.where(blk(2), a, 0.0) for a in a_kb]
    for m in inv_levels:
        t_b = [t.astype(bf16) for t in t_inv]
        x = [_dot(t, jnp.where(m, a, 0.0).astype(bf16)).astype(bf16) for t, a in zip(t_b, a_kb)]
        t_inv = [t - _dot(xx, tb) for t, xx, tb in zip(t_inv, x, t_b)]
    g = [_dot(t.astype(bf16), jnp.concatenate([kk, akk], axis=1)).astype(bf16)
         for t, kk, akk in zip(t_inv, kk_s, a_kk)]
    w_col = [jnp.sum(eye * jnp.exp(t), axis=1, keepdims=True) for t in tot]
    h = [h_ref[p] for p in pairs]
    for sc in range(CHUNKS_PER_STEP):
        ix = [at(sc, p) for p in pairs]
        h_b = [x.astype(bf16) for x in h]
        u = [_dot(g[i], jnp.concatenate([h_b[p], v_s[i]], axis=0)).astype(bf16) for p, i in enumerate(ix)]
        y = [_dot(jnp.concatenate([r_s[i], a_rk[i], -a_rb[i]], axis=1), jnp.concatenate([h_b[p], v_s[i], u[p]], axis=0))
             for p, i in enumerate(ix)]
        h1 = [_dot_tn(kbd_s[i], jnp.concatenate([v_s[i], -u[p]], axis=0)) for p, i in enumerate(ix)]
        for p in pairs:
            y_ref[p, pl.ds(offs[sc], c), :] = y[p][:c] + y[p][c:]
        h = [h[p] * w_col[i] + h1[p] for p, i in enumerate(ix)]
    for p in pairs:
        h_ref[p] = h[p]


def rwkv_chunk_scan(r, v, kk, lw, kd, bd, n_ctx):
    npair, n, _ = r.shape
    blk_rows = CHUNKS_PER_STEP * CHUNK
    nc = n // blk_rows
    ncc = n_ctx // blk_rows

    def cidx(d, j):
        bwd = jnp.where(j < ncc, ncc - 1 - j, nc - 1 - (j - ncc))
        return jnp.where(d == 0, j, bwd)

    shared = pl.BlockSpec((npair, blk_rows, LANES), lambda d, j: (0, cidx(d, j), 0))
    per_dir = pl.BlockSpec((None, npair, blk_rows, LANES), lambda d, j: (d, 0, cidx(d, j), 0))
    return pl.pallas_call(
        functools.partial(_rwkv_chunk_body, n_pairs=npair),
        grid=(2, nc),
        in_specs=[shared, shared, shared, per_dir, per_dir, per_dir],
        out_specs=per_dir,
        out_shape=jax.ShapeDtypeStruct((2, npair, n, LANES), f32),
        scratch_shapes=[pltpu.VMEM((npair, LANES, LANES), f32)],
        compiler_params=_params(("arbitrary", "arbitrary")),
        name="rwkv_chunk_scan",
    )(r, v, kk, lw, kd, bd)


def _rwkv_readout_body(y_ref, bonus_ref, g_ref, lng_ref, lnb_ref, e_ref, o_ref, *, n_pairs):
    inv = 1.0 / A_HEAD_DIM
    for p in range(n_pairs):
        y = y_ref[0, p] + y_ref[1, p]
        yc = y - _seg_sum(y, e_ref) * inv
        var = _seg_sum(yc * yc, e_ref) * inv
        yn = yc * lax.rsqrt(var + GN_EPS) * lng_ref[p] + lnb_ref[p]
        o_ref[:, p * LANES:(p + 1) * LANES] = ((yn + bonus_ref[p]) * g_ref[p]).astype(o_ref.dtype)


def rwkv_readout(y, bonus, g, ln_g, ln_b, seg_pair):
    _, npair, n, _ = y.shape
    pm_spec = pl.BlockSpec((npair, ROW_TILE, LANES), lambda i: (0, i, 0))
    vec_spec = pl.BlockSpec((npair, 1, LANES), lambda i: (0, 0, 0))
    return pl.pallas_call(
        functools.partial(_rwkv_readout_body, n_pairs=npair),
        grid=(n // ROW_TILE,),
        in_specs=[pl.BlockSpec((2, npair, ROW_TILE, LANES), lambda i: (0, 0, i, 0)), pm_spec, pm_spec,
                  vec_spec, vec_spec, pl.BlockSpec((LANES, LANES), lambda i: (0, 0))],
        out_specs=pl.BlockSpec((ROW_TILE, npair * LANES), lambda i: (i, 0)),
        out_shape=jax.ShapeDtypeStruct((n, npair * LANES), bf16),
        compiler_params=_params(("arbitrary",)),
        name="rwkv_readout",
    )(y, bonus, g, ln_g.reshape(npair, 1, LANES), ln_b.reshape(npair, 1, LANES), seg_pair)


def _mla_prep_body(u_ref, cs_ref, gq_ref, gkv_ref, wqn_ref, wqp_ref, wqr_ref, wkv_ref, q_o, k_o, v_o,
                   *, n_heads, c_kvd, c_q, c_kr):
    ql = u_ref[:, c_q:c_q + 512]
    qn = ql * lax.rsqrt(jnp.sum(ql * ql, axis=-1, keepdims=True) * (1.0 / Q_LORA) + RMS_EPS) * gq_ref[...]
    qn = qn.astype(bf16)
    kvd = u_ref[:, c_kvd:c_kvd + KV_LORA]
    kvn = kvd * lax.rsqrt(jnp.mean(kvd * kvd, axis=-1, keepdims=True) + RMS_EPS) * gkv_ref[...]
    kv = _dot(kvn.astype(bf16), wkv_ref[...])
    cs = cs_ref[...]
    lane = lax.broadcasted_iota(jnp.int32, (1, LANES), 1)
    low = lane < QK_ROPE
    cos_t = jnp.where(low, cs, 0.0)
    sin_t = jnp.where(low, pltpu.roll(cs, QK_ROPE, 1), 0.0)
    kr = u_ref[:, c_kr:c_kr + LANES]
    prod = kr * cs
    k_pe = jnp.where(low, prod + pltpu.roll(prod, QK_ROPE, 1), 0.0).astype(bf16)
    q_nope = _dot(qn, wqn_ref[...])
    q_pe = _dot(qn, wqp_ref[...])
    q_pr = _dot(qn, wqr_ref[...])
    for h in range(n_heads):
        sl = slice(h * LANES, (h + 1) * LANES)
        q_o[h, :, 0:LANES] = (q_nope[:, sl] * MLA_Q_SCALE).astype(bf16)
        q_o[h, :, LANES:2 * LANES] = ((q_pe[:, sl] * cos_t + q_pr[:, sl] * sin_t) * MLA_Q_SCALE).astype(bf16)
        k_o[h, :, 0:LANES] = kv[:, sl].astype(bf16)
        k_o[h, :, LANES:2 * LANES] = k_pe
        v_o[h] = kv[:, n_heads * LANES + h * LANES:n_heads * LANES + (h + 1) * LANES].astype(bf16)


def mla_prepare(ub, cs, pw, cols):
    n = ub.shape[0]
    nh = pw["wq_nope"].shape[1] // LANES
    full = lambda a: pl.BlockSpec(a.shape, lambda i: (0,) * a.ndim)
    consts = [pw["gq"], pw["gkv"], pw["wq_nope"], pw["wq_pe"], pw["wq_pr"], pw["wkv"]]
    return pl.pallas_call(
        functools.partial(_mla_prep_body, n_heads=nh, **cols),
        grid=(n // ROW_TILE,),
        in_specs=[pl.BlockSpec((ROW_TILE, ub.shape[1]), lambda i: (i, 0)),
                  pl.BlockSpec((ROW_TILE, LANES), lambda i: (i, 0))] + [full(a) for a in consts],
        out_specs=[pl.BlockSpec((nh, ROW_TILE, 2 * LANES), lambda i: (0, i, 0)),
                   pl.BlockSpec((nh, ROW_TILE, 2 * LANES), lambda i: (0, i, 0)),
                   pl.BlockSpec((nh, ROW_TILE, LANES), lambda i: (0, i, 0))],
        out_shape=[jax.ShapeDtypeStruct((nh, n, 2 * LANES), bf16),
                   jax.ShapeDtypeStruct((nh, n, 2 * LANES), bf16),
                   jax.ShapeDtypeStruct((nh, n, LANES), bf16)],
        compiler_params=_params(("arbitrary",)),
        name="mla_prepare",
    )(ub, cs, *consts)


def _mla_attn_body(q_ref, k_ref, v_ref, o_ref, *, n_ctx, n_tok, tk, tq):
    i = pl.program_id(1)
    hs = range(MLA_HEADS_PER_STEP)
    subs = range(tq // ROW_TILE)

    def rows(sb):
        return slice(sb * ROW_TILE, (sb + 1) * ROW_TILE)

    def write(chains, o):
        for (h, sb), val in zip(chains, o):
            o_ref[rows(sb), h * V_DIM:(h + 1) * V_DIM] = val.astype(o_ref.dtype)

    def context_queries():
        chains = [(h, 0) for h in hs]
        each = lambda f: [f(n, h) for n, (h, _) in enumerate(chains)]
        s = each(lambda n, h: _dot_nt(q_ref[h, rows(0), :], k_ref[h, 0:n_ctx, :]))
        p = each(lambda n, h: jnp.exp2(s[n] - jnp.max(s[n], axis=-1, keepdims=True)))
        pv = each(lambda n, h: _dot(p[n].astype(bf16), v_ref[h, 0:n_ctx, :]))
        write(chains, each(lambda n, h: pv[n] / jnp.sum(p[n], axis=-1, keepdims=True)))

    def latent_queries(chains):
        each = lambda f: [f(n, h, sb) for n, (h, sb) in enumerate(chains)]
        q = each(lambda n, h, sb: q_ref[h, rows(sb), :])

        def step(c, carry):
            off = pl.multiple_of(c * tk, tk)
            s = each(lambda n, h, sb: _dot_nt(q[n], k_ref[h, pl.ds(off, tk), :]))
            m_new = each(lambda n, h, sb: jnp.maximum(carry[n][0], jnp.max(s[n], axis=-1, keepdims=True)))
            alpha = each(lambda n, h, sb: jnp.exp2(carry[n][0] - m_new[n]))
            p = each(lambda n, h, sb: jnp.exp2(s[n] - m_new[n]))
            l = each(lambda n, h, sb: alpha[n] * carry[n][1] + jnp.sum(p[n], axis=-1, keepdims=True))
            pv = each(lambda n, h, sb: _dot(p[n].astype(bf16), v_ref[h, pl.ds(off, tk), :]))
            return tuple((m_new[n], l[n], alpha[n] * carry[n][2] + pv[n]) for n in range(len(chains)))

        init = tuple((jnp.full((ROW_TILE, 1), NEG_BIG, f32), jnp.zeros((ROW_TILE, 1), f32),
                      jnp.zeros((ROW_TILE, V_DIM), f32)) for _ in chains)
        fin = lax.fori_loop(0, n_tok // tk, step, init)
        write(chains, [f[2] / f[1] for f in fin])

    @pl.when(i == 0)
    def _():
        context_queries()
        if len(subs) > 1:
            latent_queries([(h, sb) for h in hs for sb in subs[1:]])

    @pl.when(i > 0)
    def _():
        latent_queries([(h, sb) for h in hs for sb in subs])


def mla_attention(q, k, v, n_ctx):
    nh, n, dq = q.shape
    hps = MLA_HEADS_PER_STEP
    assert n_ctx == ROW_TILE
    tk = next(t for t in (768, 512, 256) if n % t == 0)
    tq = tk
    return pl.pallas_call(
        functools.partial(_mla_attn_body, n_ctx=n_ctx, n_tok=n, tk=tk, tq=tq),
        grid=(nh // hps, n // tq),
        in_specs=[
            pl.BlockSpec((hps, tq, dq), lambda h, i: (h, i, 0)),
            pl.BlockSpec((hps, n, dq), lambda h, i: (h, 0, 0)),
            pl.BlockSpec((hps, n, V_DIM), lambda h, i: (h, 0, 0)),
        ],
        out_specs=pl.BlockSpec((tq, hps * V_DIM), lambda h, i: (i, h)),
        out_shape=jax.ShapeDtypeStruct((n, nh * V_DIM), bf16),
        compiler_params=_params(("arbitrary", "arbitrary")),
        name="mla_attention",
    )(q, k, v)


def _na_body(q_ref, k_ref, v_ref, bias_ref, o_ref, *, n_ctx, n_rows):
    j = pl.program_id(1)
    lane = lax.broadcasted_iota(jnp.int32, (1, LANES), 1)
    low = lane < C_HEAD_DIM
    win = WIN_ROWS * GRID_W
    lanes = lambda lp: slice(lp * LANES, (lp + 1) * LANES)
    pairs = range(NA_PAIRS_PER_STEP)
    kc = [k_ref[0:n_ctx, lanes(lp)] for lp in pairs]
    vc = [v_ref[0:n_ctx, lanes(lp)] for lp in pairs]

    def head_mask(hh, x):
        return jnp.where(low if hh == 0 else ~low, x, jnp.zeros_like(x))

    @pl.when(j == 0)
    def _():
        units = [(lp, hh) for lp in pairs for hh in range(2)]
        each = lambda f: [f(n, lp, hh) for n, (lp, hh) in enumerate(units)]
        qm = each(lambda n, lp, hh: head_mask(hh, q_ref[:, lanes(lp)] * NA_SCALE))
        s = each(lambda n, lp, hh: _dot_nt(qm[n], kc[lp]))
        p = each(lambda n, lp, hh: jnp.exp(s[n] - jnp.max(s[n], axis=-1, keepdims=True)))
        o = each(lambda n, lp, hh: _dot(p[n].astype(bf16), vc[lp]) / jnp.sum(p[n], axis=-1, keepdims=True))
        for lp in pairs:
            o_ref[:, lanes(lp)] = jnp.where(low, o[2 * lp], o[2 * lp + 1]).astype(o_ref.dtype)

    @pl.when(j > 0)
    def _():
        grid_rows = range(ROW_TILE // GRID_W)
        units = [(il, lp, hh) for il in grid_rows for lp in pairs for hh in range(2)]
        per_row = lambda f: [f(il) for il in grid_rows]
        per_rp = lambda f: [[f(il, lp) for lp in pairs] for il in grid_rows]
        per_unit = lambda f: [f(n, il, lp, hh) for n, (il, lp, hh) in enumerate(units)]
        i = per_row(lambda il: (j - 1) * (ROW_TILE // GRID_W) + il)
        r0 = per_row(lambda il: jnp.clip(i[il] - WIN_ROWS // 2, 0, n_rows - WIN_ROWS))
        d0 = per_row(lambda il: r0[il] - i[il] + WIN_ROWS - 1)
        off = per_row(lambda il: pl.multiple_of(n_ctx + r0[il] * GRID_W, GRID_W))
        q2 = per_rp(lambda il, lp: q_ref[il * GRID_W:(il + 1) * GRID_W, lanes(lp)] * NA_SCALE)
        kw = per_rp(lambda il, lp: k_ref[pl.ds(off[il], win), lanes(lp)])
        vw = per_rp(lambda il, lp: v_ref[pl.ds(off[il], win), lanes(lp)])
        qm = per_unit(lambda n, il, lp, hh: head_mask(hh, q2[il][lp]))
        s_w = per_unit(lambda n, il, lp, hh: _dot_nt(qm[n], kw[il][lp]) + bias_ref[2 * lp + hh, d0[il]])
        s_c = per_unit(lambda n, il, lp, hh: _dot_nt(qm[n], kc[lp]))
        m = per_unit(lambda n, il, lp, hh: jnp.maximum(jnp.max(s_w[n], axis=-1, keepdims=True),
                                                       jnp.max(s_c[n], axis=-1, keepdims=True)))
        p_w = per_unit(lambda n, il, lp, hh: jnp.exp(s_w[n] - m[n]))
        p_c = per_unit(lambda n, il, lp, hh: jnp.exp(s_c[n] - m[n]))
        l = per_unit(lambda n, il, lp, hh: jnp.sum(p_w[n], axis=-1, keepdims=True)
                     + jnp.sum(p_c[n], axis=-1, keepdims=True))
        o = per_unit(lambda n, il, lp, hh: (_dot(p_w[n].astype(bf16), vw[il][lp])
                                            + _dot(p_c[n].astype(bf16), vc[lp])) / l[n])
        for n in range(0, len(units), 2):
            il, lp, _ = units[n]
            o_ref[il * GRID_W:(il + 1) * GRID_W, lanes(lp)] = jnp.where(low, o[n], o[n + 1]).astype(o_ref.dtype)


def na_attention(uc, bias, n_ctx):
    n = uc.shape[0]
    width = uc.shape[1] // 3
    pps = NA_PAIRS_PER_STEP
    ngroup = width // (pps * LANES)
    n_rows = (n - n_ctx) // GRID_W
    return pl.pallas_call(
        functools.partial(_na_body, n_ctx=n_ctx, n_rows=n_rows),
        grid=(ngroup, n // ROW_TILE),
        in_specs=[
            pl.BlockSpec((ROW_TILE, pps * LANES), lambda p, j: (j, p)),
            pl.BlockSpec((n, pps * LANES), lambda p, j: (0, ngroup + p)),
            pl.BlockSpec((n, pps * LANES), lambda p, j: (0, 2 * ngroup + p)),
            pl.BlockSpec((2 * pps, WIN_ROWS, GRID_W, WIN_ROWS * GRID_W), lambda p, j: (p, 0, 0, 0)),
        ],
        out_specs=pl.BlockSpec((ROW_TILE, pps * LANES), lambda p, j: (j, p)),
        out_shape=jax.ShapeDtypeStruct((n, width), bf16),
        compiler_params=_params(("arbitrary", "arbitrary")),
        name="na_attention",
    )(uc, uc, uc, bias)


def _rope_tables(n_ctx, seq):
    t = jnp.arange(seq)
    row = (t // GRID_W).astype(f32)
    col = (t % GRID_W).astype(f32)
    n_freq = QK_ROPE // 4
    inv = ROPE_BASE ** (-jnp.arange(n_freq, dtype=f32) / n_freq)
    ar, ac = row[:, None] * inv[None, :], col[:, None] * inv[None, :]
    cos = jnp.concatenate([jnp.cos(ar), jnp.cos(ar), jnp.cos(ac), jnp.cos(ac)], axis=1)
    sin = jnp.concatenate([-jnp.sin(ar), jnp.sin(ar), -jnp.sin(ac), jnp.sin(ac)], axis=1)
    lat = jnp.concatenate([cos, sin], axis=1)
    ctx = jnp.concatenate([jnp.ones((n_ctx, QK_ROPE), f32), jnp.zeros((n_ctx, QK_ROPE), f32)], axis=1)
    return jnp.concatenate([ctx, lat], axis=0)


def _pair_swap_perm():
    q = QK_ROPE // 4
    return np.concatenate([np.arange(q, 2 * q), np.arange(0, q), np.arange(3 * q, 4 * q), np.arange(2 * q, 3 * q)])


def _na_bias_tables(rpb):
    cols = np.arange(GRID_W)
    c0 = np.clip(cols - WIN_COLS // 2, 0, GRID_W - WIN_COLS)
    inside = (cols[None, :] >= c0[:, None]) & (cols[None, :] < c0[:, None] + WIN_COLS)
    rel = np.clip(cols[None, :] - cols[:, None] + WIN_COLS - 1, 0, 2 * WIN_COLS - 2)
    full = jnp.where(inside[None, None], rpb[:, :, rel], NEG_BIG)
    tabs = [jnp.concatenate([full[:, d0 + jj] for jj in range(WIN_ROWS)], axis=-1) for d0 in range(WIN_ROWS)]
    return jnp.stack(tabs, axis=1).astype(f32)


def _layer_weights(l, w_in, rwkv_conv, rwkv_w0, rwkv_w_up, rwkv_a0, rwkv_a_up, rwkv_g_up, rwkv_k_k, rwkv_k_a,
                   rwkv_r_k, mla_q_norm_g, mla_w_uq, mla_kv_norm_g, mla_w_ukv, w_branch, w_out, mlp_w1, mlp_w2):
    d = w_in.shape[1]
    aw = rwkv_k_k.shape[1]
    nh_b = mla_w_ukv.shape[2] // (QK_NOPE + V_DIM)
    n_lora = 2 * LORA_W + 2 * LORA_A + LORA_G
    c_lora = 3 * aw
    c_mla = c_lora + n_lora
    c_na = c_mla + Q_LORA + KV_LORA + QK_ROPE
    c_gate = c_na + 3 * aw
    wi = w_in[l]
    perm = _pair_swap_perm()
    kr = wi[:, c_mla + Q_LORA + KV_LORA:c_na]
    w_b = jnp.concatenate([
        wi[:, c_lora:c_mla],
        wi[:, c_mla + Q_LORA:c_mla + Q_LORA + KV_LORA],
        wi[:, c_mla:c_mla + Q_LORA], jnp.zeros((d, 512 - Q_LORA), f32),
        kr, kr[:, perm]], axis=1).astype(bf16)
    cols = dict(c_kvd=n_lora, c_q=n_lora + KV_LORA, c_kr=n_lora + KV_LORA + 512)

    def lora_pad(w, start):
        r = w.shape[1]
        out = jnp.zeros((2, n_lora, aw), f32)
        for z in range(2):
            out = out.at[z, start + z * r:start + (z + 1) * r].set(w[z])
        return out.astype(bf16)

    g_up = jnp.zeros((n_lora, aw), f32).at[2 * LORA_W + 2 * LORA_A:].set(rwkv_g_up[l]).astype(bf16)
    head_id = np.arange(2 * LANES) // A_HEAD_DIM
    seg = jnp.asarray(head_id[:, None] == head_id[None, :], bf16)
    rwkv = dict(conv_rkv=rwkv_conv[l][:, :c_lora], conv_l=rwkv_conv[l][:, c_lora:c_mla],
                w_up=lora_pad(rwkv_w_up[l], 0), a_up=lora_pad(rwkv_a_up[l], 2 * LORA_W), g_up=g_up,
                w0=rwkv_w0[l], a0=rwkv_a0[l], k_k=rwkv_k_k[l][None], k_a=rwkv_k_a[l][None],
                r_k=rwkv_r_k[l].reshape(1, aw), seg=seg)

    uq = mla_w_uq[l].reshape(Q_LORA, nh_b, QK_NOPE + QK_ROPE)
    uq = jnp.concatenate([uq, jnp.zeros((512 - Q_LORA, nh_b, QK_NOPE + QK_ROPE), f32)], axis=0)
    zpad = jnp.zeros((512, nh_b, LANES - QK_ROPE), f32)
    pe = uq[:, :, QK_NOPE:]
    ukv = mla_w_ukv[l].reshape(KV_LORA, nh_b, QK_NOPE + V_DIM)
    mla = dict(
        gq=jnp.concatenate([mla_q_norm_g[l], jnp.zeros((512 - Q_LORA,), f32)])[None],
        gkv=mla_kv_norm_g[l][None],
        wq_nope=uq[:, :, :QK_NOPE].reshape(512, nh_b * QK_NOPE).astype(bf16),
        wq_pe=jnp.concatenate([pe, zpad], axis=2).reshape(512, nh_b * LANES).astype(bf16),
        wq_pr=jnp.concatenate([pe[:, :, perm], zpad], axis=2).reshape(512, nh_b * LANES).astype(bf16),
        wkv=jnp.concatenate([ukv[:, :, :QK_NOPE].reshape(KV_LORA, -1), ukv[:, :, QK_NOPE:].reshape(KV_LORA, -1)],
                            axis=1).astype(bf16))
    return dict(w_rkv=wi[:, :c_lora].astype(bf16), w_b=w_b, w_na=wi[:, c_na:c_gate].astype(bf16),
                w_gate=wi[:, c_gate:].astype(bf16), cols=cols, rwkv=rwkv, mla=mla,
                w_branch=w_branch[l].astype(bf16), w_out=w_out[l].astype(bf16),
                mlp_w1=mlp_w1[l].astype(bf16), mlp_w2=mlp_w2[l].astype(bf16))


def kernel(x, c, ctx, c_ctx, ada_w, ada_b, norm_mix_g, norm_mlp_g, w_in, rwkv_conv, rwkv_w0, rwkv_w_up, rwkv_a0, rwkv_a_up, rwkv_g_up, rwkv_k_k, rwkv_k_a, rwkv_r_k, rwkv_ln_g, rwkv_ln_b, mla_q_norm_g, mla_w_uq, mla_kv_norm_g, mla_w_ukv, na_rpb, w_branch, w_out, mlp_w1, mlp_w2, final_norm_g):
    batch, seq, d = x.shape
    n_ctx = ctx.shape[1]
    depth = ada_w.shape[0]
    assert batch == 1 and n_ctx == ROW_TILE and seq % ROW_TILE == 0 and seq // GRID_W >= WIN_ROWS
    n_ctx_tiles = n_ctx // ROW_TILE

    xs = jnp.concatenate([ctx[0], x[0]], axis=0)
    cond = jnp.concatenate([c, c_ctx[None], jnp.zeros((6, d), f32)], axis=0)
    mods = ada_modulation(cond, ada_w, ada_b)
    cs = _rope_tables(n_ctx, seq)

    for l in range(depth):
        lw = _layer_weights(l, w_in, rwkv_conv, rwkv_w0, rwkv_w_up, rwkv_a0, rwkv_a_up, rwkv_g_up, rwkv_k_k,
                            rwkv_k_a, rwkv_r_k, mla_q_norm_g, mla_w_uq, mla_kv_norm_g, mla_w_ukv, w_branch,
                            w_out, mlp_w1, mlp_w2)
        mods8 = mods[l]
        mods2 = mods8[:2, None, :]
        h = norm_modulate(xs, norm_mix_g[l], mods2, 0)
        u3 = matmul(h, lw["w_rkv"], f32, 512)
        ub = matmul(h, lw["w_b"], f32, lw["w_b"].shape[1])
        uc = matmul(h, lw["w_na"], bf16, 512)
        gates = matmul(h, lw["w_gate"], bf16, 512, sigmoid=True)

        r, v, kk, bonus, g, lwd, kd, bd = rwkv_prepare(u3, ub, lw["rwkv"], n_ctx)
        y = rwkv_chunk_scan(r, v, kk, lwd, kd, bd, n_ctx)
        ya = rwkv_readout(y, bonus, g, rwkv_ln_g[l], rwkv_ln_b[l], lw["rwkv"]["seg"][:LANES, :LANES])

        q_b, k_b, v_b = mla_prepare(ub, cs, lw["mla"], lw["cols"])
        yb = mla_attention(q_b, k_b, v_b, n_ctx)

        yc = na_attention(uc, _na_bias_tables(na_rpb[l]), n_ctx)

        merged = merge_branches(ya, yb, yc, gates, lw["w_branch"])
        xs = matmul_gated_residual(merged, lw["w_out"], xs, mods8, 2, n_ctx)
        h2 = norm_modulate(xs, norm_mlp_g[l], mods2, 3)
        xs = mlp_gated_residual(h2, lw["mlp_w1"], lw["mlp_w2"], xs, mods8, 5, n_ctx)

    return final_norm(xs, final_norm_g, n_ctx_tiles)[None]
```

```python
import functools
import math

import jax
import jax.numpy as jnp
import numpy as np
from jax import lax
from jax.experimental import pallas as pl
from jax.experimental.pallas import tpu as pltpu

f32 = jnp.float32
bf16 = jnp.bfloat16

GRID_W = 64
RMS_EPS = 1e-6
N_BRANCH = 3
A_HEAD_DIM = 64
LORA_W = 96
LORA_A = 96
LORA_G = 256
GN_EPS = 64e-5
QK_NOPE = 128
QK_ROPE = 64
V_DIM = 128
Q_LORA = 448
KV_LORA = 128
MLA_SCALE = (QK_NOPE + QK_ROPE) ** -0.5
MLA_Q_SCALE = MLA_SCALE * math.log2(math.e)
ROPE_BASE = 10000.0
C_HEAD_DIM = 64
WIN_ROWS = 8
WIN_COLS = 16
NA_SCALE = C_HEAD_DIM ** -0.5
DECAY_SCALE = math.exp(-0.5)

LANES = 128
ROW_TILE = 256
CHUNK = 64
CHUNKS_PER_STEP = 2
PREP_TILE = 128
NA_PAIRS_PER_STEP = 2
MLA_HEADS_PER_STEP = 2
VMEM_LIMIT = 48 * 1024 * 1024
NEG_BIG = -1e30


def _params(sem):
    return pltpu.CompilerParams(dimension_semantics=sem, vmem_limit_bytes=VMEM_LIMIT)


def _dot(a, b):
    return jnp.dot(a, b, preferred_element_type=f32)


def _dot_nt(a, b):
    return lax.dot_general(a, b, (((1,), (1,)), ((), ())), preferred_element_type=f32)


def _ada_body(s_ref, w_ref, b_ref, o_ref):
    s = s_ref[...]
    s = s * jax.nn.sigmoid(s)
    o_ref[...] = _dot(s, w_ref[...]) + b_ref[...]


def ada_modulation(cond, ada_w, ada_b):
    nl, d, n6 = ada_w.shape
    tn = 1024
    return pl.pallas_call(
        _ada_body,
        grid=(nl, n6 // tn),
        in_specs=[
            pl.BlockSpec((8, d), lambda l, j: (0, 0)),
            pl.BlockSpec((None, d, tn), lambda l, j: (l, 0, j)),
            pl.BlockSpec((None, 1, tn), lambda l, j: (l, 0, j)),
        ],
        out_specs=pl.BlockSpec((None, 8, tn), lambda l, j: (l, 0, j)),
        out_shape=jax.ShapeDtypeStruct((nl, 8, n6), f32),
        compiler_params=_params(("arbitrary", "arbitrary")),
        name="ada_modulation",
    )(cond, ada_w, ada_b.reshape(nl, 1, n6))


def _norm_mod_body(x_ref, g_ref, m_ref, o_ref, *, off, d):
    x = x_ref[...]
    y = x * lax.rsqrt(jnp.mean(x * x, axis=-1, keepdims=True) + RMS_EPS) * g_ref[...]
    shift = m_ref[:, off * d:(off + 1) * d]
    scale = m_ref[:, (off + 1) * d:(off + 2) * d]
    o_ref[...] = (y * (1.0 + scale) + shift).astype(o_ref.dtype)


def norm_modulate(x, g, mods2, off):
    n, d = x.shape
    return pl.pallas_call(
        functools.partial(_norm_mod_body, off=off, d=d),
        grid=(n // ROW_TILE,),
        in_specs=[
            pl.BlockSpec((ROW_TILE, d), lambda i: (i, 0)),
            pl.BlockSpec((1, d), lambda i: (0, 0)),
            pl.BlockSpec((None, 1, mods2.shape[-1]), lambda i: (jnp.where(i == 0, 1, 0), 0, 0)),
        ],
        out_specs=pl.BlockSpec((ROW_TILE, d), lambda i: (i, 0)),
        out_shape=jax.ShapeDtypeStruct((n, d), bf16),
        compiler_params=_params(("arbitrary",)),
        name="norm_modulate",
    )(x, g.reshape(1, d), mods2)


def _final_norm_body(x_ref, g_ref, o_ref):
    x = x_ref[...]
    o_ref[...] = x * lax.rsqrt(jnp.mean(x * x, axis=-1, keepdims=True) + RMS_EPS) * g_ref[...]


def final_norm(x, g, n_ctx_tiles):
    n, d = x.shape
    t = n - n_ctx_tiles * ROW_TILE
    return pl.pallas_call(
        _final_norm_body,
        grid=(t // ROW_TILE,),
        in_specs=[
            pl.BlockSpec((ROW_TILE, d), lambda i: (i + n_ctx_tiles, 0)),
            pl.BlockSpec((1, d), lambda i: (0, 0)),
        ],
        out_specs=pl.BlockSpec((ROW_TILE, d), lambda i: (i, 0)),
        out_shape=jax.ShapeDtypeStruct((t, d), f32),
        compiler_params=_params(("arbitrary",)),
        name="final_norm",
    )(x, g.reshape(1, d))


def _mm_body(a_ref, w_ref, o_ref, *, sigmoid):
    acc = _dot(a_ref[...], w_ref[...])
    o_ref[...] = (jax.nn.sigmoid(acc) if sigmoid else acc).astype(o_ref.dtype)


def _row_tile(n):
    for tm in (1056, 768, 512, 384, 256):
        if n % tm == 0:
            return tm
    raise ValueError(f"no row tile for {n} rows")


def matmul(a, w, out_dtype, tn, sigmoid=False):
    m, k = a.shape
    n = w.shape[1]
    tm = _row_tile(m)
    return pl.pallas_call(
        functools.partial(_mm_body, sigmoid=sigmoid),
        grid=(m // tm, n // tn),
        in_specs=[pl.BlockSpec((tm, k), lambda i, j: (i, 0)), pl.BlockSpec((k, tn), lambda i, j: (0, j))],
        out_specs=pl.BlockSpec((tm, tn), lambda i, j: (i, j)),
        out_shape=jax.ShapeDtypeStruct((m, n), out_dtype),
        compiler_params=_params(("arbitrary", "arbitrary")),
        name="matmul",
    )(a, w)


def _gate_rows(m_ref, tm, n_ctx):
    rows = pl.program_id(0) * tm + lax.broadcasted_iota(jnp.int32, (tm, 1), 0)
    return jnp.where(rows < n_ctx, m_ref[1:2, :], m_ref[0:1, :])


def _mm_res_body(a_ref, w_ref, x_ref, m_ref, o_ref, *, tm, n_ctx):
    acc = _dot(a_ref[...], w_ref[...])
    o_ref[...] = x_ref[...] + _gate_rows(m_ref, tm, n_ctx) * acc


def matmul_gated_residual(a, w, x, mods8, gate_off, n_ctx, tn=512):
    m, k = a.shape
    n = w.shape[1]
    tm = _row_tile(m)
    nj = n // tn
    return pl.pallas_call(
        functools.partial(_mm_res_body, tm=tm, n_ctx=n_ctx),
        grid=(m // tm, nj),
        in_specs=[
            pl.BlockSpec((tm, k), lambda i, j: (i, 0)),
            pl.BlockSpec((k, tn), lambda i, j: (0, j)),
            pl.BlockSpec((tm, tn), lambda i, j: (i, j)),
            pl.BlockSpec((8, tn), lambda i, j: (0, gate_off * nj + j)),
        ],
        out_specs=pl.BlockSpec((tm, tn), lambda i, j: (i, j)),
        out_shape=jax.ShapeDtypeStruct((m, n), f32),
        compiler_params=_params(("arbitrary", "arbitrary")),
        name="matmul_gated_residual",
    )(a, w, x, mods8)


def _merge_body(ya_ref, yb_ref, yc_ref, g0_ref, g1_ref, g2_ref, w_ref, o_ref):
    acc = g0_ref[...] * _dot(ya_ref[...], w_ref[0])
    acc = acc + g1_ref[...] * _dot(yb_ref[...], w_ref[1])
    acc = acc + g2_ref[...] * _dot(yc_ref[...], w_ref[2])
    o_ref[...] = acc.astype(o_ref.dtype)


def merge_branches(ya, yb, yc, gates, w_branch, tn=1024):
    m, k = ya.shape
    d = w_branch.shape[-1]
    tm = 528 if m % 528 == 0 else ROW_TILE
    nj = d // tn
    y_spec = pl.BlockSpec((tm, k), lambda i, j: (i, 0))
    return pl.pallas_call(
        _merge_body,
        grid=(m // tm, nj),
        in_specs=[
            y_spec, y_spec, y_spec,
            pl.BlockSpec((tm, tn), lambda i, j: (i, j)),
            pl.BlockSpec((tm, tn), lambda i, j: (i, nj + j)),
            pl.BlockSpec((tm, tn), lambda i, j: (i, 2 * nj + j)),
            pl.BlockSpec((N_BRANCH, k, tn), lambda i, j: (0, 0, j)),
        ],
        out_specs=pl.BlockSpec((tm, tn), lambda i, j: (i, j)),
        out_shape=jax.ShapeDtypeStruct((m, d), bf16),
        compiler_params=_params(("arbitrary", "arbitrary")),
        name="merge_branches",
    )(ya, yb, yc, gates, gates, gates, w_branch)


def _mlp_body(h_ref, w1_ref, w2_ref, x_ref, m_ref, o_ref, acc_ref, *, tm, n_ctx):
    f = pl.program_id(1)

    @pl.when(f == 0)
    def _():
        acc_ref[...] = jnp.zeros_like(acc_ref)

    h1 = jnp.maximum(_dot(h_ref[...], w1_ref[...]), 0.0)
    acc_ref[...] += _dot((h1 * h1).astype(bf16), w2_ref[...])

    @pl.when(f == pl.num_programs(1) - 1)
    def _():
        o_ref[...] = x_ref[...] + _gate_rows(m_ref, tm, n_ctx) * acc_ref[...]


def mlp_gated_residual(h, w1, w2, x, mods8, gate_off, n_ctx, tf=1024):
    m, d = h.shape
    dff = w1.shape[1]
    tm = 528 if m % 528 == 0 else ROW_TILE
    return pl.pallas_call(
        functools.partial(_mlp_body, tm=tm, n_ctx=n_ctx),
        grid=(m // tm, dff // tf),
        in_specs=[
            pl.BlockSpec((tm, d), lambda i, f: (i, 0)),
            pl.BlockSpec((d, tf), lambda i, f: (0, f)),
            pl.BlockSpec((tf, d), lambda i, f: (f, 0)),
            pl.BlockSpec((tm, d), lambda i, f: (i, 0)),
            pl.BlockSpec((8, d), lambda i, f: (0, gate_off)),
        ],
        out_specs=pl.BlockSpec((tm, d), lambda i, f: (i, 0)),
        out_shape=jax.ShapeDtypeStruct((m, d), f32),
        scratch_shapes=[pltpu.VMEM((tm, d), f32)],
        compiler_params=_params(("arbitrary", "arbitrary")),
        name="mlp_gated_residual",
    )(h, w1, w2, x, mods8)


def _seg_sum(x, e_ref):
    hi = x.astype(bf16)
    lo = (x - hi.astype(f32)).astype(bf16)
    e = e_ref[...]
    w = e.shape[0]
    parts = [_dot(hi[:, g:g + w], e) + _dot(lo[:, g:g + w], e) for g in range(0, x.shape[1], w)]
    return parts[0] if len(parts) == 1 else jnp.concatenate(parts, axis=1)


def _to_pairs(o_ref, val, lead=None):
    for p in range(val.shape[-1] // LANES):
        piece = val[:, p * LANES:(p + 1) * LANES]
        if lead is None:
            o_ref[p] = piece
        else:
            o_ref[lead, p] = piece


def _rwkv_prep_body(u_ref, up_ref, un_ref, l_ref, lp_ref, ln_ref, cw_ref, cwl_ref, wu_ref, au_ref, gu_ref,
                    w0_ref, a0_ref, kk_ref, ka_ref, rk_ref, e_ref,
                    r_o, v_o, kkn_o, bonus_o, g_o, lw_o, kd_o, bd_o, *, n_ctx_tiles, aw):
    i = pl.program_id(0)
    last = pl.num_programs(0) - 1
    left_zero = (i == 0) | (i == n_ctx_tiles)
    right_zero = (i == n_ctx_tiles - 1) | (i == last)
    rows = lax.broadcasted_iota(jnp.int32, (PREP_TILE, 1), 0)

    def conv(x_ref, xp_ref, xn_ref, w_ref):
        x = x_ref[...]
        prev_row = jnp.where(left_zero, 0.0, xp_ref[7:8, :])
        next_row = jnp.where(right_zero, 0.0, xn_ref[0:1, :])
        x_prev = jnp.where(rows == 0, prev_row, pltpu.roll(x, 1, 0))
        x_next = jnp.where(rows == PREP_TILE - 1, next_row, pltpu.roll(x, PREP_TILE - 1, 0))
        return w_ref[0:1, :] * x_prev + w_ref[1:2, :] * x + w_ref[2:3, :] * x_next

    y = conv(u_ref, up_ref, un_ref, cw_ref)
    r, k, v = y[:, :aw], y[:, aw:2 * aw], y[:, 2 * aw:]
    yl = conv(l_ref, lp_ref, ln_ref, cwl_ref)
    th = jnp.tanh(yl).astype(bf16)
    sg = jax.nn.sigmoid(yl).astype(bf16)
    ylb = yl.astype(bf16)

    _to_pairs(g_o, _dot(sg, gu_ref[...]))
    _to_pairs(r_o, r)
    _to_pairs(v_o, v.astype(bf16))

    kkr = k * kk_ref[...]
    norm = jnp.sqrt(_seg_sum(kkr * kkr, e_ref))
    kkn = kkr / jnp.maximum(norm, 1e-12)
    _to_pairs(kkn_o, kkn)

    ksum = None
    for dr in range(2):
        z = w0_ref[dr:dr + 1, :] + _dot(th, wu_ref[dr])
        _to_pairs(lw_o, -DECAY_SCALE * jax.nn.sigmoid(z), lead=dr)
        a = jax.nn.sigmoid(a0_ref[dr:dr + 1, :] + _dot(ylb, au_ref[dr]))
        kd = k * (1.0 + (a - 1.0) * ka_ref[...])
        _to_pairs(kd_o, kd, lead=dr)
        _to_pairs(bd_o, kkn * a, lead=dr)
        ksum = kd if ksum is None else ksum + kd
    rk = _seg_sum(r * (0.5 * ksum) * rk_ref[...], e_ref)
    _to_pairs(bonus_o, rk * v)


def rwkv_prepare(u3, ub, pw, n_ctx):
    n = u3.shape[0]
    aw = u3.shape[1] // 3
    npair = aw // LANES
    lw = pw["conv_l"].shape[1]
    tpb = PREP_TILE // 8
    nb8 = n // 8

    def prev_map(i):
        return (jnp.maximum(i * tpb - 1, 0), 0)

    def next_map(i):
        return (jnp.minimum((i + 1) * tpb, nb8 - 1), 0)

    full = lambda a: pl.BlockSpec(a.shape, lambda i: (0,) * a.ndim)
    pm = jax.ShapeDtypeStruct((npair, n, LANES), f32)
    pm2 = jax.ShapeDtypeStruct((2, npair, n, LANES), f32)
    pm_bf16 = jax.ShapeDtypeStruct((npair, n, LANES), bf16)
    pm_spec = pl.BlockSpec((npair, PREP_TILE, LANES), lambda i: (0, i, 0))
    pm2_spec = pl.BlockSpec((2, npair, PREP_TILE, LANES), lambda i: (0, 0, i, 0))
    consts = [pw["conv_rkv"], pw["conv_l"], pw["w_up"], pw["a_up"], pw["g_up"], pw["w0"], pw["a0"],
              pw["k_k"], pw["k_a"], pw["r_k"], pw["seg"]]
    return pl.pallas_call(
        functools.partial(_rwkv_prep_body, n_ctx_tiles=n_ctx // PREP_TILE, aw=aw),
        grid=(n // PREP_TILE,),
        in_specs=[
            pl.BlockSpec((PREP_TILE, 3 * aw), lambda i: (i, 0)),
            pl.BlockSpec((8, 3 * aw), prev_map),
            pl.BlockSpec((8, 3 * aw), next_map),
            pl.BlockSpec((PREP_TILE, lw), lambda i: (i, 0)),
            pl.BlockSpec((8, lw), prev_map),
            pl.BlockSpec((8, lw), next_map),
        ] + [full(a) for a in consts],
        out_specs=[pm_spec] * 5 + [pm2_spec] * 3,
        out_shape=[pm, pm_bf16] + [pm] * 3 + [pm2] * 3,
        compiler_params=_params(("arbitrary",)),
        name="rwkv_prepare",
    )(u3, u3, u3, ub, ub, ub, *consts)


def _dot_tn(a, b):
    return lax.dot_general(a, b, (((0,), (0,)), ((), ())), preferred_element_type=f32)


def _rwkv_chunk_body(r_ref, v_ref, kk_ref, lw_ref, k_ref, b_ref, y_ref, h_ref, *, n_pairs):
    d = pl.program_id(0)
    j = pl.program_id(1)
    c = CHUNK
    c2 = 2 * c

    @pl.when(j == 0)
    def _():
        h_ref[...] = jnp.zeros_like(h_ref)

    sign = jnp.where(d == 0, 1, -1)
    ti = lax.broadcasted_iota(jnp.int32, (c, c), 0)
    si = lax.broadcasted_iota(jnp.int32, (c, c), 1)
    incl_c = jnp.where((ti - si) * sign >= 0, 1.0, 0.0).astype(bf16)
    t2 = lax.broadcasted_iota(jnp.int32, (c2, c2), 0)
    s2 = lax.broadcasted_iota(jnp.int32, (c2, c2), 1)
    ahead = (t2 - s2) * sign
    blk = lambda n: (t2 // n) == (s2 // n)
    incl = blk(c) & (ahead >= 0)
    strict = blk(c) & (ahead > 0)
    eye = jnp.where(t2 == s2, 1.0, 0.0)
    inv_levels = []
    n = 2
    while n < c:
        inv_levels.append(strict & blk(2 * n) & ~blk(n))
        n *= 2
    low = lax.broadcasted_iota(jnp.int32, (1, LANES), 1) < A_HEAD_DIM

    def stack(x):
        return jnp.concatenate([jnp.where(low, x, 0.0), jnp.where(low, 0.0, x)], axis=0).astype(bf16)

    pairs = range(n_pairs)
    first = jnp.where(d == 0, 0, CHUNKS_PER_STEP - 1)
    offs = [pl.multiple_of((first + sign * sc) * c, c) for sc in range(CHUNKS_PER_STEP)]
    units = [(sc, p) for sc in range(CHUNKS_PER_STEP) for p in pairs]
    each = lambda f: [f(sc, p) for sc, p in units]
    at = lambda sc, p: sc * n_pairs + p
    ld = lambda ref, sc, p: ref[p, pl.ds(offs[sc], c), :]
    lw = each(lambda sc, p: ld(lw_ref, sc, p))
    lw_hi = [x.astype(bf16) for x in lw]
    lw_lo = [(x - hi.astype(f32)).astype(bf16) for x, hi in zip(lw, lw_hi)]
    cum = [_dot(incl_c, hi) + _dot(incl_c, lo) for hi, lo in zip(lw_hi, lw_lo)]
    tot = [jnp.sum(x, axis=0, keepdims=True) for x in lw]
    e_neg = [jnp.exp(-x) for x in cum]
    e_end = [jnp.exp(t - x) for t, x in zip(tot, cum)]
    r_s = each(lambda sc, p: stack(ld(r_ref, sc, p) * jnp.exp(cum[at(sc, p)])))
    kk_s = each(lambda sc, p: stack(ld(kk_ref, sc, p) * jnp.exp(cum[at(sc, p)] - lw[at(sc, p)])))
    kb_s = each(lambda sc, p: jnp.concatenate([stack(ld(b_ref, sc, p) * e_neg[at(sc, p)]),
                                               stack(ld(k_ref, sc, p) * e_neg[at(sc, p)])], axis=0))
    kbd_s = each(lambda sc, p: jnp.concatenate([stack(ld(k_ref, sc, p) * e_end[at(sc, p)]),
                                                stack(ld(b_ref, sc, p) * e_end[at(sc, p)])], axis=0))
    v_s = each(lambda sc, p: stack(ld(v_ref, sc, p)))
    a1 = [_dot_nt(a, b) for a, b in zip(kk_s, kb_s)]
    a2 = [_dot_nt(a, b) for a, b in zip(r_s, kb_s)]
    a_kb = [jnp.where(strict, a[:, :c2], 0.0) for a in a1]
    a_kk = [jnp.where(strict, a[:, c2:], 0.0).astype(bf16) for a in a1]
    a_rb = [jnp.where(incl, a[:, :c2], 0.0).astype(bf16) for a in a2]
    a_rk = [jnp.where(incl, a[:, c2:], 0.0).astype(bf16) for a in a2]
    t_inv = [eye - jnp.where(blk(2), a, 0.0) for a in a_kb]
    for m in inv_levels:
        t_b = [t.astype(bf16) for t in t_inv]
        x = [_dot(t, jnp.where(m, a, 0.0).astype(bf16)).astype(bf16) for t, a in zip(t_b, a_kb)]
        t_inv = [t - _dot(xx, tb) for t, xx, tb in zip(t_inv, x, t_b)]
    g = [_dot(t.astype(bf16), jnp.concatenate([kk, akk], axis=1)).astype(bf16)
         for t, kk, akk in zip(t_inv, kk_s, a_kk)]
    w_col = [jnp.sum(eye * jnp.exp(t), axis=1, keepdims=True) for t in tot]
    h = [h_ref[p] for p in pairs]
    for sc in range(CHUNKS_PER_STEP):
        ix = [at(sc, p) for p in pairs]
        h_b = [x.astype(bf16) for x in h]
        u = [_dot(g[i], jnp.concatenate([h_b[p], v_s[i]], axis=0)).astype(bf16) for p, i in enumerate(ix)]
        y = [_dot(jnp.concatenate([r_s[i], a_rk[i], -a_rb[i]], axis=1), jnp.concatenate([h_b[p], v_s[i], u[p]], axis=0))
             for p, i in enumerate(ix)]
        h1 = [_dot_tn(kbd_s[i], jnp.concatenate([v_s[i], -u[p]], axis=0)) for p, i in enumerate(ix)]
        for p in pairs:
            y_ref[p, pl.ds(offs[sc], c), :] = y[p][:c] + y[p][c:]
        h = [h[p] * w_col[i] + h1[p] for p, i in enumerate(ix)]
    for p in pairs:
        h_ref[p] = h[p]


def rwkv_chunk_scan(r, v, kk, lw, kd, bd, n_ctx):
    npair, n, _ = r.shape
    blk_rows = CHUNKS_PER_STEP * CHUNK
    nc = n // blk_rows
    ncc = n_ctx // blk_rows

    def cidx(d, j):
        bwd = jnp.where(j < ncc, ncc - 1 - j, nc - 1 - (j - ncc))
        return jnp.where(d == 0, j, bwd)

    shared = pl.BlockSpec((npair, blk_rows, LANES), lambda d, j: (0, cidx(d, j), 0))
    per_dir = pl.BlockSpec((None, npair, blk_rows, LANES), lambda d, j: (d, 0, cidx(d, j), 0))
    return pl.pallas_call(
        functools.partial(_rwkv_chunk_body, n_pairs=npair),
        grid=(2, nc),
        in_specs=[shared, shared, shared, per_dir, per_dir, per_dir],
        out_specs=per_dir,
        out_shape=jax.ShapeDtypeStruct((2, npair, n, LANES), f32),
        scratch_shapes=[pltpu.VMEM((npair, LANES, LANES), f32)],
        compiler_params=_params(("arbitrary", "arbitrary")),
        name="rwkv_chunk_scan",
    )(r, v, kk, lw, kd, bd)


def _rwkv_readout_body(y_ref, bonus_ref, g_ref, lng_ref, lnb_ref, e_ref, o_ref, *, n_pairs):
    inv = 1.0 / A_HEAD_DIM
    for p in range(n_pairs):
        y = y_ref[0, p] + y_ref[1, p]
        yc = y - _seg_sum(y, e_ref) * inv
        var = _seg_sum(yc * yc, e_ref) * inv
        yn = yc * lax.rsqrt(var + GN_EPS) * lng_ref[p] + lnb_ref[p]
        o_ref[:, p * LANES:(p + 1) * LANES] = ((yn + bonus_ref[p]) * g_ref[p]).astype(o_ref.dtype)


def rwkv_readout(y, bonus, g, ln_g, ln_b, seg_pair):
    _, npair, n, _ = y.shape
    pm_spec = pl.BlockSpec((npair, ROW_TILE, LANES), lambda i: (0, i, 0))
    vec_spec = pl.BlockSpec((npair, 1, LANES), lambda i: (0, 0, 0))
    return pl.pallas_call(
        functools.partial(_rwkv_readout_body, n_pairs=npair),
        grid=(n // ROW_TILE,),
        in_specs=[pl.BlockSpec((2, npair, ROW_TILE, LANES), lambda i: (0, 0, i, 0)), pm_spec, pm_spec,
                  vec_spec, vec_spec, pl.BlockSpec((LANES, LANES), lambda i: (0, 0))],
        out_specs=pl.BlockSpec((ROW_TILE, npair * LANES), lambda i: (i, 0)),
        out_shape=jax.ShapeDtypeStruct((n, npair * LANES), bf16),
        compiler_params=_params(("arbitrary",)),
        name="rwkv_readout",
    )(y, bonus, g, ln_g.reshape(npair, 1, LANES), ln_b.reshape(npair, 1, LANES), seg_pair)


def _mla_prep_body(u_ref, cs_ref, gq_ref, gkv_ref, wqn_ref, wqp_ref, wqr_ref, wkv_ref, q_o, k_o, v_o,
                   *, n_heads, c_kvd, c_q, c_kr):
    ql = u_ref[:, c_q:c_q + 512]
    qn = ql * lax.rsqrt(jnp.sum(ql * ql, axis=-1, keepdims=True) * (1.0 / Q_LORA) + RMS_EPS) * gq_ref[...]
    qn = qn.astype(bf16)
    kvd = u_ref[:, c_kvd:c_kvd + KV_LORA]
    kvn = kvd * lax.rsqrt(jnp.mean(kvd * kvd, axis=-1, keepdims=True) + RMS_EPS) * gkv_ref[...]
    kv = _dot(kvn.astype(bf16), wkv_ref[...])
    cs = cs_ref[...]
    lane = lax.broadcasted_iota(jnp.int32, (1, LANES), 1)
    low = lane < QK_ROPE
    cos_t = jnp.where(low, cs, 0.0)
    sin_t = jnp.where(low, pltpu.roll(cs, QK_ROPE, 1), 0.0)
    kr = u_ref[:, c_kr:c_kr + LANES]
    prod = kr * cs
    k_pe = jnp.where(low, prod + pltpu.roll(prod, QK_ROPE, 1), 0.0).astype(bf16)
    q_nope = _dot(qn, wqn_ref[...])
    q_pe = _dot(qn, wqp_ref[...])
    q_pr = _dot(qn, wqr_ref[...])
    for h in range(n_heads):
        sl = slice(h * LANES, (h + 1) * LANES)
        q_o[h, :, 0:LANES] = (q_nope[:, sl] * MLA_Q_SCALE).astype(bf16)
        q_o[h, :, LANES:2 * LANES] = ((q_pe[:, sl] * cos_t + q_pr[:, sl] * sin_t) * MLA_Q_SCALE).astype(bf16)
        k_o[h, :, 0:LANES] = kv[:, sl].astype(bf16)
        k_o[h, :, LANES:2 * LANES] = k_pe
        v_o[h] = kv[:, n_heads * LANES + h * LANES:n_heads * LANES + (h + 1) * LANES].astype(bf16)


def mla_prepare(ub, cs, pw, cols):
    n = ub.shape[0]
    nh = pw["wq_nope"].shape[1] // LANES
    full = lambda a: pl.BlockSpec(a.shape, lambda i: (0,) * a.ndim)
    consts = [pw["gq"], pw["gkv"], pw["wq_nope"], pw["wq_pe"], pw["wq_pr"], pw["wkv"]]
    return pl.pallas_call(
        functools.partial(_mla_prep_body, n_heads=nh, **cols),
        grid=(n // ROW_TILE,),
        in_specs=[pl.BlockSpec((ROW_TILE, ub.shape[1]), lambda i: (i, 0)),
                  pl.BlockSpec((ROW_TILE, LANES), lambda i: (i, 0))] + [full(a) for a in consts],
        out_specs=[pl.BlockSpec((nh, ROW_TILE, 2 * LANES), lambda i: (0, i, 0)),
                   pl.BlockSpec((nh, ROW_TILE, 2 * LANES), lambda i: (0, i, 0)),
                   pl.BlockSpec((nh, ROW_TILE, LANES), lambda i: (0, i, 0))],
        out_shape=[jax.ShapeDtypeStruct((nh, n, 2 * LANES), bf16),
                   jax.ShapeDtypeStruct((nh, n, 2 * LANES), bf16),
                   jax.ShapeDtypeStruct((nh, n, LANES), bf16)],
        compiler_params=_params(("arbitrary",)),
        name="mla_prepare",
    )(ub, cs, *consts)


def _mla_attn_body(q_ref, k_ref, v_ref, o_ref, *, n_ctx, n_tok, tk, tq):
    i = pl.program_id(1)
    hs = range(MLA_HEADS_PER_STEP)
    subs = range(tq // ROW_TILE)

    def rows(sb):
        return slice(sb * ROW_TILE, (sb + 1) * ROW_TILE)

    def write(chains, o):
        for (h, sb), val in zip(chains, o):
            o_ref[rows(sb), h * V_DIM:(h + 1) * V_DIM] = val.astype(o_ref.dtype)

    def context_queries():
        chains = [(h, 0) for h in hs]
        each = lambda f: [f(n, h) for n, (h, _) in enumerate(chains)]
        s = each(lambda n, h: _dot_nt(q_ref[h, rows(0), :], k_ref[h, 0:n_ctx, :]))
        p = each(lambda n, h: jnp.exp2(s[n] - jnp.max(s[n], axis=-1, keepdims=True)))
        pv = each(lambda n, h: _dot(p[n].astype(bf16), v_ref[h, 0:n_ctx, :]))
        write(chains, each(lambda n, h: pv[n] / jnp.sum(p[n], axis=-1, keepdims=True)))

    def latent_queries(chains):
        each = lambda f: [f(n, h, sb) for n, (h, sb) in enumerate(chains)]
        q = each(lambda n, h, sb: q_ref[h, rows(sb), :])

        def step(c, carry):
            off = pl.multiple_of(c * tk, tk)
            s = each(lambda n, h, sb: _dot_nt(q[n], k_ref[h, pl.ds(off, tk), :]))
            m_new = each(lambda n, h, sb: jnp.maximum(carry[n][0], jnp.max(s[n], axis=-1, keepdims=True)))
            alpha = each(lambda n, h, sb: jnp.exp2(carry[n][0] - m_new[n]))
            p = each(lambda n, h, sb: jnp.exp2(s[n] - m_new[n]))
            l = each(lambda n, h, sb: alpha[n] * carry[n][1] + jnp.sum(p[n], axis=-1, keepdims=True))
            pv = each(lambda n, h, sb: _dot(p[n].astype(bf16), v_ref[h, pl.ds(off, tk), :]))
            return tuple((m_new[n], l[n], alpha[n] * carry[n][2] + pv[n]) for n in range(len(chains)))

        init = tuple((jnp.full((ROW_TILE, 1), NEG_BIG, f32), jnp.zeros((ROW_TILE, 1), f32),
                      jnp.zeros((ROW_TILE, V_DIM), f32)) for _ in chains)
        fin = lax.fori_loop(0, n_tok // tk, step, init)
        write(chains, [f[2] / f[1] for f in fin])

    @pl.when(i == 0)
    def _():
        context_queries()
        if len(subs) > 1:
            latent_queries([(h, sb) for h in hs for sb in subs[1:]])

    @pl.when(i > 0)
    def _():
        latent_queries([(h, sb) for h in hs for sb in subs])


def mla_attention(q, k, v, n_ctx):
    nh, n, dq = q.shape
    hps = MLA_HEADS_PER_STEP
    assert n_ctx == ROW_TILE
    tk = next(t for t in (768, 512, 256) if n % t == 0)
    tq = tk
    return pl.pallas_call(
        functools.partial(_mla_attn_body, n_ctx=n_ctx, n_tok=n, tk=tk, tq=tq),
        grid=(nh // hps, n // tq),
        in_specs=[
            pl.BlockSpec((hps, tq, dq), lambda h, i: (h, i, 0)),
            pl.BlockSpec((hps, n, dq), lambda h, i: (h, 0, 0)),
            pl.BlockSpec((hps, n, V_DIM), lambda h, i: (h, 0, 0)),
        ],
        out_specs=pl.BlockSpec((tq, hps * V_DIM), lambda h, i: (i, h)),
        out_shape=jax.ShapeDtypeStruct((n, nh * V_DIM), bf16),
        compiler_params=_params(("arbitrary", "arbitrary")),
        name="mla_attention",
    )(q, k, v)


def _na_body(q_ref, k_ref, v_ref, bias_ref, o_ref, *, n_ctx, n_rows):
    j = pl.program_id(1)
    lane = lax.broadcasted_iota(jnp.int32, (1, LANES), 1)
    low = lane < C_HEAD_DIM
    win = WIN_ROWS * GRID_W
    lanes = lambda lp: slice(lp * LANES, (lp + 1) * LANES)
    pairs = range(NA_PAIRS_PER_STEP)
    kc = [k_ref[0:n_ctx, lanes(lp)] for lp in pairs]
    vc = [v_ref[0:n_ctx, lanes(lp)] for lp in pairs]

    def head_mask(hh, x):
        return jnp.where(low if hh == 0 else ~low, x, jnp.zeros_like(x))

    @pl.when(j == 0)
    def _():
        units = [(lp, hh) for lp in pairs for hh in range(2)]
        each = lambda f: [f(n, lp, hh) for n, (lp, hh) in enumerate(units)]
        qm = each(lambda n, lp, hh: head_mask(hh, q_ref[:, lanes(lp)] * NA_SCALE))
        s = each(lambda n, lp, hh: _dot_nt(qm[n], kc[lp]))
        p = each(lambda n, lp, hh: jnp.exp(s[n] - jnp.max(s[n], axis=-1, keepdims=True)))
        o = each(lambda n, lp, hh: _dot(p[n].astype(bf16), vc[lp]) / jnp.sum(p[n], axis=-1, keepdims=True))
        for lp in pairs:
            o_ref[:, lanes(lp)] = jnp.where(low, o[2 * lp], o[2 * lp + 1]).astype(o_ref.dtype)

    @pl.when(j > 0)
    def _():
        grid_rows = range(ROW_TILE // GRID_W)
        units = [(il, lp, hh) for il in grid_rows for lp in pairs for hh in range(2)]
        per_row = lambda f: [f(il) for il in grid_rows]
        per_rp = lambda f: [[f(il, lp) for lp in pairs] for il in grid_rows]
        per_unit = lambda f: [f(n, il, lp, hh) for n, (il, lp, hh) in enumerate(units)]
        i = per_row(lambda il: (j - 1) * (ROW_TILE // GRID_W) + il)
        r0 = per_row(lambda il: jnp.clip(i[il] - WIN_ROWS // 2, 0, n_rows - WIN_ROWS))
        d0 = per_row(lambda il: r0[il] - i[il] + WIN_ROWS - 1)
        off = per_row(lambda il: pl.multiple_of(n_ctx + r0[il] * GRID_W, GRID_W))
        q2 = per_rp(lambda il, lp: q_ref[il * GRID_W:(il + 1) * GRID_W, lanes(lp)] * NA_SCALE)
        kw = per_rp(lambda il, lp: k_ref[pl.ds(off[il], win), lanes(lp)])
        vw = per_rp(lambda il, lp: v_ref[pl.ds(off[il], win), lanes(lp)])
        qm = per_unit(lambda n, il, lp, hh: head_mask(hh, q2[il][lp]))
        s_w = per_unit(lambda n, il, lp, hh: _dot_nt(qm[n], kw[il][lp]) + bias_ref[2 * lp + hh, d0[il]])
        s_c = per_unit(lambda n, il, lp, hh: _dot_nt(qm[n], kc[lp]))
        m = per_unit(lambda n, il, lp, hh: jnp.maximum(jnp.max(s_w[n], axis=-1, keepdims=True),
                                                       jnp.max(s_c[n], axis=-1, keepdims=True)))
        p_w = per_unit(lambda n, il, lp, hh: jnp.exp(s_w[n] - m[n]))
        p_c = per_unit(lambda n, il, lp, hh: jnp.exp(s_c[n] - m[n]))
        l = per_unit(lambda n, il, lp, hh: jnp.sum(p_w[n], axis=-1, keepdims=True)
                     + jnp.sum(p_c[n], axis=-1, keepdims=True))
        o = per_unit(lambda n, il, lp, hh: (_dot(p_w[n].astype(bf16), vw[il][lp])
                                            + _dot(p_c[n].astype(bf16), vc[lp])) / l[n])
        for n in range(0, len(units), 2):
            il, lp, _ = units[n]
            o_ref[il * GRID_W:(il + 1) * GRID_W, lanes(lp)] = jnp.where(low, o[n], o[n + 1]).astype(o_ref.dtype)


def na_attention(uc, bias, n_ctx):
    n = uc.shape[0]
    width = uc.shape[1] // 3
    pps = NA_PAIRS_PER_STEP
    ngroup = width // (pps * LANES)
    n_rows = (n - n_ctx) // GRID_W
    return pl.pallas_call(
        functools.partial(_na_body, n_ctx=n_ctx, n_rows=n_rows),
        grid=(ngroup, n // ROW_TILE),
        in_specs=[
            pl.BlockSpec((ROW_TILE, pps * LANES), lambda p, j: (j, p)),
            pl.BlockSpec((n, pps * LANES), lambda p, j: (0, ngroup + p)),
            pl.BlockSpec((n, pps * LANES), lambda p, j: (0, 2 * ngroup + p)),
            pl.BlockSpec((2 * pps, WIN_ROWS, GRID_W, WIN_ROWS * GRID_W), lambda p, j: (p, 0, 0, 0)),
        ],
        out_specs=pl.BlockSpec((ROW_TILE, pps * LANES), lambda p, j: (j, p)),
        out_shape=jax.ShapeDtypeStruct((n, width), bf16),
        compiler_params=_params(("arbitrary", "arbitrary")),
        name="na_attention",
    )(uc, uc, uc, bias)


def _rope_tables(n_ctx, seq):
    t = jnp.arange(seq)
    row = (t // GRID_W).astype(f32)
    col = (t % GRID_W).astype(f32)
    n_freq = QK_ROPE // 4
    inv = ROPE_BASE ** (-jnp.arange(n_freq, dtype=f32) / n_freq)
    ar, ac = row[:, None] * inv[None, :], col[:, None] * inv[None, :]
    cos = jnp.concatenate([jnp.cos(ar), jnp.cos(ar), jnp.cos(ac), jnp.cos(ac)], axis=1)
    sin = jnp.concatenate([-jnp.sin(ar), jnp.sin(ar), -jnp.sin(ac), jnp.sin(ac)], axis=1)
    lat = jnp.concatenate([cos, sin], axis=1)
    ctx = jnp.concatenate([jnp.ones((n_ctx, QK_ROPE), f32), jnp.zeros((n_ctx, QK_ROPE), f32)], axis=1)
    return jnp.concatenate([ctx, lat], axis=0)


def _pair_swap_perm():
    q = QK_ROPE // 4
    return np.concatenate([np.arange(q, 2 * q), np.arange(0, q), np.arange(3 * q, 4 * q), np.arange(2 * q, 3 * q)])


def _na_bias_tables(rpb):
    cols = np.arange(GRID_W)
    c0 = np.clip(cols - WIN_COLS // 2, 0, GRID_W - WIN_COLS)
    inside = (cols[None, :] >= c0[:, None]) & (cols[None, :] < c0[:, None] + WIN_COLS)
    rel = np.clip(cols[None, :] - cols[:, None] + WIN_COLS - 1, 0, 2 * WIN_COLS - 2)
    full = jnp.where(inside[None, None], rpb[:, :, rel], NEG_BIG)
    tabs = [jnp.concatenate([full[:, d0 + jj] for jj in range(WIN_ROWS)], axis=-1) for d0 in range(WIN_ROWS)]
    return jnp.stack(tabs, axis=1).astype(f32)


def _layer_weights(l, w_in, rwkv_conv, rwkv_w0, rwkv_w_up, rwkv_a0, rwkv_a_up, rwkv_g_up, rwkv_k_k, rwkv_k_a,
                   rwkv_r_k, mla_q_norm_g, mla_w_uq, mla_kv_norm_g, mla_w_ukv, w_branch, w_out, mlp_w1, mlp_w2):
    d = w_in.shape[1]
    aw = rwkv_k_k.shape[1]
    nh_b = mla_w_ukv.shape[2] // (QK_NOPE + V_DIM)
    n_lora = 2 * LORA_W + 2 * LORA_A + LORA_G
    c_lora = 3 * aw
    c_mla = c_lora + n_lora
    c_na = c_mla + Q_LORA + KV_LORA + QK_ROPE
    c_gate = c_na + 3 * aw
    wi = w_in[l]
    perm = _pair_swap_perm()
    kr = wi[:, c_mla + Q_LORA + KV_LORA:c_na]
    w_b = jnp.concatenate([
        wi[:, c_lora:c_mla],
        wi[:, c_mla + Q_LORA:c_mla + Q_LORA + KV_LORA],
        wi[:, c_mla:c_mla + Q_LORA], jnp.zeros((d, 512 - Q_LORA), f32),
        kr, kr[:, perm]], axis=1).astype(bf16)
    cols = dict(c_kvd=n_lora, c_q=n_lora + KV_LORA, c_kr=n_lora + KV_LORA + 512)

    def lora_pad(w, start):
        r = w.shape[1]
        out = jnp.zeros((2, n_lora, aw), f32)
        for z in range(2):
            out = out.at[z, start + z * r:start + (z + 1) * r].set(w[z])
        return out.astype(bf16)

    g_up = jnp.zeros((n_lora, aw), f32).at[2 * LORA_W + 2 * LORA_A:].set(rwkv_g_up[l]).astype(bf16)
    head_id = np.arange(2 * LANES) // A_HEAD_DIM
    seg = jnp.asarray(head_id[:, None] == head_id[None, :], bf16)
    rwkv = dict(conv_rkv=rwkv_conv[l][:, :c_lora], conv_l=rwkv_conv[l][:, c_lora:c_mla],
                w_up=lora_pad(rwkv_w_up[l], 0), a_up=lora_pad(rwkv_a_up[l], 2 * LORA_W), g_up=g_up,
                w0=rwkv_w0[l], a0=rwkv_a0[l], k_k=rwkv_k_k[l][None], k_a=rwkv_k_a[l][None],
                r_k=rwkv_r_k[l].reshape(1, aw), seg=seg)

    uq = mla_w_uq[l].reshape(Q_LORA, nh_b, QK_NOPE + QK_ROPE)
    uq = jnp.concatenate([uq, jnp.zeros((512 - Q_LORA, nh_b, QK_NOPE + QK_ROPE), f32)], axis=0)
    zpad = jnp.zeros((512, nh_b, LANES - QK_ROPE), f32)
    pe = uq[:, :, QK_NOPE:]
    ukv = mla_w_ukv[l].reshape(KV_LORA, nh_b, QK_NOPE + V_DIM)
    mla = dict(
        gq=jnp.concatenate([mla_q_norm_g[l], jnp.zeros((512 - Q_LORA,), f32)])[None],
        gkv=mla_kv_norm_g[l][None],
        wq_nope=uq[:, :, :QK_NOPE].reshape(512, nh_b * QK_NOPE).astype(bf16),
        wq_pe=jnp.concatenate([pe, zpad], axis=2).reshape(512, nh_b * LANES).astype(bf16),
        wq_pr=jnp.concatenate([pe[:, :, perm], zpad], axis=2).reshape(512, nh_b * LANES).astype(bf16),
        wkv=jnp.concatenate([ukv[:, :, :QK_NOPE].reshape(KV_LORA, -1), ukv[:, :, QK_NOPE:].reshape(KV_LORA, -1)],
                            axis=1).astype(bf16))
    return dict(w_rkv=wi[:, :c_lora].astype(bf16), w_b=w_b, w_na=wi[:, c_na:c_gate].astype(bf16),
                w_gate=wi[:, c_gate:].astype(bf16), cols=cols, rwkv=rwkv, mla=mla,
                w_branch=w_branch[l].astype(bf16), w_out=w_out[l].astype(bf16),
                mlp_w1=mlp_w1[l].astype(bf16), mlp_w2=mlp_w2[l].astype(bf16))


def kernel(x, c, ctx, c_ctx, ada_w, ada_b, norm_mix_g, norm_mlp_g, w_in, rwkv_conv, rwkv_w0, rwkv_w_up, rwkv_a0, rwkv_a_up, rwkv_g_up, rwkv_k_k, rwkv_k_a, rwkv_r_k, rwkv_ln_g, rwkv_ln_b, mla_q_norm_g, mla_w_uq, mla_kv_norm_g, mla_w_ukv, na_rpb, w_branch, w_out, mlp_w1, mlp_w2, final_norm_g):
    batch, seq, d = x.shape
    n_ctx = ctx.shape[1]
    depth = ada_w.shape[0]
    assert batch == 1 and n_ctx == ROW_TILE and seq % ROW_TILE == 0 and seq // GRID_W >= WIN_ROWS
    n_ctx_tiles = n_ctx // ROW_TILE

    xs = jnp.concatenate([ctx[0], x[0]], axis=0)
    cond = jnp.concatenate([c, c_ctx[None], jnp.zeros((6, d), f32)], axis=0)
    mods = ada_modulation(cond, ada_w, ada_b)
    cs = _rope_tables(n_ctx, seq)

    for l in range(depth):
        lw = _layer_weights(l, w_in, rwkv_conv, rwkv_w0, rwkv_w_up, rwkv_a0, rwkv_a_up, rwkv_g_up, rwkv_k_k,
                            rwkv_k_a, rwkv_r_k, mla_q_norm_g, mla_w_uq, mla_kv_norm_g, mla_w_ukv, w_branch,
                            w_out, mlp_w1, mlp_w2)
        mods8 = mods[l]
        mods2 = mods8[:2, None, :]
        h = norm_modulate(xs, norm_mix_g[l], mods2, 0)
        u3 = matmul(h, lw["w_rkv"], f32, 1024)
        ub = matmul(h, lw["w_b"], f32, lw["w_b"].shape[1])
        uc = matmul(h, lw["w_na"], bf16, 1024)
        gates = matmul(h, lw["w_gate"], bf16, 1024, sigmoid=True)

        r, v, kk, bonus, g, lwd, kd, bd = rwkv_prepare(u3, ub, lw["rwkv"], n_ctx)
        y = rwkv_chunk_scan(r, v, kk, lwd, kd, bd, n_ctx)
        ya = rwkv_readout(y, bonus, g, rwkv_ln_g[l], rwkv_ln_b[l], lw["rwkv"]["seg"][:LANES, :LANES])

        q_b, k_b, v_b = mla_prepare(ub, cs, lw["mla"], lw["cols"])
        yb = mla_attention(q_b, k_b, v_b, n_ctx)

        yc = na_attention(uc, _na_bias_tables(na_rpb[l]), n_ctx)

        merged = merge_branches(ya, yb, yc, gates, lw["w_branch"])
        xs = matmul_gated_residual(merged, lw["w_out"], xs, mods8, 2, n_ctx)
        h2 = norm_modulate(xs, norm_mlp_g[l], mods2, 3)
        xs = mlp_gated_residual(h2, lw["mlp_w1"], lw["mlp_w2"], xs, mods8, 5, n_ctx)

    return final_norm(xs, final_norm_g, n_ctx_tiles)[None]
```

```python
import functools
import math

import jax
import jax.numpy as jnp
import numpy as np
from jax import lax
from jax.experimental import pallas as pl
from jax.experimental.pallas import tpu as pltpu

f32 = jnp.float32
bf16 = jnp.bfloat16

GRID_W = 64
RMS_EPS = 1e-6
N_BRANCH = 3
A_HEAD_DIM = 64
LORA_W = 96
LORA_A = 96
LORA_G = 256
GN_EPS = 64e-5
QK_NOPE = 128
QK_ROPE = 64
V_DIM = 128
Q_LORA = 448
KV_LORA = 128
MLA_SCALE = (QK_NOPE + QK_ROPE) ** -0.5
MLA_Q_SCALE = MLA_SCALE * math.log2(math.e)
ROPE_BASE = 10000.0
C_HEAD_DIM = 64
WIN_ROWS = 8
WIN_COLS = 16
NA_SCALE = C_HEAD_DIM ** -0.5
DECAY_SCALE = math.exp(-0.5)

LANES = 128
ROW_TILE = 256
CHUNK = 64
CHUNKS_PER_STEP = 2
PREP_TILE = 128
NA_PAIRS_PER_STEP = 2
MLA_CHAIN_LAG = 2
MLA_HEADS_PER_STEP = 2
VMEM_LIMIT = 48 * 1024 * 1024
NEG_BIG = -1e30


def _params(sem):
    return pltpu.CompilerParams(dimension_semantics=sem, vmem_limit_bytes=VMEM_LIMIT)


def _dot(a, b):
    return jnp.dot(a, b, preferred_element_type=f32)


def _dot_nt(a, b):
    return lax.dot_general(a, b, (((1,), (1,)), ((), ())), preferred_element_type=f32)


def _ada_body(s_ref, w_ref, b_ref, o_ref):
    s = s_ref[...]
    s = s * jax.nn.sigmoid(s)
    o_ref[...] = _dot(s, w_ref[...]) + b_ref[...]


def ada_modulation(cond, ada_w, ada_b):
    nl, d, n6 = ada_w.shape
    tn = 1024
    return pl.pallas_call(
        _ada_body,
        grid=(nl, n6 // tn),
        in_specs=[
            pl.BlockSpec((8, d), lambda l, j: (0, 0)),
            pl.BlockSpec((None, d, tn), lambda l, j: (l, 0, j)),
            pl.BlockSpec((None, 1, tn), lambda l, j: (l, 0, j)),
        ],
        out_specs=pl.BlockSpec((None, 8, tn), lambda l, j: (l, 0, j)),
        out_shape=jax.ShapeDtypeStruct((nl, 8, n6), f32),
        compiler_params=_params(("arbitrary", "arbitrary")),
        name="ada_modulation",
    )(cond, ada_w, ada_b.reshape(nl, 1, n6))


def _norm_mod_body(x_ref, g_ref, m_ref, o_ref, *, off, d):
    x = x_ref[...]
    y = x * lax.rsqrt(jnp.mean(x * x, axis=-1, keepdims=True) + RMS_EPS) * g_ref[...]
    shift = m_ref[:, off * d:(off + 1) * d]
    scale = m_ref[:, (off + 1) * d:(off + 2) * d]
    o_ref[...] = (y * (1.0 + scale) + shift).astype(o_ref.dtype)


def norm_modulate(x, g, mods2, off):
    n, d = x.shape
    return pl.pallas_call(
        functools.partial(_norm_mod_body, off=off, d=d),
        grid=(n // ROW_TILE,),
        in_specs=[
            pl.BlockSpec((ROW_TILE, d), lambda i: (i, 0)),
            pl.BlockSpec((1, d), lambda i: (0, 0)),
            pl.BlockSpec((None, 1, mods2.shape[-1]), lambda i: (jnp.where(i == 0, 1, 0), 0, 0)),
        ],
        out_specs=pl.BlockSpec((ROW_TILE, d), lambda i: (i, 0)),
        out_shape=jax.ShapeDtypeStruct((n, d), bf16),
        compiler_params=_params(("arbitrary",)),
        name="norm_modulate",
    )(x, g.reshape(1, d), mods2)


def _final_norm_body(x_ref, g_ref, o_ref):
    x = x_ref[...]
    o_ref[...] = x * lax.rsqrt(jnp.mean(x * x, axis=-1, keepdims=True) + RMS_EPS) * g_ref[...]


def final_norm(x, g, n_ctx_tiles):
    n, d = x.shape
    t = n - n_ctx_tiles * ROW_TILE
    return pl.pallas_call(
        _final_norm_body,
        grid=(t // ROW_TILE,),
        in_specs=[
            pl.BlockSpec((ROW_TILE, d), lambda i: (i + n_ctx_tiles, 0)),
            pl.BlockSpec((1, d), lambda i: (0, 0)),
        ],
        out_specs=pl.BlockSpec((ROW_TILE, d), lambda i: (i, 0)),
        out_shape=jax.ShapeDtypeStruct((t, d), f32),
        compiler_params=_params(("arbitrary",)),
        name="final_norm",
    )(x, g.reshape(1, d))


def _mm_body(a_ref, w_ref, o_ref, *, sigmoid):
    acc = _dot(a_ref[...], w_ref[...])
    o_ref[...] = (jax.nn.sigmoid(acc) if sigmoid else acc).astype(o_ref.dtype)


def _row_tile(n):
    for tm in (1056, 768, 512, 384, 256):
        if n % tm == 0:
            return tm
    raise ValueError(f"no row tile for {n} rows")


def matmul(a, w, out_dtype, tn, sigmoid=False, layer=None, col0=0, ncols=None):
    m, k = a.shape
    n = (w.shape[-1] - col0) if ncols is None else ncols
    tm = _row_tile(m)
    assert col0 % tn == 0 and n % tn == 0
    j0 = col0 // tn
    if layer is None:
        w_spec = pl.BlockSpec((k, tn), lambda i, j: (0, j0 + j))
    else:
        w_spec = pl.BlockSpec((None, k, tn), lambda i, j: (layer, 0, j0 + j))
    return pl.pallas_call(
        functools.partial(_mm_body, sigmoid=sigmoid),
        grid=(m // tm, n // tn),
        in_specs=[pl.BlockSpec((tm, k), lambda i, j: (i, 0)), w_spec],
        out_specs=pl.BlockSpec((tm, tn), lambda i, j: (i, j)),
        out_shape=jax.ShapeDtypeStruct((m, n), out_dtype),
        compiler_params=_params(("arbitrary", "arbitrary")),
        name="matmul",
    )(a, w)


def _cast_body(x_ref, o_ref):
    o_ref[...] = x_ref[...].astype(o_ref.dtype)


def cast_w_in_groups(w_in, skip_start, skip_cols):
    nl, d, c = w_in.shape
    tc = 2 * LANES
    assert skip_start % tc == 0 and skip_cols % tc == 0 and c % tc == 0
    j_skip, n_skip = skip_start // tc, skip_cols // tc
    return pl.pallas_call(
        _cast_body,
        grid=(nl, (c - skip_cols) // tc),
        in_specs=[pl.BlockSpec((None, d, tc), lambda l, j: (l, 0, jnp.where(j < j_skip, j, j + n_skip)))],
        out_specs=pl.BlockSpec((None, d, tc), lambda l, j: (l, 0, j)),
        out_shape=jax.ShapeDtypeStruct((nl, d, c - skip_cols), bf16),
        compiler_params=_params(("arbitrary", "arbitrary")),
        name="cast_w_in_groups",
    )(w_in)


def _gate_rows(m_ref, tm, n_ctx):
    rows = pl.program_id(0) * tm + lax.broadcasted_iota(jnp.int32, (tm, 1), 0)
    return jnp.where(rows < n_ctx, m_ref[1:2, :], m_ref[0:1, :])


def _mm_res_body(a_ref, w_ref, x_ref, m_ref, o_ref, *, tm, n_ctx):
    acc = _dot(a_ref[...], w_ref[...])
    o_ref[...] = x_ref[...] + _gate_rows(m_ref, tm, n_ctx) * acc


def matmul_gated_residual(a, w, x, mods8, gate_off, n_ctx, tn=512):
    m, k = a.shape
    n = w.shape[1]
    tm = _row_tile(m)
    nj = n // tn
    return pl.pallas_call(
        functools.partial(_mm_res_body, tm=tm, n_ctx=n_ctx),
        grid=(m // tm, nj),
        in_specs=[
            pl.BlockSpec((tm, k), lambda i, j: (i, 0)),
            pl.BlockSpec((k, tn), lambda i, j: (0, j)),
            pl.BlockSpec((tm, tn), lambda i, j: (i, j)),
            pl.BlockSpec((8, tn), lambda i, j: (0, gate_off * nj + j)),
        ],
        out_specs=pl.BlockSpec((tm, tn), lambda i, j: (i, j)),
        out_shape=jax.ShapeDtypeStruct((m, n), f32),
        compiler_params=_params(("arbitrary", "arbitrary")),
        name="matmul_gated_residual",
    )(a, w, x, mods8)


def _merge_body(ya_ref, yb_ref, yc_ref, g0_ref, g1_ref, g2_ref, w_ref, o_ref):
    acc = g0_ref[...] * _dot(ya_ref[...], w_ref[0])
    acc = acc + g1_ref[...] * _dot(yb_ref[...], w_ref[1])
    acc = acc + g2_ref[...] * _dot(yc_ref[...], w_ref[2])
    o_ref[...] = acc.astype(o_ref.dtype)


def merge_branches(ya, yb, yc, gates, w_branch, tn=1024):
    m, k = ya.shape
    d = w_branch.shape[-1]
    tm = 528 if m % 528 == 0 else ROW_TILE
    nj = d // tn
    y_spec = pl.BlockSpec((tm, k), lambda i, j: (i, 0))
    return pl.pallas_call(
        _merge_body,
        grid=(m // tm, nj),
        in_specs=[
            y_spec, y_spec, y_spec,
            pl.BlockSpec((tm, tn), lambda i, j: (i, j)),
            pl.BlockSpec((tm, tn), lambda i, j: (i, nj + j)),
            pl.BlockSpec((tm, tn), lambda i, j: (i, 2 * nj + j)),
            pl.BlockSpec((N_BRANCH, k, tn), lambda i, j: (0, 0, j)),
        ],
        out_specs=pl.BlockSpec((tm, tn), lambda i, j: (i, j)),
        out_shape=jax.ShapeDtypeStruct((m, d), bf16),
        compiler_params=_params(("arbitrary", "arbitrary")),
        name="merge_branches",
    )(ya, yb, yc, gates, gates, gates, w_branch)


def _mlp_body(h_ref, w1_ref, w2_ref, x_ref, m_ref, o_ref, acc_ref, *, tm, n_ctx):
    f = pl.program_id(1)

    @pl.when(f == 0)
    def _():
        acc_ref[...] = jnp.zeros_like(acc_ref)

    h1 = jnp.maximum(_dot(h_ref[...], w1_ref[...]), 0.0)
    acc_ref[...] += _dot((h1 * h1).astype(bf16), w2_ref[...])

    @pl.when(f == pl.num_programs(1) - 1)
    def _():
        o_ref[...] = x_ref[...] + _gate_rows(m_ref, tm, n_ctx) * acc_ref[...]


def mlp_gated_residual(h, w1, w2, x, mods8, gate_off, n_ctx, tf=1024):
    m, d = h.shape
    dff = w1.shape[1]
    tm = 528 if m % 528 == 0 else ROW_TILE
    return pl.pallas_call(
        functools.partial(_mlp_body, tm=tm, n_ctx=n_ctx),
        grid=(m // tm, dff // tf),
        in_specs=[
            pl.BlockSpec((tm, d), lambda i, f: (i, 0)),
            pl.BlockSpec((d, tf), lambda i, f: (0, f)),
            pl.BlockSpec((tf, d), lambda i, f: (f, 0)),
            pl.BlockSpec((tm, d), lambda i, f: (i, 0)),
            pl.BlockSpec((8, d), lambda i, f: (0, gate_off)),
        ],
        out_specs=pl.BlockSpec((tm, d), lambda i, f: (i, 0)),
        out_shape=jax.ShapeDtypeStruct((m, d), f32),
        scratch_shapes=[pltpu.VMEM((tm, d), f32)],
        compiler_params=_params(("arbitrary", "arbitrary")),
        name="mlp_gated_residual",
    )(h, w1, w2, x, mods8)


def _seg_sum(x, e_ref):
    hi = x.astype(bf16)
    lo = (x - hi.astype(f32)).astype(bf16)
    e = e_ref[...]
    w = e.shape[0]
    parts = [_dot(hi[:, g:g + w], e) + _dot(lo[:, g:g + w], e) for g in range(0, x.shape[1], w)]
    return parts[0] if len(parts) == 1 else jnp.concatenate(parts, axis=1)


def _to_pairs(o_ref, val, lead=None):
    for p in range(val.shape[-1] // LANES):
        piece = val[:, p * LANES:(p + 1) * LANES]
        if lead is None:
            o_ref[p] = piece
        else:
            o_ref[lead, p] = piece


def _rwkv_prep_body(u_ref, up_ref, un_ref, l_ref, lp_ref, ln_ref, cw_ref, cwl_ref, wu_ref, au_ref, gu_ref,
                    w0_ref, a0_ref, kk_ref, ka_ref, rk_ref, e_ref,
                    r_o, v_o, kkn_o, bonus_o, g_o, lw_o, kd_o, bd_o, *, n_ctx_tiles, aw):
    i = pl.program_id(0)
    last = pl.num_programs(0) - 1
    left_zero = (i == 0) | (i == n_ctx_tiles)
    right_zero = (i == n_ctx_tiles - 1) | (i == last)
    rows = lax.broadcasted_iota(jnp.int32, (PREP_TILE, 1), 0)

    def conv(x_ref, xp_ref, xn_ref, w_ref):
        x = x_ref[...]
        prev_row = jnp.where(left_zero, 0.0, xp_ref[7:8, :])
        next_row = jnp.where(right_zero, 0.0, xn_ref[0:1, :])
        x_prev = jnp.where(rows == 0, prev_row, pltpu.roll(x, 1, 0))
        x_next = jnp.where(rows == PREP_TILE - 1, next_row, pltpu.roll(x, PREP_TILE - 1, 0))
        return w_ref[0:1, :] * x_prev + w_ref[1:2, :] * x + w_ref[2:3, :] * x_next

    y = conv(u_ref, up_ref, un_ref, cw_ref)
    r, k, v = y[:, :aw], y[:, aw:2 * aw], y[:, 2 * aw:]
    yl = conv(l_ref, lp_ref, ln_ref, cwl_ref)
    th = jnp.tanh(yl).astype(bf16)
    sg = jax.nn.sigmoid(yl).astype(bf16)
    ylb = yl.astype(bf16)

    _to_pairs(g_o, _dot(sg, gu_ref[...]))
    _to_pairs(r_o, r)
    _to_pairs(v_o, v.astype(bf16))

    kkr = k * kk_ref[...]
    norm = jnp.sqrt(_seg_sum(kkr * kkr, e_ref))
    kkn = kkr / jnp.maximum(norm, 1e-12)
    _to_pairs(kkn_o, kkn)

    ksum = None
    for dr in range(2):
        z = w0_ref[dr:dr + 1, :] + _dot(th, wu_ref[dr])
        _to_pairs(lw_o, -DECAY_SCALE * jax.nn.sigmoid(z), lead=dr)
        a = jax.nn.sigmoid(a0_ref[dr:dr + 1, :] + _dot(ylb, au_ref[dr]))
        kd = k * (1.0 + (a - 1.0) * ka_ref[...])
        _to_pairs(kd_o, kd, lead=dr)
        _to_pairs(bd_o, kkn * a, lead=dr)
        ksum = kd if ksum is None else ksum + kd
    rk = _seg_sum(r * (0.5 * ksum) * rk_ref[...], e_ref)
    _to_pairs(bonus_o, rk * v)


def rwkv_prepare(u3, ub, pw, n_ctx):
    n = u3.shape[0]
    aw = u3.shape[1] // 3
    npair = aw // LANES
    lw = pw["conv_l"].shape[1]
    tpb = PREP_TILE // 8
    nb8 = n // 8

    def prev_map(i):
        return (jnp.maximum(i * tpb - 1, 0), 0)

    def next_map(i):
        return (jnp.minimum((i + 1) * tpb, nb8 - 1), 0)

    full = lambda a: pl.BlockSpec(a.shape, lambda i: (0,) * a.ndim)
    pm = jax.ShapeDtypeStruct((npair, n, LANES), f32)
    pm2 = jax.ShapeDtypeStruct((2, npair, n, LANES), f32)
    pm_bf16 = jax.ShapeDtypeStruct((npair, n, LANES), bf16)
    pm_spec = pl.BlockSpec((npair, PREP_TILE, LANES), lambda i: (0, i, 0))
    pm2_spec = pl.BlockSpec((2, npair, PREP_TILE, LANES), lambda i: (0, 0, i, 0))
    consts = [pw["conv_rkv"], pw["conv_l"], pw["w_up"], pw["a_up"], pw["g_up"], pw["w0"], pw["a0"],
              pw["k_k"], pw["k_a"], pw["r_k"], pw["seg"]]
    return pl.pallas_call(
        functools.partial(_rwkv_prep_body, n_ctx_tiles=n_ctx // PREP_TILE, aw=aw),
        grid=(n // PREP_TILE,),
        in_specs=[
            pl.BlockSpec((PREP_TILE, 3 * aw), lambda i: (i, 0)),
            pl.BlockSpec((8, 3 * aw), prev_map),
            pl.BlockSpec((8, 3 * aw), next_map),
            pl.BlockSpec((PREP_TILE, lw), lambda i: (i, 0)),
            pl.BlockSpec((8, lw), prev_map),
            pl.BlockSpec((8, lw), next_map),
        ] + [full(a) for a in consts],
        out_specs=[pm_spec] * 5 + [pm2_spec] * 3,
        out_shape=[pm, pm_bf16] + [pm] * 3 + [pm2] * 3,
        compiler_params=_params(("arbitrary",)),
        name="rwkv_prepare",
    )(u3, u3, u3, ub, ub, ub, *consts)


def _dot_tn(a, b):
    return lax.dot_general(a, b, (((0,), (0,)), ((), ())), preferred_element_type=f32)


def _rwkv_chunk_body(r_ref, v_ref, kk_ref, lw_ref, k_ref, b_ref, y_ref, h_ref, *, n_pairs):
    d = pl.program_id(0)
    j = pl.program_id(1)
    c = CHUNK
    c2 = 2 * c

    @pl.when(j == 0)
    def _():
        h_ref[...] = jnp.zeros_like(h_ref)

    sign = jnp.where(d == 0, 1, -1)
    ti = lax.broadcasted_iota(jnp.int32, (c, c), 0)
    si = lax.broadcasted_iota(jnp.int32, (c, c), 1)
    incl_c = jnp.where((ti - si) * sign >= 0, 1.0, 0.0).astype(bf16)
    t2 = lax.broadcasted_iota(jnp.int32, (c2, c2), 0)
    s2 = lax.broadcasted_iota(jnp.int32, (c2, c2), 1)
    ahead = (t2 - s2) * sign
    blk = lambda n: (t2 // n) == (s2 // n)
    incl = blk(c) & (ahead >= 0)
    strict = blk(c) & (ahead > 0)
    eye = jnp.where(t2 == s2, 1.0, 0.0)
    inv_levels = []
    n = 2
    while n < c:
        inv_levels.append(strict & blk(2 * n) & ~blk(n))
        n *= 2
    low = lax.broadcasted_iota(jnp.int32, (1, LANES), 1) < A_HEAD_DIM

    def stack(x):
        return jnp.concatenate([jnp.where(low, x, 0.0), jnp.where(low, 0.0, x)], axis=0).astype(bf16)

    pairs = range(n_pairs)
    first = jnp.where(d == 0, 0, CHUNKS_PER_STEP - 1)
    offs = [pl.multiple_of((first + sign * sc) * c, c) for sc in range(CHUNKS_PER_STEP)]
    units = [(sc, p) for sc in range(CHUNKS_PER_STEP) for p in pairs]
    each = lambda f: [f(sc, p) for sc, p in units]
    at = lambda sc, p: sc * n_pairs + p
    ld = lambda ref, sc, p: ref[p, pl.ds(offs[sc], c), :]
    lw = each(lambda sc, p: ld(lw_ref, sc, p))
    lw_hi = [x.astype(bf16) for x in lw]
    lw_lo = [(x - hi.astype(f32)).astype(bf16) for x, hi in zip(lw, lw_hi)]
    cum = [_dot(incl_c, hi) + _dot(incl_c, lo) for hi, lo in zip(lw_hi, lw_lo)]
    tot = [jnp.sum(x, axis=0, keepdims=True) for x in lw]
    e_neg = [jnp.exp(-x) for x in cum]
    e_end = [jnp.exp(t - x) for t, x in zip(tot, cum)]
    r_s = each(lambda sc, p: stack(ld(r_ref, sc, p) * jnp.exp(cum[at(sc, p)])))
    kk_s = each(lambda sc, p: stack(ld(kk_ref, sc, p) * jnp.exp(cum[at(sc, p)] - lw[at(sc, p)])))
    kb_s = each(lambda sc, p: jnp.concatenate([stack(ld(b_ref, sc, p) * e_neg[at(sc, p)]),
                                               stack(ld(k_ref, sc, p) * e_neg[at(sc, p)])], axis=0))
    kbd_s = each(lambda sc, p: jnp.concatenate([stack(ld(k_ref, sc, p) * e_end[at(sc, p)]),
                                                stack(ld(b_ref, sc, p) * e_end[at(sc, p)])], axis=0))
    v_s = each(lambda sc, p: stack(ld(v_ref, sc, p)))
    a1 = [_dot_nt(a, b) for a, b in zip(kk_s, kb_s)]
    a2 = [_dot_nt(a, b) for a, b in zip(r_s, kb_s)]
    a_kb = [jnp.where(strict, a[:, :c2], 0.0) for a in a1]
    a_kk = [jnp.where(strict, a[:, c2:], 0.0).astype(bf16) for a in a1]
    a_rb = [jnp.where(incl, a[:, :c2], 0.0).astype(bf16) for a in a2]
    a_rk = [jnp.where(incl, a[:, c2:], 0.0).astype(bf16) for a in a2]
    t_inv = [eye - jnp.where(blk(2), a, 0.0) for a in a_kb]
    for m in inv_levels:
        t_b = [t.astype(bf16) for t in t_inv]
        x = [_dot(t, jnp.where(m, a, 0.0).astype(bf16)).astype(bf16) for t, a in zip(t_b, a_kb)]
        t_inv = [t - _dot(xx, tb) for t, xx, tb in zip(t_inv, x, t_b)]
    g = [_dot(t.astype(bf16), jnp.concatenate([kk, akk], axis=1)).astype(bf16)
         for t, kk, akk in zip(t_inv, kk_s, a_kk)]
    w_col = [jnp.sum(eye * jnp.exp(t), axis=1, keepdims=True) for t in tot]
    h = [h_ref[p] for p in pairs]
    for sc in range(CHUNKS_PER_STEP):
        ix = [at(sc, p) for p in pairs]
        h_b = [x.astype(bf16) for x in h]
        u = [_dot(g[i], jnp.concatenate([h_b[p], v_s[i]], axis=0)).astype(bf16) for p, i in enumerate(ix)]
        y = [_dot(jnp.concatenate([r_s[i], a_rk[i], -a_rb[i]], axis=1), jnp.concatenate([h_b[p], v_s[i], u[p]], axis=0))
             for p, i in enumerate(ix)]
        h1 = [_dot_tn(kbd_s[i], jnp.concatenate([v_s[i], -u[p]], axis=0)) for p, i in enumerate(ix)]
        for p in pairs:
            y_ref[p, pl.ds(offs[sc], c), :] = y[p][:c] + y[p][c:]
        h = [h[p] * w_col[i] + h1[p] for p, i in enumerate(ix)]
    for p in pairs:
        h_ref[p] = h[p]


def rwkv_chunk_scan(r, v, kk, lw, kd, bd, n_ctx):
    npair, n, _ = r.shape
    blk_rows = CHUNKS_PER_STEP * CHUNK
    nc = n // blk_rows
    ncc = n_ctx // blk_rows

    def cidx(d, j):
        bwd = jnp.where(j < ncc, ncc - 1 - j, nc - 1 - (j - ncc))
        return jnp.where(d == 0, j, bwd)

    shared = pl.BlockSpec((npair, blk_rows, LANES), lambda d, j: (0, cidx(d, j), 0))
    per_dir = pl.BlockSpec((None, npair, blk_rows, LANES), lambda d, j: (d, 0, cidx(d, j), 0))
    return pl.pallas_call(
        functools.partial(_rwkv_chunk_body, n_pairs=npair),
        grid=(2, nc),
        in_specs=[shared, shared, shared, per_dir, per_dir, per_dir],
        out_specs=per_dir,
        out_shape=jax.ShapeDtypeStruct((2, npair, n, LANES), f32),
        scratch_shapes=[pltpu.VMEM((npair, LANES, LANES), f32)],
        compiler_params=_params(("arbitrary", "arbitrary")),
        name="rwkv_chunk_scan",
    )(r, v, kk, lw, kd, bd)


def _rwkv_readout_body(y_ref, bonus_ref, g_ref, lng_ref, lnb_ref, e_ref, o_ref, *, n_pairs):
    inv = 1.0 / A_HEAD_DIM
    for p in range(n_pairs):
        y = y_ref[0, p] + y_ref[1, p]
        yc = y - _seg_sum(y, e_ref) * inv
        var = _seg_sum(yc * yc, e_ref) * inv
        yn = yc * lax.rsqrt(var + GN_EPS) * lng_ref[p] + lnb_ref[p]
        o_ref[:, p * LANES:(p + 1) * LANES] = ((yn + bonus_ref[p]) * g_ref[p]).astype(o_ref.dtype)


def rwkv_readout(y, bonus, g, ln_g, ln_b, seg_pair):
    _, npair, n, _ = y.shape
    pm_spec = pl.BlockSpec((npair, ROW_TILE, LANES), lambda i: (0, i, 0))
    vec_spec = pl.BlockSpec((npair, 1, LANES), lambda i: (0, 0, 0))
    return pl.pallas_call(
        functools.partial(_rwkv_readout_body, n_pairs=npair),
        grid=(n // ROW_TILE,),
        in_specs=[pl.BlockSpec((2, npair, ROW_TILE, LANES), lambda i: (0, 0, i, 0)), pm_spec, pm_spec,
                  vec_spec, vec_spec, pl.BlockSpec((LANES, LANES), lambda i: (0, 0))],
        out_specs=pl.BlockSpec((ROW_TILE, npair * LANES), lambda i: (i, 0)),
        out_shape=jax.ShapeDtypeStruct((n, npair * LANES), bf16),
        compiler_params=_params(("arbitrary",)),
        name="rwkv_readout",
    )(y, bonus, g, ln_g.reshape(npair, 1, LANES), ln_b.reshape(npair, 1, LANES), seg_pair)


def _mla_prep_body(u_ref, cs_ref, gq_ref, gkv_ref, wqn_ref, wqp_ref, wqr_ref, wkv_ref, q_o, k_o, v_o,
                   *, n_heads, c_kvd, c_q, c_kr):
    ql = u_ref[:, c_q:c_q + 512]
    qn = ql * lax.rsqrt(jnp.sum(ql * ql, axis=-1, keepdims=True) * (1.0 / Q_LORA) + RMS_EPS) * gq_ref[...]
    qn = qn.astype(bf16)
    kvd = u_ref[:, c_kvd:c_kvd + KV_LORA]
    kvn = kvd * lax.rsqrt(jnp.mean(kvd * kvd, axis=-1, keepdims=True) + RMS_EPS) * gkv_ref[...]
    kv = _dot(kvn.astype(bf16), wkv_ref[...])
    cs = cs_ref[...]
    lane = lax.broadcasted_iota(jnp.int32, (1, LANES), 1)
    low = lane < QK_ROPE
    cos_t = jnp.where(low, cs, 0.0)
    sin_t = jnp.where(low, pltpu.roll(cs, QK_ROPE, 1), 0.0)
    kr = u_ref[:, c_kr:c_kr + LANES]
    prod = kr * cs
    k_pe = jnp.where(low, prod + pltpu.roll(prod, QK_ROPE, 1), 0.0).astype(bf16)
    q_nope = _dot(qn, wqn_ref[...])
    q_pe = _dot(qn, wqp_ref[...])
    q_pr = _dot(qn, wqr_ref[...])
    for h in range(n_heads):
        sl = slice(h * LANES, (h + 1) * LANES)
        q_o[h, :, 0:LANES] = (q_nope[:, sl] * MLA_Q_SCALE).astype(bf16)
        q_o[h, :, LANES:2 * LANES] = ((q_pe[:, sl] * cos_t + q_pr[:, sl] * sin_t) * MLA_Q_SCALE).astype(bf16)
        k_o[h, :, 0:LANES] = kv[:, sl].astype(bf16)
        k_o[h, :, LANES:2 * LANES] = k_pe
        v_o[h] = kv[:, n_heads * LANES + h * LANES:n_heads * LANES + (h + 1) * LANES].astype(bf16)


def mla_prepare(ub, cs, pw, cols):
    n = ub.shape[0]
    nh = pw["wq_nope"].shape[1] // LANES
    full = lambda a: pl.BlockSpec(a.shape, lambda i: (0,) * a.ndim)
    consts = [pw["gq"], pw["gkv"], pw["wq_nope"], pw["wq_pe"], pw["wq_pr"], pw["wkv"]]
    return pl.pallas_call(
        functools.partial(_mla_prep_body, n_heads=nh, **cols),
        grid=(n // ROW_TILE,),
        in_specs=[pl.BlockSpec((ROW_TILE, ub.shape[1]), lambda i: (i, 0)),
                  pl.BlockSpec((ROW_TILE, LANES), lambda i: (i, 0))] + [full(a) for a in consts],
        out_specs=[pl.BlockSpec((nh, ROW_TILE, 2 * LANES), lambda i: (0, i, 0)),
                   pl.BlockSpec((nh, ROW_TILE, 2 * LANES), lambda i: (0, i, 0)),
                   pl.BlockSpec((nh, ROW_TILE, LANES), lambda i: (0, i, 0))],
        out_shape=[jax.ShapeDtypeStruct((nh, n, 2 * LANES), bf16),
                   jax.ShapeDtypeStruct((nh, n, 2 * LANES), bf16),
                   jax.ShapeDtypeStruct((nh, n, LANES), bf16)],
        compiler_params=_params(("arbitrary",)),
        name="mla_prepare",
    )(ub, cs, *consts)


def _mla_attn_body(q_ref, k_ref, v_ref, o_ref, *, n_ctx, n_tok, tk, tq):
    i = pl.program_id(1)
    hs = range(MLA_HEADS_PER_STEP)
    subs = range(tq // ROW_TILE)

    def rows(sb):
        return slice(sb * ROW_TILE, (sb + 1) * ROW_TILE)

    def write(chains, o):
        for (h, sb), val in zip(chains, o):
            o_ref[rows(sb), h * V_DIM:(h + 1) * V_DIM] = val.astype(o_ref.dtype)

    def context_queries():
        chains = [(h, 0) for h in hs]
        each = lambda f: [f(n, h) for n, (h, _) in enumerate(chains)]
        s = each(lambda n, h: _dot_nt(q_ref[h, rows(0), :], k_ref[h, 0:n_ctx, :]))
        p = each(lambda n, h: jnp.exp2(s[n] - jnp.max(s[n], axis=-1, keepdims=True)))
        pv = each(lambda n, h: _dot(p[n].astype(bf16), v_ref[h, 0:n_ctx, :]))
        write(chains, each(lambda n, h: pv[n] / jnp.sum(p[n], axis=-1, keepdims=True)))

    def latent_queries(chains):
        each = lambda f: [f(n, h, sb) for n, (h, sb) in enumerate(chains)]
        q = each(lambda n, h, sb: q_ref[h, rows(sb), :])

        def step(c, carry):
            off = pl.multiple_of(c * tk, tk)
            nch = len(chains)
            s, out = [None] * nch, [None] * nch

            def finish(n):
                h = chains[n][0]
                m_old, l_old, acc_old = carry[n]
                m_new = jnp.maximum(m_old, jnp.max(s[n], axis=-1, keepdims=True))
                alpha = jnp.exp2(m_old - m_new)
                p = jnp.exp2(s[n] - m_new)
                l_new = alpha * l_old + jnp.sum(p, axis=-1, keepdims=True)
                pv = _dot(p.astype(bf16), v_ref[h, pl.ds(off, tk), :])
                out[n] = (m_new, l_new, alpha * acc_old + pv)

            for n in range(nch + MLA_CHAIN_LAG):
                if n < nch:
                    s[n] = _dot_nt(q[n], k_ref[chains[n][0], pl.ds(off, tk), :])
                if n >= MLA_CHAIN_LAG:
                    finish(n - MLA_CHAIN_LAG)
            return tuple(out)

        init = tuple((jnp.full((ROW_TILE, 1), NEG_BIG, f32), jnp.zeros((ROW_TILE, 1), f32),
                      jnp.zeros((ROW_TILE, V_DIM), f32)) for _ in chains)
        fin = lax.fori_loop(0, n_tok // tk, step, init)
        write(chains, [f[2] / f[1] for f in fin])

    @pl.when(i == 0)
    def _():
        context_queries()
        if len(subs) > 1:
            latent_queries([(h, sb) for h in hs for sb in subs[1:]])

    @pl.when(i > 0)
    def _():
        latent_queries([(h, sb) for h in hs for sb in subs])


def mla_attention(q, k, v, n_ctx):
    nh, n, dq = q.shape
    hps = MLA_HEADS_PER_STEP
    assert n_ctx == ROW_TILE
    tk = next(t for t in (768, 512, 256) if n % t == 0)
    tq = tk
    return pl.pallas_call(
        functools.partial(_mla_attn_body, n_ctx=n_ctx, n_tok=n, tk=tk, tq=tq),
        grid=(nh // hps, n // tq),
        in_specs=[
            pl.BlockSpec((hps, tq, dq), lambda h, i: (h, i, 0)),
            pl.BlockSpec((hps, n, dq), lambda h, i: (h, 0, 0)),
            pl.BlockSpec((hps, n, V_DIM), lambda h, i: (h, 0, 0)),
        ],
        out_specs=pl.BlockSpec((tq, hps * V_DIM), lambda h, i: (i, h)),
        out_shape=jax.ShapeDtypeStruct((n, nh * V_DIM), bf16),
        compiler_params=_params(("arbitrary", "arbitrary")),
        name="mla_attention",
    )(q, k, v)


def _na_body(q_ref, k_ref, v_ref, bias_ref, o_ref, *, n_ctx, n_rows):
    j = pl.program_id(1)
    lane = lax.broadcasted_iota(jnp.int32, (1, LANES), 1)
    low = lane < C_HEAD_DIM
    win = WIN_ROWS * GRID_W
    lanes = lambda lp: slice(lp * LANES, (lp + 1) * LANES)
    pairs = range(NA_PAIRS_PER_STEP)
    kc = [k_ref[0:n_ctx, lanes(lp)] for lp in pairs]
    vc = [v_ref[0:n_ctx, lanes(lp)] for lp in pairs]

    def head_mask(hh, x):
        return jnp.where(low if hh == 0 else ~low, x, jnp.zeros_like(x))

    @pl.when(j == 0)
    def _():
        units = [(lp, hh) for lp in pairs for hh in range(2)]
        each = lambda f: [f(n, lp, hh) for n, (lp, hh) in enumerate(units)]
        qm = each(lambda n, lp, hh: head_mask(hh, q_ref[:, lanes(lp)] * NA_SCALE))
        s = each(lambda n, lp, hh: _dot_nt(qm[n], kc[lp]))
        p = each(lambda n, lp, hh: jnp.exp(s[n] - jnp.max(s[n], axis=-1, keepdims=True)))
        o = each(lambda n, lp, hh: _dot(p[n].astype(bf16), vc[lp]) / jnp.sum(p[n], axis=-1, keepdims=True))
        for lp in pairs:
            o_ref[:, lanes(lp)] = jnp.where(low, o[2 * lp], o[2 * lp + 1]).astype(o_ref.dtype)

    @pl.when(j > 0)
    def _():
        grid_rows = range(ROW_TILE // GRID_W)
        units = [(il, lp, hh) for il in grid_rows for lp in pairs for hh in range(2)]
        per_row = lambda f: [f(il) for il in grid_rows]
        per_rp = lambda f: [[f(il, lp) for lp in pairs] for il in grid_rows]
        per_unit = lambda f: [f(n, il, lp, hh) for n, (il, lp, hh) in enumerate(units)]
        i = per_row(lambda il: (j - 1) * (ROW_TILE // GRID_W) + il)
        r0 = per_row(lambda il: jnp.clip(i[il] - WIN_ROWS // 2, 0, n_rows - WIN_ROWS))
        d0 = per_row(lambda il: r0[il] - i[il] + WIN_ROWS - 1)
        off = per_row(lambda il: pl.multiple_of(n_ctx + r0[il] * GRID_W, GRID_W))
        q2 = per_rp(lambda il, lp: q_ref[il * GRID_W:(il + 1) * GRID_W, lanes(lp)] * NA_SCALE)
        kw = per_rp(lambda il, lp: k_ref[pl.ds(off[il], win), lanes(lp)])
        vw = per_rp(lambda il, lp: v_ref[pl.ds(off[il], win), lanes(lp)])
        qm = per_unit(lambda n, il, lp, hh: head_mask(hh, q2[il][lp]))
        s_w = per_unit(lambda n, il, lp, hh: _dot_nt(qm[n], kw[il][lp]) + bias_ref[2 * lp + hh, d0[il]])
        s_c = per_unit(lambda n, il, lp, hh: _dot_nt(qm[n], kc[lp]))
        m = per_unit(lambda n, il, lp, hh: jnp.maximum(jnp.max(s_w[n], axis=-1, keepdims=True),
                                                       jnp.max(s_c[n], axis=-1, keepdims=True)))
        p_w = per_unit(lambda n, il, lp, hh: jnp.exp(s_w[n] - m[n]))
        p_c = per_unit(lambda n, il, lp, hh: jnp.exp(s_c[n] - m[n]))
        l = per_unit(lambda n, il, lp, hh: jnp.sum(p_w[n], axis=-1, keepdims=True)
                     + jnp.sum(p_c[n], axis=-1, keepdims=True))
        o = per_unit(lambda n, il, lp, hh: (_dot(p_w[n].astype(bf16), vw[il][lp])
                                            + _dot(p_c[n].astype(bf16), vc[lp])) / l[n])
        for n in range(0, len(units), 2):
            il, lp, _ = units[n]
            o_ref[il * GRID_W:(il + 1) * GRID_W, lanes(lp)] = jnp.where(low, o[n], o[n + 1]).astype(o_ref.dtype)


def na_attention(uc, bias, n_ctx):
    n = uc.shape[0]
    width = uc.shape[1] // 3
    pps = NA_PAIRS_PER_STEP
    ngroup = width // (pps * LANES)
    n_rows = (n - n_ctx) // GRID_W
    return pl.pallas_call(
        functools.partial(_na_body, n_ctx=n_ctx, n_rows=n_rows),
        grid=(ngroup, n // ROW_TILE),
        in_specs=[
            pl.BlockSpec((ROW_TILE, pps * LANES), lambda p, j: (j, p)),
            pl.BlockSpec((n, pps * LANES), lambda p, j: (0, ngroup + p)),
            pl.BlockSpec((n, pps * LANES), lambda p, j: (0, 2 * ngroup + p)),
            pl.BlockSpec((2 * pps, WIN_ROWS, GRID_W, WIN_ROWS * GRID_W), lambda p, j: (p, 0, 0, 0)),
        ],
        out_specs=pl.BlockSpec((ROW_TILE, pps * LANES), lambda p, j: (j, p)),
        out_shape=jax.ShapeDtypeStruct((n, width), bf16),
        compiler_params=_params(("arbitrary", "arbitrary")),
        name="na_attention",
    )(uc, uc, uc, bias)


def _rope_tables(n_ctx, seq):
    t = jnp.arange(seq)
    row = (t // GRID_W).astype(f32)
    col = (t % GRID_W).astype(f32)
    n_freq = QK_ROPE // 4
    inv = ROPE_BASE ** (-jnp.arange(n_freq, dtype=f32) / n_freq)
    ar, ac = row[:, None] * inv[None, :], col[:, None] * inv[None, :]
    cos = jnp.concatenate([jnp.cos(ar), jnp.cos(ar), jnp.cos(ac), jnp.cos(ac)], axis=1)
    sin = jnp.concatenate([-jnp.sin(ar), jnp.sin(ar), -jnp.sin(ac), jnp.sin(ac)], axis=1)
    lat = jnp.concatenate([cos, sin], axis=1)
    ctx = jnp.concatenate([jnp.ones((n_ctx, QK_ROPE), f32), jnp.zeros((n_ctx, QK_ROPE), f32)], axis=1)
    return jnp.concatenate([ctx, lat], axis=0)


def _pair_swap_perm():
    q = QK_ROPE // 4
    return np.concatenate([np.arange(q, 2 * q), np.arange(0, q), np.arange(3 * q, 4 * q), np.arange(2 * q, 3 * q)])


def _na_bias_tables(rpb):
    cols = np.arange(GRID_W)
    c0 = np.clip(cols - WIN_COLS // 2, 0, GRID_W - WIN_COLS)
    inside = (cols[None, :] >= c0[:, None]) & (cols[None, :] < c0[:, None] + WIN_COLS)
    rel = np.clip(cols[None, :] - cols[:, None] + WIN_COLS - 1, 0, 2 * WIN_COLS - 2)
    full = jnp.where(inside[None, None], rpb[:, :, rel], NEG_BIG)
    tabs = [jnp.concatenate([full[:, d0 + jj] for jj in range(WIN_ROWS)], axis=-1) for d0 in range(WIN_ROWS)]
    return jnp.stack(tabs, axis=1).astype(f32)


def _layer_weights(l, w_in, rwkv_conv, rwkv_w0, rwkv_w_up, rwkv_a0, rwkv_a_up, rwkv_g_up, rwkv_k_k, rwkv_k_a,
                   rwkv_r_k, mla_q_norm_g, mla_w_uq, mla_kv_norm_g, mla_w_ukv, w_branch, w_out, mlp_w1, mlp_w2):
    d = w_in.shape[1]
    aw = rwkv_k_k.shape[1]
    nh_b = mla_w_ukv.shape[2] // (QK_NOPE + V_DIM)
    n_lora = 2 * LORA_W + 2 * LORA_A + LORA_G
    c_lora = 3 * aw
    c_mla = c_lora + n_lora
    c_na = c_mla + Q_LORA + KV_LORA + QK_ROPE
    c_gate = c_na + 3 * aw
    wi = w_in[l]
    perm = _pair_swap_perm()
    kr = wi[:, c_mla + Q_LORA + KV_LORA:c_na]
    w_b = jnp.concatenate([
        wi[:, c_lora:c_mla],
        wi[:, c_mla + Q_LORA:c_mla + Q_LORA + KV_LORA],
        wi[:, c_mla:c_mla + Q_LORA], jnp.zeros((d, 512 - Q_LORA), f32),
        kr, kr[:, perm]], axis=1).astype(bf16)
    cols = dict(c_kvd=n_lora, c_q=n_lora + KV_LORA, c_kr=n_lora + KV_LORA + 512)

    def lora_pad(w, start):
        r = w.shape[1]
        out = jnp.zeros((2, n_lora, aw), f32)
        for z in range(2):
            out = out.at[z, start + z * r:start + (z + 1) * r].set(w[z])
        return out.astype(bf16)

    g_up = jnp.zeros((n_lora, aw), f32).at[2 * LORA_W + 2 * LORA_A:].set(rwkv_g_up[l]).astype(bf16)
    head_id = np.arange(2 * LANES) // A_HEAD_DIM
    seg = jnp.asarray(head_id[:, None] == head_id[None, :], bf16)
    rwkv = dict(conv_rkv=rwkv_conv[l][:, :c_lora], conv_l=rwkv_conv[l][:, c_lora:c_mla],
                w_up=lora_pad(rwkv_w_up[l], 0), a_up=lora_pad(rwkv_a_up[l], 2 * LORA_W), g_up=g_up,
                w0=rwkv_w0[l], a0=rwkv_a0[l], k_k=rwkv_k_k[l][None], k_a=rwkv_k_a[l][None],
                r_k=rwkv_r_k[l].reshape(1, aw), seg=seg)

    uq = mla_w_uq[l].reshape(Q_LORA, nh_b, QK_NOPE + QK_ROPE)
    uq = jnp.concatenate([uq, jnp.zeros((512 - Q_LORA, nh_b, QK_NOPE + QK_ROPE), f32)], axis=0)
    zpad = jnp.zeros((512, nh_b, LANES - QK_ROPE), f32)
    pe = uq[:, :, QK_NOPE:]
    ukv = mla_w_ukv[l].reshape(KV_LORA, nh_b, QK_NOPE + V_DIM)
    mla = dict(
        gq=jnp.concatenate([mla_q_norm_g[l], jnp.zeros((512 - Q_LORA,), f32)])[None],
        gkv=mla_kv_norm_g[l][None],
        wq_nope=uq[:, :, :QK_NOPE].reshape(512, nh_b * QK_NOPE).astype(bf16),
        wq_pe=jnp.concatenate([pe, zpad], axis=2).reshape(512, nh_b * LANES).astype(bf16),
        wq_pr=jnp.concatenate([pe[:, :, perm], zpad], axis=2).reshape(512, nh_b * LANES).astype(bf16),
        wkv=jnp.concatenate([ukv[:, :, :QK_NOPE].reshape(KV_LORA, -1), ukv[:, :, QK_NOPE:].reshape(KV_LORA, -1)],
                            axis=1).astype(bf16))
    return dict(w_b=w_b, cols=cols, rwkv=rwkv, mla=mla,
                w_branch=w_branch[l].astype(bf16), w_out=w_out[l].astype(bf16),
                mlp_w1=mlp_w1[l].astype(bf16), mlp_w2=mlp_w2[l].astype(bf16))


def kernel(x, c, ctx, c_ctx, ada_w, ada_b, norm_mix_g, norm_mlp_g, w_in, rwkv_conv, rwkv_w0, rwkv_w_up, rwkv_a0, rwkv_a_up, rwkv_g_up, rwkv_k_k, rwkv_k_a, rwkv_r_k, rwkv_ln_g, rwkv_ln_b, mla_q_norm_g, mla_w_uq, mla_kv_norm_g, mla_w_ukv, na_rpb, w_branch, w_out, mlp_w1, mlp_w2, final_norm_g):
    batch, seq, d = x.shape
    n_ctx = ctx.shape[1]
    depth = ada_w.shape[0]
    assert batch == 1 and n_ctx == ROW_TILE and seq % ROW_TILE == 0 and seq // GRID_W >= WIN_ROWS
    n_ctx_tiles = n_ctx // ROW_TILE

    xs = jnp.concatenate([ctx[0], x[0]], axis=0)
    cond = jnp.concatenate([c, c_ctx[None], jnp.zeros((6, d), f32)], axis=0)
    mods = ada_modulation(cond, ada_w, ada_b)
    cs = _rope_tables(n_ctx, seq)
    aw = rwkv_k_k.shape[1]
    n_b = w_in.shape[2] - 3 * aw - 3 * aw - N_BRANCH * d
    w_main = cast_w_in_groups(w_in, 3 * aw, n_b)

    for l in range(depth):
        lw = _layer_weights(l, w_in, rwkv_conv, rwkv_w0, rwkv_w_up, rwkv_a0, rwkv_a_up, rwkv_g_up, rwkv_k_k,
                            rwkv_k_a, rwkv_r_k, mla_q_norm_g, mla_w_uq, mla_kv_norm_g, mla_w_ukv, w_branch,
                            w_out, mlp_w1, mlp_w2)
        mods8 = mods[l]
        mods2 = mods8[:2, None, :]
        h = norm_modulate(xs, norm_mix_g[l], mods2, 0)
        u3 = matmul(h, w_main, f32, 1024, layer=l, col0=0, ncols=3 * aw)
        ub = matmul(h, lw["w_b"], f32, lw["w_b"].shape[1])
        uc = matmul(h, w_main, bf16, 1024, layer=l, col0=3 * aw, ncols=3 * aw)
        gates = matmul(h, w_main, bf16, 1024, sigmoid=True, layer=l, col0=6 * aw, ncols=N_BRANCH * d)

        r, v, kk, bonus, g, lwd, kd, bd = rwkv_prepare(u3, ub, lw["rwkv"], n_ctx)
        y = rwkv_chunk_scan(r, v, kk, lwd, kd, bd, n_ctx)
        ya = rwkv_readout(y, bonus, g, rwkv_ln_g[l], rwkv_ln_b[l], lw["rwkv"]["seg"][:LANES, :LANES])

        q_b, k_b, v_b = mla_prepare(ub, cs, lw["mla"], lw["cols"])
        yb = mla_attention(q_b, k_b, v_b, n_ctx)

        yc = na_attention(uc, _na_bias_tables(na_rpb[l]), n_ctx)

        merged = merge_branches(ya, yb, yc, gates, lw["w_branch"])
        xs = matmul_gated_residual(merged, lw["w_out"], xs, mods8, 2, n_ctx)
        h2 = norm_modulate(xs, norm_mlp_g[l], mods2, 3)
        xs = mlp_gated_residual(h2, lw["mlp_w1"], lw["mlp_w2"], xs, mods8, 5, n_ctx)

    return final_norm(xs, final_norm_g, n_ctx_tiles)[None]
```

```python
import functools
import math

import jax
import jax.numpy as jnp
import numpy as np
from jax import lax
from jax.experimental import pallas as pl
from jax.experimental.pallas import tpu as pltpu

f32 = jnp.float32
bf16 = jnp.bfloat16

GRID_W = 64
RMS_EPS = 1e-6
N_BRANCH = 3
A_HEAD_DIM = 64
LORA_W = 96
LORA_A = 96
LORA_G = 256
GN_EPS = 64e-5
QK_NOPE = 128
QK_ROPE = 64
V_DIM = 128
Q_LORA = 448
KV_LORA = 128
MLA_SCALE = (QK_NOPE + QK_ROPE) ** -0.5
MLA_Q_SCALE = MLA_SCALE * math.log2(math.e)
ROPE_BASE = 10000.0
C_HEAD_DIM = 64
WIN_ROWS = 8
WIN_COLS = 16
NA_SCALE = C_HEAD_DIM ** -0.5
DECAY_SCALE = math.exp(-0.5)

LANES = 128
ROW_TILE = 256
CHUNK = 64
CHUNKS_PER_STEP = 2
PREP_TILE = 128
NA_PAIRS_PER_STEP = 2
MLA_CHAIN_LAG = 2
MLA_HEADS_PER_STEP = 2
VMEM_LIMIT = 48 * 1024 * 1024
NEG_BIG = -1e30


def _params(sem):
    return pltpu.CompilerParams(dimension_semantics=sem, vmem_limit_bytes=VMEM_LIMIT)


def _dot(a, b):
    return jnp.dot(a, b, preferred_element_type=f32)


def _dot_nt(a, b):
    return lax.dot_general(a, b, (((1,), (1,)), ((), ())), preferred_element_type=f32)


def _ada_body(s_ref, w_ref, b_ref, o_ref):
    s = s_ref[...]
    s = s * jax.nn.sigmoid(s)
    o_ref[...] = _dot(s, w_ref[...]) + b_ref[...]


def ada_modulation(cond, ada_w, ada_b):
    nl, d, n6 = ada_w.shape
    tn = 1024
    return pl.pallas_call(
        _ada_body,
        grid=(nl, n6 // tn),
        in_specs=[
            pl.BlockSpec((8, d), lambda l, j: (0, 0)),
            pl.BlockSpec((None, d, tn), lambda l, j: (l, 0, j)),
            pl.BlockSpec((None, 1, tn), lambda l, j: (l, 0, j)),
        ],
        out_specs=pl.BlockSpec((None, 8, tn), lambda l, j: (l, 0, j)),
        out_shape=jax.ShapeDtypeStruct((nl, 8, n6), f32),
        compiler_params=_params(("arbitrary", "arbitrary")),
        name="ada_modulation",
    )(cond, ada_w, ada_b.reshape(nl, 1, n6))


def _norm_mod_body(x_ref, g_ref, m_ref, o_ref, *, off, d):
    x = x_ref[...]
    y = x * lax.rsqrt(jnp.mean(x * x, axis=-1, keepdims=True) + RMS_EPS) * g_ref[...]
    shift = m_ref[:, off * d:(off + 1) * d]
    scale = m_ref[:, (off + 1) * d:(off + 2) * d]
    o_ref[...] = (y * (1.0 + scale) + shift).astype(o_ref.dtype)


def norm_modulate(x, g, mods2, off):
    n, d = x.shape
    return pl.pallas_call(
        functools.partial(_norm_mod_body, off=off, d=d),
        grid=(n // ROW_TILE,),
        in_specs=[
            pl.BlockSpec((ROW_TILE, d), lambda i: (i, 0)),
            pl.BlockSpec((1, d), lambda i: (0, 0)),
            pl.BlockSpec((None, 1, mods2.shape[-1]), lambda i: (jnp.where(i == 0, 1, 0), 0, 0)),
        ],
        out_specs=pl.BlockSpec((ROW_TILE, d), lambda i: (i, 0)),
        out_shape=jax.ShapeDtypeStruct((n, d), bf16),
        compiler_params=_params(("arbitrary",)),
        name="norm_modulate",
    )(x, g.reshape(1, d), mods2)


def _final_norm_body(x_ref, g_ref, o_ref):
    x = x_ref[...]
    o_ref[...] = x * lax.rsqrt(jnp.mean(x * x, axis=-1, keepdims=True) + RMS_EPS) * g_ref[...]


def final_norm(x, g, n_ctx_tiles):
    n, d = x.shape
    t = n - n_ctx_tiles * ROW_TILE
    return pl.pallas_call(
        _final_norm_body,
        grid=(t // ROW_TILE,),
        in_specs=[
            pl.BlockSpec((ROW_TILE, d), lambda i: (i + n_ctx_tiles, 0)),
            pl.BlockSpec((1, d), lambda i: (0, 0)),
        ],
        out_specs=pl.BlockSpec((ROW_TILE, d), lambda i: (i, 0)),
        out_shape=jax.ShapeDtypeStruct((t, d), f32),
        compiler_params=_params(("arbitrary",)),
        name="final_norm",
    )(x, g.reshape(1, d))


def _mm_body(a_ref, w_ref, o_ref, *, sigmoid):
    acc = _dot(a_ref[...], w_ref[...])
    o_ref[...] = (jax.nn.sigmoid(acc) if sigmoid else acc).astype(o_ref.dtype)


def _row_tile(n):
    for tm in (1056, 768, 512, 384, 256):
        if n % tm == 0:
            return tm
    raise ValueError(f"no row tile for {n} rows")


def matmul(a, w, out_dtype, tn, sigmoid=False, layer=None, col0=0, ncols=None):
    m, k = a.shape
    n = (w.shape[-1] - col0) if ncols is None else ncols
    tm = _row_tile(m)
    assert col0 % tn == 0 and n % tn == 0
    j0 = col0 // tn
    if layer is None:
        w_spec = pl.BlockSpec((k, tn), lambda i, j: (0, j0 + j))
    else:
        w_spec = pl.BlockSpec((None, k, tn), lambda i, j: (layer, 0, j0 + j))
    return pl.pallas_call(
        functools.partial(_mm_body, sigmoid=sigmoid),
        grid=(m // tm, n // tn),
        in_specs=[pl.BlockSpec((tm, k), lambda i, j: (i, 0)), w_spec],
        out_specs=pl.BlockSpec((tm, tn), lambda i, j: (i, j)),
        out_shape=jax.ShapeDtypeStruct((m, n), out_dtype),
        compiler_params=_params(("arbitrary", "arbitrary")),
        name="matmul",
    )(a, w)


def _cast_body(x_ref, o_ref):
    o_ref[...] = x_ref[...].astype(o_ref.dtype)


def cast_w_in_groups(w_in, mid_start, mid_cols):
    nl, d, c = w_in.shape
    tc = 2 * LANES
    assert mid_start % tc == 0 and mid_cols % tc == 0 and c % tc == 0
    j_mid, n_mid = mid_start // tc, mid_cols // tc

    def cast(n_blocks, col_block):
        return pl.pallas_call(
            _cast_body,
            grid=(nl, n_blocks),
            in_specs=[pl.BlockSpec((None, d, tc), lambda l, j: (l, 0, col_block(j)))],
            out_specs=pl.BlockSpec((None, d, tc), lambda l, j: (l, 0, j)),
            out_shape=jax.ShapeDtypeStruct((nl, d, n_blocks * tc), bf16),
            compiler_params=_params(("arbitrary", "arbitrary")),
            name="cast_w_in_groups",
        )(w_in)

    return (cast(c // tc - n_mid, lambda j: jnp.where(j < j_mid, j, j + n_mid)),
            cast(n_mid, lambda j: j_mid + j))


def _gate_rows(m_ref, tm, n_ctx):
    rows = pl.program_id(0) * tm + lax.broadcasted_iota(jnp.int32, (tm, 1), 0)
    return jnp.where(rows < n_ctx, m_ref[1:2, :], m_ref[0:1, :])


def _mm_res_body(a_ref, w_ref, x_ref, m_ref, o_ref, *, tm, n_ctx):
    acc = _dot(a_ref[...], w_ref[...])
    o_ref[...] = x_ref[...] + _gate_rows(m_ref, tm, n_ctx) * acc


def matmul_gated_residual(a, w, x, mods8, gate_off, n_ctx, tn=512):
    m, k = a.shape
    n = w.shape[1]
    tm = _row_tile(m)
    nj = n // tn
    return pl.pallas_call(
        functools.partial(_mm_res_body, tm=tm, n_ctx=n_ctx),
        grid=(m // tm, nj),
        in_specs=[
            pl.BlockSpec((tm, k), lambda i, j: (i, 0)),
            pl.BlockSpec((k, tn), lambda i, j: (0, j)),
            pl.BlockSpec((tm, tn), lambda i, j: (i, j)),
            pl.BlockSpec((8, tn), lambda i, j: (0, gate_off * nj + j)),
        ],
        out_specs=pl.BlockSpec((tm, tn), lambda i, j: (i, j)),
        out_shape=jax.ShapeDtypeStruct((m, n), f32),
        compiler_params=_params(("arbitrary", "arbitrary")),
        name="matmul_gated_residual",
    )(a, w, x, mods8)


def _merge_body(ya_ref, yb_ref, yc_ref, g0_ref, g1_ref, g2_ref, w_ref, o_ref):
    acc = g0_ref[...] * _dot(ya_ref[...], w_ref[0])
    acc = acc + g1_ref[...] * _dot(yb_ref[...], w_ref[1])
    acc = acc + g2_ref[...] * _dot(yc_ref[...], w_ref[2])
    o_ref[...] = acc.astype(o_ref.dtype)


def merge_branches(ya, yb, yc, gates, w_branch, tn=1024):
    m, k = ya.shape
    d = w_branch.shape[-1]
    tm = 528 if m % 528 == 0 else ROW_TILE
    nj = d // tn
    y_spec = pl.BlockSpec((tm, k), lambda i, j: (i, 0))
    return pl.pallas_call(
        _merge_body,
        grid=(m // tm, nj),
        in_specs=[
            y_spec, y_spec, y_spec,
            pl.BlockSpec((tm, tn), lambda i, j: (i, j)),
            pl.BlockSpec((tm, tn), lambda i, j: (i, nj + j)),
            pl.BlockSpec((tm, tn), lambda i, j: (i, 2 * nj + j)),
            pl.BlockSpec((N_BRANCH, k, tn), lambda i, j: (0, 0, j)),
        ],
        out_specs=pl.BlockSpec((tm, tn), lambda i, j: (i, j)),
        out_shape=jax.ShapeDtypeStruct((m, d), bf16),
        compiler_params=_params(("arbitrary", "arbitrary")),
        name="merge_branches",
    )(ya, yb, yc, gates, gates, gates, w_branch)


def _mlp_body(h_ref, w1_ref, w2_ref, x_ref, m_ref, o_ref, acc_ref, *, tm, n_ctx):
    f = pl.program_id(1)

    @pl.when(f == 0)
    def _():
        acc_ref[...] = jnp.zeros_like(acc_ref)

    h1 = jnp.maximum(_dot(h_ref[...], w1_ref[...]), 0.0)
    acc_ref[...] += _dot((h1 * h1).astype(bf16), w2_ref[...])

    @pl.when(f == pl.num_programs(1) - 1)
    def _():
        o_ref[...] = x_ref[...] + _gate_rows(m_ref, tm, n_ctx) * acc_ref[...]


def mlp_gated_residual(h, w1, w2, x, mods8, gate_off, n_ctx, tf=1024):
    m, d = h.shape
    dff = w1.shape[1]
    tm = 528 if m % 528 == 0 else ROW_TILE
    return pl.pallas_call(
        functools.partial(_mlp_body, tm=tm, n_ctx=n_ctx),
        grid=(m // tm, dff // tf),
        in_specs=[
            pl.BlockSpec((tm, d), lambda i, f: (i, 0)),
            pl.BlockSpec((d, tf), lambda i, f: (0, f)),
            pl.BlockSpec((tf, d), lambda i, f: (f, 0)),
            pl.BlockSpec((tm, d), lambda i, f: (i, 0)),
            pl.BlockSpec((8, d), lambda i, f: (0, gate_off)),
        ],
        out_specs=pl.BlockSpec((tm, d), lambda i, f: (i, 0)),
        out_shape=jax.ShapeDtypeStruct((m, d), f32),
        scratch_shapes=[pltpu.VMEM((tm, d), f32)],
        compiler_params=_params(("arbitrary", "arbitrary")),
        name="mlp_gated_residual",
    )(h, w1, w2, x, mods8)


def _seg_sum(x, e_ref):
    hi = x.astype(bf16)
    lo = (x - hi.astype(f32)).astype(bf16)
    e = e_ref[...]
    w = e.shape[0]
    parts = [_dot(hi[:, g:g + w], e) + _dot(lo[:, g:g + w], e) for g in range(0, x.shape[1], w)]
    return parts[0] if len(parts) == 1 else jnp.concatenate(parts, axis=1)


def _to_pairs(o_ref, val, lead=None):
    for p in range(val.shape[-1] // LANES):
        piece = val[:, p * LANES:(p + 1) * LANES]
        if lead is None:
            o_ref[p] = piece
        else:
            o_ref[lead, p] = piece


def _rwkv_prep_body(u_ref, up_ref, un_ref, l_ref, lp_ref, ln_ref, cw_ref, cwl_ref, wu_ref, au_ref, gu_ref,
                    w0_ref, a0_ref, kk_ref, ka_ref, rk_ref, e_ref,
                    r_o, v_o, kkn_o, bonus_o, g_o, lw_o, kd_o, bd_o, *, n_ctx_tiles, aw):
    i = pl.program_id(0)
    last = pl.num_programs(0) - 1
    left_zero = (i == 0) | (i == n_ctx_tiles)
    right_zero = (i == n_ctx_tiles - 1) | (i == last)
    rows = lax.broadcasted_iota(jnp.int32, (PREP_TILE, 1), 0)

    def conv(x_ref, xp_ref, xn_ref, w_ref):
        x = x_ref[...]
        prev_row = jnp.where(left_zero, 0.0, xp_ref[7:8, :])
        next_row = jnp.where(right_zero, 0.0, xn_ref[0:1, :])
        x_prev = jnp.where(rows == 0, prev_row, pltpu.roll(x, 1, 0))
        x_next = jnp.where(rows == PREP_TILE - 1, next_row, pltpu.roll(x, PREP_TILE - 1, 0))
        return w_ref[0:1, :] * x_prev + w_ref[1:2, :] * x + w_ref[2:3, :] * x_next

    y = conv(u_ref, up_ref, un_ref, cw_ref)
    r, k, v = y[:, :aw], y[:, aw:2 * aw], y[:, 2 * aw:]
    yl = conv(l_ref, lp_ref, ln_ref, cwl_ref)
    th = jnp.tanh(yl).astype(bf16)
    sg = jax.nn.sigmoid(yl).astype(bf16)
    ylb = yl.astype(bf16)

    _to_pairs(g_o, _dot(sg, gu_ref[...]))
    _to_pairs(r_o, r)
    _to_pairs(v_o, v.astype(bf16))

    kkr = k * kk_ref[...]
    norm = jnp.sqrt(_seg_sum(kkr * kkr, e_ref))
    kkn = kkr / jnp.maximum(norm, 1e-12)
    _to_pairs(kkn_o, kkn)

    ksum = None
    for dr in range(2):
        z = w0_ref[dr:dr + 1, :] + _dot(th, wu_ref[dr])
        _to_pairs(lw_o, -DECAY_SCALE * jax.nn.sigmoid(z), lead=dr)
        a = jax.nn.sigmoid(a0_ref[dr:dr + 1, :] + _dot(ylb, au_ref[dr]))
        kd = k * (1.0 + (a - 1.0) * ka_ref[...])
        _to_pairs(kd_o, kd, lead=dr)
        _to_pairs(bd_o, kkn * a, lead=dr)
        ksum = kd if ksum is None else ksum + kd
    rk = _seg_sum(r * (0.5 * ksum) * rk_ref[...], e_ref)
    _to_pairs(bonus_o, rk * v)


def rwkv_prepare(u3, ub, pw, n_ctx):
    n = u3.shape[0]
    aw = u3.shape[1] // 3
    npair = aw // LANES
    lw = pw["conv_l"].shape[1]
    tpb = PREP_TILE // 8
    nb8 = n // 8

    def prev_map(i):
        return (jnp.maximum(i * tpb - 1, 0), 0)

    def next_map(i):
        return (jnp.minimum((i + 1) * tpb, nb8 - 1), 0)

    full = lambda a: pl.BlockSpec(a.shape, lambda i: (0,) * a.ndim)
    pm = jax.ShapeDtypeStruct((npair, n, LANES), f32)
    pm2 = jax.ShapeDtypeStruct((2, npair, n, LANES), f32)
    pm_bf16 = jax.ShapeDtypeStruct((npair, n, LANES), bf16)
    pm_spec = pl.BlockSpec((npair, PREP_TILE, LANES), lambda i: (0, i, 0))
    pm2_spec = pl.BlockSpec((2, npair, PREP_TILE, LANES), lambda i: (0, 0, i, 0))
    consts = [pw["conv_rkv"], pw["conv_l"], pw["w_up"], pw["a_up"], pw["g_up"], pw["w0"], pw["a0"],
              pw["k_k"], pw["k_a"], pw["r_k"], pw["seg"]]
    return pl.pallas_call(
        functools.partial(_rwkv_prep_body, n_ctx_tiles=n_ctx // PREP_TILE, aw=aw),
        grid=(n // PREP_TILE,),
        in_specs=[
            pl.BlockSpec((PREP_TILE, 3 * aw), lambda i: (i, 0)),
            pl.BlockSpec((8, 3 * aw), prev_map),
            pl.BlockSpec((8, 3 * aw), next_map),
            pl.BlockSpec((PREP_TILE, lw), lambda i: (i, 0)),
            pl.BlockSpec((8, lw), prev_map),
            pl.BlockSpec((8, lw), next_map),
        ] + [full(a) for a in consts],
        out_specs=[pm_spec] * 5 + [pm2_spec] * 3,
        out_shape=[pm, pm_bf16] + [pm] * 3 + [pm2] * 3,
        compiler_params=_params(("arbitrary",)),
        name="rwkv_prepare",
    )(u3, u3, u3, ub, ub, ub, *consts)


def _dot_tn(a, b):
    return lax.dot_general(a, b, (((0,), (0,)), ((), ())), preferred_element_type=f32)


def _rwkv_chunk_body(r_ref, v_ref, kk_ref, lw_ref, k_ref, b_ref, y_ref, h_ref, *, n_pairs):
    d = pl.program_id(0)
    j = pl.program_id(1)
    c = CHUNK
    c2 = 2 * c

    @pl.when(j == 0)
    def _():
        h_ref[...] = jnp.zeros_like(h_ref)

    sign = jnp.where(d == 0, 1, -1)
    ti = lax.broadcasted_iota(jnp.int32, (c, c), 0)
    si = lax.broadcasted_iota(jnp.int32, (c, c), 1)
    incl_c = jnp.where((ti - si) * sign >= 0, 1.0, 0.0).astype(bf16)
    t2 = lax.broadcasted_iota(jnp.int32, (c2, c2), 0)
    s2 = lax.broadcasted_iota(jnp.int32, (c2, c2), 1)
    ahead = (t2 - s2) * sign
    blk = lambda n: (t2 // n) == (s2 // n)
    incl = blk(c) & (ahead >= 0)
    strict = blk(c) & (ahead > 0)
    eye = jnp.where(t2 == s2, 1.0, 0.0)
    inv_levels = []
    n = 2
    while n < c:
        inv_levels.append(strict & blk(2 * n) & ~blk(n))
        n *= 2
    low = lax.broadcasted_iota(jnp.int32, (1, LANES), 1) < A_HEAD_DIM

    def stack(x):
        return jnp.concatenate([jnp.where(low, x, 0.0), jnp.where(low, 0.0, x)], axis=0).astype(bf16)

    pairs = range(n_pairs)
    first = jnp.where(d == 0, 0, CHUNKS_PER_STEP - 1)
    offs = [pl.multiple_of((first + sign * sc) * c, c) for sc in range(CHUNKS_PER_STEP)]
    units = [(sc, p) for sc in range(CHUNKS_PER_STEP) for p in pairs]
    each = lambda f: [f(sc, p) for sc, p in units]
    at = lambda sc, p: sc * n_pairs + p
    ld = lambda ref, sc, p: ref[p, pl.ds(offs[sc], c), :]
    lw = each(lambda sc, p: ld(lw_ref, sc, p))
    lw_hi = [x.astype(bf16) for x in lw]
    lw_lo = [(x - hi.astype(f32)).astype(bf16) for x, hi in zip(lw, lw_hi)]
    cum = [_dot(incl_c, hi) + _dot(incl_c, lo) for hi, lo in zip(lw_hi, lw_lo)]
    tot = [jnp.sum(x, axis=0, keepdims=True) for x in lw]
    e_neg = [jnp.exp(-x) for x in cum]
    e_end = [jnp.exp(t - x) for t, x in zip(tot, cum)]
    r_s = each(lambda sc, p: stack(ld(r_ref, sc, p) * jnp.exp(cum[at(sc, p)])))
    kk_s = each(lambda sc, p: stack(ld(kk_ref, sc, p) * jnp.exp(cum[at(sc, p)] - lw[at(sc, p)])))
    kb_s = each(lambda sc, p: jnp.concatenate([stack(ld(b_ref, sc, p) * e_neg[at(sc, p)]),
                                               stack(ld(k_ref, sc, p) * e_neg[at(sc, p)])], axis=0))
    kbd_s = each(lambda sc, p: jnp.concatenate([stack(ld(k_ref, sc, p) * e_end[at(sc, p)]),
                                                stack(ld(b_ref, sc, p) * e_end[at(sc, p)])], axis=0))
    v_s = each(lambda sc, p: stack(ld(v_ref, sc, p)))
    a1 = [_dot_nt(a, b) for a, b in zip(kk_s, kb_s)]
    a2 = [_dot_nt(a, b) for a, b in zip(r_s, kb_s)]
    a_kb = [jnp.where(strict, a[:, :c2], 0.0) for a in a1]
    a_kk = [jnp.where(strict, a[:, c2:], 0.0).astype(bf16) for a in a1]
    a_rb = [jnp.where(incl, a[:, :c2], 0.0).astype(bf16) for a in a2]
    a_rk = [jnp.where(incl, a[:, c2:], 0.0).astype(bf16) for a in a2]
    t_inv = [eye - jnp.where(blk(2), a, 0.0) for a in a_kb]
    for m in inv_levels:
        t_b = [t.astype(bf16) for t in t_inv]
        x = [_dot(t, jnp.where(m, a, 0.0).astype(bf16)).astype(bf16) for t, a in zip(t_b, a_kb)]
        t_inv = [t - _dot(xx, tb) for t, xx, tb in zip(t_inv, x, t_b)]
    g = [_dot(t.astype(bf16), jnp.concatenate([kk, akk], axis=1)).astype(bf16)
         for t, kk, akk in zip(t_inv, kk_s, a_kk)]
    w_col = [jnp.sum(eye * jnp.exp(t), axis=1, keepdims=True) for t in tot]
    h = [h_ref[p] for p in pairs]
    for sc in range(CHUNKS_PER_STEP):
        ix = [at(sc, p) for p in pairs]
        h_b = [x.astype(bf16) for x in h]
        u = [_dot(g[i], jnp.concatenate([h_b[p], v_s[i]], axis=0)).astype(bf16) for p, i in enumerate(ix)]
        y = [_dot(jnp.concatenate([r_s[i], a_rk[i], -a_rb[i]], axis=1), jnp.concatenate([h_b[p], v_s[i], u[p]], axis=0))
             for p, i in enumerate(ix)]
        h1 = [_dot_tn(kbd_s[i], jnp.concatenate([v_s[i], -u[p]], axis=0)) for p, i in enumerate(ix)]
        for p in pairs:
            y_ref[p, pl.ds(offs[sc], c), :] = y[p][:c] + y[p][c:]
        h = [h[p] * w_col[i] + h1[p] for p, i in enumerate(ix)]
    for p in pairs:
        h_ref[p] = h[p]


def rwkv_chunk_scan(r, v, kk, lw, kd, bd, n_ctx):
    npair, n, _ = r.shape
    blk_rows = CHUNKS_PER_STEP * CHUNK
    nc = n // blk_rows
    ncc = n_ctx // blk_rows

    def cidx(d, j):
        bwd = jnp.where(j < ncc, ncc - 1 - j, nc - 1 - (j - ncc))
        return jnp.where(d == 0, j, bwd)

    shared = pl.BlockSpec((npair, blk_rows, LANES), lambda d, j: (0, cidx(d, j), 0))
    per_dir = pl.BlockSpec((None, npair, blk_rows, LANES), lambda d, j: (d, 0, cidx(d, j), 0))
    return pl.pallas_call(
        functools.partial(_rwkv_chunk_body, n_pairs=npair),
        grid=(2, nc),
        in_specs=[shared, shared, shared, per_dir, per_dir, per_dir],
        out_specs=per_dir,
        out_shape=jax.ShapeDtypeStruct((2, npair, n, LANES), f32),
        scratch_shapes=[pltpu.VMEM((npair, LANES, LANES), f32)],
        compiler_params=_params(("arbitrary", "arbitrary")),
        name="rwkv_chunk_scan",
    )(r, v, kk, lw, kd, bd)


def _rwkv_readout_body(y_ref, bonus_ref, g_ref, lng_ref, lnb_ref, e_ref, o_ref, *, n_pairs):
    inv = 1.0 / A_HEAD_DIM
    for p in range(n_pairs):
        y = y_ref[0, p] + y_ref[1, p]
        yc = y - _seg_sum(y, e_ref) * inv
        var = _seg_sum(yc * yc, e_ref) * inv
        yn = yc * lax.rsqrt(var + GN_EPS) * lng_ref[p] + lnb_ref[p]
        o_ref[:, p * LANES:(p + 1) * LANES] = ((yn + bonus_ref[p]) * g_ref[p]).astype(o_ref.dtype)


def rwkv_readout(y, bonus, g, ln_g, ln_b, seg_pair):
    _, npair, n, _ = y.shape
    pm_spec = pl.BlockSpec((npair, ROW_TILE, LANES), lambda i: (0, i, 0))
    vec_spec = pl.BlockSpec((npair, 1, LANES), lambda i: (0, 0, 0))
    return pl.pallas_call(
        functools.partial(_rwkv_readout_body, n_pairs=npair),
        grid=(n // ROW_TILE,),
        in_specs=[pl.BlockSpec((2, npair, ROW_TILE, LANES), lambda i: (0, 0, i, 0)), pm_spec, pm_spec,
                  vec_spec, vec_spec, pl.BlockSpec((LANES, LANES), lambda i: (0, 0))],
        out_specs=pl.BlockSpec((ROW_TILE, npair * LANES), lambda i: (i, 0)),
        out_shape=jax.ShapeDtypeStruct((n, npair * LANES), bf16),
        compiler_params=_params(("arbitrary",)),
        name="rwkv_readout",
    )(y, bonus, g, ln_g.reshape(npair, 1, LANES), ln_b.reshape(npair, 1, LANES), seg_pair)


def _mla_prep_body(u_ref, cs_ref, gq_ref, gkv_ref, wqn_ref, wqp_ref, wqr_ref, wkv_ref, q_o, k_o, v_o,
                   *, n_heads, c_kvd, c_q, c_kr):
    ql = u_ref[:, c_q:c_q + 512]
    qn = ql * lax.rsqrt(jnp.sum(ql * ql, axis=-1, keepdims=True) * (1.0 / Q_LORA) + RMS_EPS) * gq_ref[...]
    qn = qn.astype(bf16)
    kvd = u_ref[:, c_kvd:c_kvd + KV_LORA]
    kvn = kvd * lax.rsqrt(jnp.mean(kvd * kvd, axis=-1, keepdims=True) + RMS_EPS) * gkv_ref[...]
    kv = _dot(kvn.astype(bf16), wkv_ref[...])
    cs = cs_ref[...]
    lane = lax.broadcasted_iota(jnp.int32, (1, LANES), 1)
    low = lane < QK_ROPE
    cos_t = jnp.where(low, cs, 0.0)
    sin_t = jnp.where(low, pltpu.roll(cs, QK_ROPE, 1), 0.0)
    kr = u_ref[:, c_kr:c_kr + LANES]
    prod = kr * cs
    k_pe = jnp.where(low, prod + pltpu.roll(prod, QK_ROPE, 1), 0.0).astype(bf16)
    q_nope = _dot(qn, wqn_ref[...])
    q_pe = _dot(qn, wqp_ref[...])
    q_pr = _dot(qn, wqr_ref[...])
    for h in range(n_heads):
        sl = slice(h * LANES, (h + 1) * LANES)
        q_o[h, :, 0:LANES] = (q_nope[:, sl] * MLA_Q_SCALE).astype(bf16)
        q_o[h, :, LANES:2 * LANES] = ((q_pe[:, sl] * cos_t + q_pr[:, sl] * sin_t) * MLA_Q_SCALE).astype(bf16)
        k_o[h, :, 0:LANES] = kv[:, sl].astype(bf16)
        k_o[h, :, LANES:2 * LANES] = k_pe
        v_o[h] = kv[:, n_heads * LANES + h * LANES:n_heads * LANES + (h + 1) * LANES].astype(bf16)


def mla_prepare(ub, cs, pw, cols):
    n = ub.shape[0]
    nh = pw["wq_nope"].shape[1] // LANES
    full = lambda a: pl.BlockSpec(a.shape, lambda i: (0,) * a.ndim)
    consts = [pw["gq"], pw["gkv"], pw["wq_nope"], pw["wq_pe"], pw["wq_pr"], pw["wkv"]]
    return pl.pallas_call(
        functools.partial(_mla_prep_body, n_heads=nh, **cols),
        grid=(n // ROW_TILE,),
        in_specs=[pl.BlockSpec((ROW_TILE, ub.shape[1]), lambda i: (i, 0)),
                  pl.BlockSpec((ROW_TILE, LANES), lambda i: (i, 0))] + [full(a) for a in consts],
        out_specs=[pl.BlockSpec((nh, ROW_TILE, 2 * LANES), lambda i: (0, i, 0)),
                   pl.BlockSpec((nh, ROW_TILE, 2 * LANES), lambda i: (0, i, 0)),
                   pl.BlockSpec((nh, ROW_TILE, LANES), lambda i: (0, i, 0))],
        out_shape=[jax.ShapeDtypeStruct((nh, n, 2 * LANES), bf16),
                   jax.ShapeDtypeStruct((nh, n, 2 * LANES), bf16),
                   jax.ShapeDtypeStruct((nh, n, LANES), bf16)],
        compiler_params=_params(("arbitrary",)),
        name="mla_prepare",
    )(ub, cs, *consts)


def _mla_attn_body(q_ref, k_ref, v_ref, o_ref, *, n_ctx, n_tok, tk, tq):
    i = pl.program_id(1)
    hs = range(MLA_HEADS_PER_STEP)
    subs = range(tq // ROW_TILE)

    def rows(sb):
        return slice(sb * ROW_TILE, (sb + 1) * ROW_TILE)

    def write(chains, o):
        for (h, sb), val in zip(chains, o):
            o_ref[rows(sb), h * V_DIM:(h + 1) * V_DIM] = val.astype(o_ref.dtype)

    def context_queries():
        chains = [(h, 0) for h in hs]
        each = lambda f: [f(n, h) for n, (h, _) in enumerate(chains)]
        s = each(lambda n, h: _dot_nt(q_ref[h, rows(0), :], k_ref[h, 0:n_ctx, :]))
        p = each(lambda n, h: jnp.exp2(s[n] - jnp.max(s[n], axis=-1, keepdims=True)))
        pv = each(lambda n, h: _dot(p[n].astype(bf16), v_ref[h, 0:n_ctx, :]))
        write(chains, each(lambda n, h: pv[n] / jnp.sum(p[n], axis=-1, keepdims=True)))

    def latent_queries(chains):
        each = lambda f: [f(n, h, sb) for n, (h, sb) in enumerate(chains)]
        q = each(lambda n, h, sb: q_ref[h, rows(sb), :])

        def step(c, carry):
            off = pl.multiple_of(c * tk, tk)
            nch = len(chains)
            s, out = [None] * nch, [None] * nch

            def finish(n):
                h = chains[n][0]
                m_old, l_old, acc_old = carry[n]
                m_new = jnp.maximum(m_old, jnp.max(s[n], axis=-1, keepdims=True))
                alpha = jnp.exp2(m_old - m_new)
                p = jnp.exp2(s[n] - m_new)
                l_new = alpha * l_old + jnp.sum(p, axis=-1, keepdims=True)
                pv = _dot(p.astype(bf16), v_ref[h, pl.ds(off, tk), :])
                out[n] = (m_new, l_new, alpha * acc_old + pv)

            for n in range(nch + MLA_CHAIN_LAG):
                if n < nch:
                    s[n] = _dot_nt(q[n], k_ref[chains[n][0], pl.ds(off, tk), :])
                if n >= MLA_CHAIN_LAG:
                    finish(n - MLA_CHAIN_LAG)
            return tuple(out)

        init = tuple((jnp.full((ROW_TILE, 1), NEG_BIG, f32), jnp.zeros((ROW_TILE, 1), f32),
                      jnp.zeros((ROW_TILE, V_DIM), f32)) for _ in chains)
        fin = lax.fori_loop(0, n_tok // tk, step, init)
        write(chains, [f[2] / f[1] for f in fin])

    @pl.when(i == 0)
    def _():
        context_queries()
        if len(subs) > 1:
            latent_queries([(h, sb) for h in hs for sb in subs[1:]])

    @pl.when(i > 0)
    def _():
        latent_queries([(h, sb) for h in hs for sb in subs])


def mla_attention(q, k, v, n_ctx):
    nh, n, dq = q.shape
    hps = MLA_HEADS_PER_STEP
    assert n_ctx == ROW_TILE
    tk = next(t for t in (768, 512, 256) if n % t == 0)
    tq = tk
    return pl.pallas_call(
        functools.partial(_mla_attn_body, n_ctx=n_ctx, n_tok=n, tk=tk, tq=tq),
        grid=(nh // hps, n // tq),
        in_specs=[
            pl.BlockSpec((hps, tq, dq), lambda h, i: (h, i, 0)),
            pl.BlockSpec((hps, n, dq), lambda h, i: (h, 0, 0)),
            pl.BlockSpec((hps, n, V_DIM), lambda h, i: (h, 0, 0)),
        ],
        out_specs=pl.BlockSpec((tq, hps * V_DIM), lambda h, i: (i, h)),
        out_shape=jax.ShapeDtypeStruct((n, nh * V_DIM), bf16),
        compiler_params=_params(("arbitrary", "arbitrary")),
        name="mla_attention",
    )(q, k, v)


def _na_body(q_ref, k_ref, v_ref, bias_ref, o_ref, *, n_ctx, n_rows):
    j = pl.program_id(1)
    lane = lax.broadcasted_iota(jnp.int32, (1, LANES), 1)
    low = lane < C_HEAD_DIM
    win = WIN_ROWS * GRID_W
    lanes = lambda lp: slice(lp * LANES, (lp + 1) * LANES)
    pairs = range(NA_PAIRS_PER_STEP)
    kc = [k_ref[0:n_ctx, lanes(lp)] for lp in pairs]
    vc = [v_ref[0:n_ctx, lanes(lp)] for lp in pairs]

    def head_mask(hh, x):
        return jnp.where(low if hh == 0 else ~low, x, jnp.zeros_like(x))

    @pl.when(j == 0)
    def _():
        units = [(lp, hh) for lp in pairs for hh in range(2)]
        each = lambda f: [f(n, lp, hh) for n, (lp, hh) in enumerate(units)]
        qm = each(lambda n, lp, hh: head_mask(hh, q_ref[:, lanes(lp)] * NA_SCALE))
        s = each(lambda n, lp, hh: _dot_nt(qm[n], kc[lp]))
        p = each(lambda n, lp, hh: jnp.exp(s[n] - jnp.max(s[n], axis=-1, keepdims=True)))
        o = each(lambda n, lp, hh: _dot(p[n].astype(bf16), vc[lp]) / jnp.sum(p[n], axis=-1, keepdims=True))
        for lp in pairs:
            o_ref[:, lanes(lp)] = jnp.where(low, o[2 * lp], o[2 * lp + 1]).astype(o_ref.dtype)

    @pl.when(j > 0)
    def _():
        grid_rows = range(ROW_TILE // GRID_W)
        units = [(il, lp, hh) for il in grid_rows for lp in pairs for hh in range(2)]
        per_row = lambda f: [f(il) for il in grid_rows]
        per_rp = lambda f: [[f(il, lp) for lp in pairs] for il in grid_rows]
        per_unit = lambda f: [f(n, il, lp, hh) for n, (il, lp, hh) in enumerate(units)]
        i = per_row(lambda il: (j - 1) * (ROW_TILE // GRID_W) + il)
        r0 = per_row(lambda il: jnp.clip(i[il] - WIN_ROWS // 2, 0, n_rows - WIN_ROWS))
        d0 = per_row(lambda il: r0[il] - i[il] + WIN_ROWS - 1)
        off = per_row(lambda il: pl.multiple_of(n_ctx + r0[il] * GRID_W, GRID_W))
        q2 = per_rp(lambda il, lp: q_ref[il * GRID_W:(il + 1) * GRID_W, lanes(lp)] * NA_SCALE)
        kw = per_rp(lambda il, lp: k_ref[pl.ds(off[il], win), lanes(lp)])
        vw = per_rp(lambda il, lp: v_ref[pl.ds(off[il], win), lanes(lp)])
        qm = per_unit(lambda n, il, lp, hh: head_mask(hh, q2[il][lp]))
        bias = lambda hd, d: jnp.concatenate([bias_ref[hd, d + 2 * jj] for jj in range(WIN_ROWS // 2)], axis=1)
        s_w = per_unit(lambda n, il, lp, hh: _dot_nt(qm[n], kw[il][lp]) + bias(2 * lp + hh, d0[il]))
        s_c = per_unit(lambda n, il, lp, hh: _dot_nt(qm[n], kc[lp]))
        m = per_unit(lambda n, il, lp, hh: jnp.maximum(jnp.max(s_w[n], axis=-1, keepdims=True),
                                                       jnp.max(s_c[n], axis=-1, keepdims=True)))
        p_w = per_unit(lambda n, il, lp, hh: jnp.exp(s_w[n] - m[n]))
        p_c = per_unit(lambda n, il, lp, hh: jnp.exp(s_c[n] - m[n]))
        l = per_unit(lambda n, il, lp, hh: jnp.sum(p_w[n], axis=-1, keepdims=True)
                     + jnp.sum(p_c[n], axis=-1, keepdims=True))
        o = per_unit(lambda n, il, lp, hh: (_dot(p_w[n].astype(bf16), vw[il][lp])
                                            + _dot(p_c[n].astype(bf16), vc[lp])) / l[n])
        for n in range(0, len(units), 2):
            il, lp, _ = units[n]
            o_ref[il * GRID_W:(il + 1) * GRID_W, lanes(lp)] = jnp.where(low, o[n], o[n + 1]).astype(o_ref.dtype)


def na_attention(uc, bias, n_ctx):
    n = uc.shape[0]
    width = uc.shape[1] // 3
    pps = NA_PAIRS_PER_STEP
    ngroup = width // (pps * LANES)
    n_rows = (n - n_ctx) // GRID_W
    return pl.pallas_call(
        functools.partial(_na_body, n_ctx=n_ctx, n_rows=n_rows),
        grid=(ngroup, n // ROW_TILE),
        in_specs=[
            pl.BlockSpec((ROW_TILE, pps * LANES), lambda p, j: (j, p)),
            pl.BlockSpec((n, pps * LANES), lambda p, j: (0, ngroup + p)),
            pl.BlockSpec((n, pps * LANES), lambda p, j: (0, 2 * ngroup + p)),
            pl.BlockSpec((2 * pps,) + bias.shape[1:], lambda p, j: (p, 0, 0, 0)),
        ],
        out_specs=pl.BlockSpec((ROW_TILE, pps * LANES), lambda p, j: (j, p)),
        out_shape=jax.ShapeDtypeStruct((n, width), bf16),
        compiler_params=_params(("arbitrary", "arbitrary")),
        name="na_attention",
    )(uc, uc, uc, bias)


def _rope_tables(n_ctx, seq):
    t = jnp.arange(seq)
    row = (t // GRID_W).astype(f32)
    col = (t % GRID_W).astype(f32)
    n_freq = QK_ROPE // 4
    inv = ROPE_BASE ** (-jnp.arange(n_freq, dtype=f32) / n_freq)
    ar, ac = row[:, None] * inv[None, :], col[:, None] * inv[None, :]
    cos = jnp.concatenate([jnp.cos(ar), jnp.cos(ar), jnp.cos(ac), jnp.cos(ac)], axis=1)
    sin = jnp.concatenate([-jnp.sin(ar), jnp.sin(ar), -jnp.sin(ac), jnp.sin(ac)], axis=1)
    lat = jnp.concatenate([cos, sin], axis=1)
    ctx = jnp.concatenate([jnp.ones((n_ctx, QK_ROPE), f32), jnp.zeros((n_ctx, QK_ROPE), f32)], axis=1)
    return jnp.concatenate([ctx, lat], axis=0)


def _pair_swap_perm():
    q = QK_ROPE // 4
    return np.concatenate([np.arange(q, 2 * q), np.arange(0, q), np.arange(3 * q, 4 * q), np.arange(2 * q, 3 * q)])


def _na_bias_tables(rpb):
    cols = np.arange(GRID_W)
    c0 = np.clip(cols - WIN_COLS // 2, 0, GRID_W - WIN_COLS)
    inside = (cols[None, :] >= c0[:, None]) & (cols[None, :] < c0[:, None] + WIN_COLS)
    rel = cols[None, :] - cols[:, None] + WIN_COLS - 1
    place = (np.arange(2 * WIN_COLS - 1)[:, None, None] == rel[None]) & inside[None]
    full = jnp.einsum("lhdk,kcq->lhdcq", rpb, jnp.asarray(place, f32), precision=lax.Precision.HIGHEST)
    full = full + jnp.asarray(np.where(inside, 0.0, NEG_BIG), f32)
    return jnp.concatenate([full[:, :, :-1], full[:, :, 1:]], axis=-1)


def _layer_weights(l, w_small, rwkv_conv, rwkv_w0, rwkv_w_up, rwkv_a0, rwkv_a_up, rwkv_g_up, rwkv_k_k, rwkv_k_a,
                   rwkv_r_k, mla_q_norm_g, mla_w_uq, mla_kv_norm_g, mla_w_ukv, w_branch, w_out, mlp_w1, mlp_w2):
    d = w_small.shape[1]
    aw = rwkv_k_k.shape[1]
    nh_b = mla_w_ukv.shape[2] // (QK_NOPE + V_DIM)
    n_lora = 2 * LORA_W + 2 * LORA_A + LORA_G
    c_lora = 3 * aw
    c_mla = n_lora
    c_end = c_mla + Q_LORA + KV_LORA + QK_ROPE
    wi = w_small[l]
    perm = _pair_swap_perm()
    kr = wi[:, c_mla + Q_LORA + KV_LORA:c_end]
    w_b = jnp.concatenate([
        wi[:, :c_mla],
        wi[:, c_mla + Q_LORA:c_mla + Q_LORA + KV_LORA],
        wi[:, c_mla:c_mla + Q_LORA], jnp.zeros((d, 512 - Q_LORA), bf16),
        kr, kr[:, perm]], axis=1)
    cols = dict(c_kvd=n_lora, c_q=n_lora + KV_LORA, c_kr=n_lora + KV_LORA + 512)

    def lora_pad(w, start):
        r = w.shape[1]
        out = jnp.zeros((2, n_lora, aw), f32)
        for z in range(2):
            out = out.at[z, start + z * r:start + (z + 1) * r].set(w[z])
        return out.astype(bf16)

    g_up = jnp.zeros((n_lora, aw), f32).at[2 * LORA_W + 2 * LORA_A:].set(rwkv_g_up[l]).astype(bf16)
    head_id = np.arange(2 * LANES) // A_HEAD_DIM
    seg = jnp.asarray(head_id[:, None] == head_id[None, :], bf16)
    rwkv = dict(conv_rkv=rwkv_conv[l][:, :c_lora], conv_l=rwkv_conv[l][:, c_lora:c_lora + n_lora],
                w_up=lora_pad(rwkv_w_up[l], 0), a_up=lora_pad(rwkv_a_up[l], 2 * LORA_W), g_up=g_up,
                w0=rwkv_w0[l], a0=rwkv_a0[l], k_k=rwkv_k_k[l][None], k_a=rwkv_k_a[l][None],
                r_k=rwkv_r_k[l].reshape(1, aw), seg=seg)

    uq = mla_w_uq[l].reshape(Q_LORA, nh_b, QK_NOPE + QK_ROPE)
    uq = jnp.concatenate([uq, jnp.zeros((512 - Q_LORA, nh_b, QK_NOPE + QK_ROPE), f32)], axis=0)
    zpad = jnp.zeros((512, nh_b, LANES - QK_ROPE), f32)
    pe = uq[:, :, QK_NOPE:]
    ukv = mla_w_ukv[l].reshape(KV_LORA, nh_b, QK_NOPE + V_DIM)
    mla = dict(
        gq=jnp.concatenate([mla_q_norm_g[l], jnp.zeros((512 - Q_LORA,), f32)])[None],
        gkv=mla_kv_norm_g[l][None],
        wq_nope=uq[:, :, :QK_NOPE].reshape(512, nh_b * QK_NOPE).astype(bf16),
        wq_pe=jnp.concatenate([pe, zpad], axis=2).reshape(512, nh_b * LANES).astype(bf16),
        wq_pr=jnp.concatenate([pe[:, :, perm], zpad], axis=2).reshape(512, nh_b * LANES).astype(bf16),
        wkv=jnp.concatenate([ukv[:, :, :QK_NOPE].reshape(KV_LORA, -1), ukv[:, :, QK_NOPE:].reshape(KV_LORA, -1)],
                            axis=1).astype(bf16))
    return dict(w_b=w_b, cols=cols, rwkv=rwkv, mla=mla,
                w_branch=w_branch[l].astype(bf16), w_out=w_out[l].astype(bf16),
                mlp_w1=mlp_w1[l].astype(bf16), mlp_w2=mlp_w2[l].astype(bf16))


def kernel(x, c, ctx, c_ctx, ada_w, ada_b, norm_mix_g, norm_mlp_g, w_in, rwkv_conv, rwkv_w0, rwkv_w_up, rwkv_a0, rwkv_a_up, rwkv_g_up, rwkv_k_k, rwkv_k_a, rwkv_r_k, rwkv_ln_g, rwkv_ln_b, mla_q_norm_g, mla_w_uq, mla_kv_norm_g, mla_w_ukv, na_rpb, w_branch, w_out, mlp_w1, mlp_w2, final_norm_g):
    batch, seq, d = x.shape
    n_ctx = ctx.shape[1]
    depth = ada_w.shape[0]
    assert batch == 1 and n_ctx == ROW_TILE and seq % ROW_TILE == 0 and seq // GRID_W >= WIN_ROWS
    n_ctx_tiles = n_ctx // ROW_TILE

    xs = jnp.concatenate([ctx[0], x[0]], axis=0)
    cond = jnp.concatenate([c, c_ctx[None], jnp.zeros((6, d), f32)], axis=0)
    mods = ada_modulation(cond, ada_w, ada_b)
    cs = _rope_tables(n_ctx, seq)
    na_bias = _na_bias_tables(na_rpb)
    aw = rwkv_k_k.shape[1]
    n_b = w_in.shape[2] - 3 * aw - 3 * aw - N_BRANCH * d
    w_main, w_small = cast_w_in_groups(w_in, 3 * aw, n_b)

    for l in range(depth):
        lw = _layer_weights(l, w_small, rwkv_conv, rwkv_w0, rwkv_w_up, rwkv_a0, rwkv_a_up, rwkv_g_up, rwkv_k_k,
                            rwkv_k_a, rwkv_r_k, mla_q_norm_g, mla_w_uq, mla_kv_norm_g, mla_w_ukv, w_branch,
                            w_out, mlp_w1, mlp_w2)
        mods8 = mods[l]
        mods2 = mods8[:2, None, :]
        h = norm_modulate(xs, norm_mix_g[l], mods2, 0)
        u3 = matmul(h, w_main, f32, 1024, layer=l, col0=0, ncols=3 * aw)
        ub = matmul(h, lw["w_b"], f32, lw["w_b"].shape[1])
        uc = matmul(h, w_main, bf16, 1024, layer=l, col0=3 * aw, ncols=3 * aw)
        gates = matmul(h, w_main, bf16, 1024, sigmoid=True, layer=l, col0=6 * aw, ncols=N_BRANCH * d)

        r, v, kk, bonus, g, lwd, kd, bd = rwkv_prepare(u3, ub, lw["rwkv"], n_ctx)
        y = rwkv_chunk_scan(r, v, kk, lwd, kd, bd, n_ctx)
        ya = rwkv_readout(y, bonus, g, rwkv_ln_g[l], rwkv_ln_b[l], lw["rwkv"]["seg"][:LANES, :LANES])

        q_b, k_b, v_b = mla_prepare(ub, cs, lw["mla"], lw["cols"])
        yb = mla_attention(q_b, k_b, v_b, n_ctx)

        yc = na_attention(uc, na_bias[l], n_ctx)

        merged = merge_branches(ya, yb, yc, gates, lw["w_branch"])
        xs = matmul_gated_residual(merged, lw["w_out"], xs, mods8, 2, n_ctx)
        h2 = norm_modulate(xs, norm_mlp_g[l], mods2, 3)
        xs = mlp_gated_residual(h2, lw["mlp_w1"], lw["mlp_w2"], xs, mods8, 5, n_ctx)

    return final_norm(xs, final_norm_g, n_ctx_tiles)[None]
```

```python
import functools
import math

import jax
import jax.numpy as jnp
import numpy as np
from jax import lax
from jax.experimental import pallas as pl
from jax.experimental.pallas import tpu as pltpu

f32 = jnp.float32
bf16 = jnp.bfloat16

GRID_W = 64
RMS_EPS = 1e-6
N_BRANCH = 3
A_HEAD_DIM = 64
LORA_W = 96
LORA_A = 96
LORA_G = 256
GN_EPS = 64e-5
QK_NOPE = 128
QK_ROPE = 64
V_DIM = 128
Q_LORA = 448
KV_LORA = 128
MLA_SCALE = (QK_NOPE + QK_ROPE) ** -0.5
MLA_Q_SCALE = MLA_SCALE * math.log2(math.e)
ROPE_BASE = 10000.0
C_HEAD_DIM = 64
WIN_ROWS = 8
WIN_COLS = 16
NA_SCALE = C_HEAD_DIM ** -0.5
DECAY_SCALE = math.exp(-0.5)

LANES = 128
ROW_TILE = 256
CHUNK = 64
CHUNKS_PER_STEP = 2
PREP_TILE = 128
NA_PAIRS_PER_STEP = 2
MLA_CHAIN_LAG = 2
MLA_HEADS_PER_STEP = 2
VMEM_LIMIT = 48 * 1024 * 1024
NEG_BIG = -1e30


def _params(sem):
    return pltpu.CompilerParams(dimension_semantics=sem, vmem_limit_bytes=VMEM_LIMIT)


def _dot(a, b):
    return jnp.dot(a, b, preferred_element_type=f32)


def _dot_nt(a, b):
    return lax.dot_general(a, b, (((1,), (1,)), ((), ())), preferred_element_type=f32)


def _ada_body(s_ref, w_ref, b_ref, o_ref):
    s = s_ref[...]
    s = s * jax.nn.sigmoid(s)
    o_ref[...] = _dot(s, w_ref[...]) + b_ref[...]


def ada_modulation(cond, ada_w, ada_b):
    nl, d, n6 = ada_w.shape
    tn = 1024
    return pl.pallas_call(
        _ada_body,
        grid=(nl, n6 // tn),
        in_specs=[
            pl.BlockSpec((8, d), lambda l, j: (0, 0)),
            pl.BlockSpec((None, d, tn), lambda l, j: (l, 0, j)),
            pl.BlockSpec((None, 1, tn), lambda l, j: (l, 0, j)),
        ],
        out_specs=pl.BlockSpec((None, 8, tn), lambda l, j: (l, 0, j)),
        out_shape=jax.ShapeDtypeStruct((nl, 8, n6), f32),
        compiler_params=_params(("arbitrary", "arbitrary")),
        name="ada_modulation",
    )(cond, ada_w, ada_b.reshape(nl, 1, n6))


def _norm_mod_body(x_ref, g_ref, m_ref, o_ref, *, off, d):
    x = x_ref[...]
    y = x * lax.rsqrt(jnp.mean(x * x, axis=-1, keepdims=True) + RMS_EPS) * g_ref[...]
    shift = m_ref[:, off * d:(off + 1) * d]
    scale = m_ref[:, (off + 1) * d:(off + 2) * d]
    o_ref[...] = (y * (1.0 + scale) + shift).astype(o_ref.dtype)


def norm_modulate(x, g, mods2, off):
    n, d = x.shape
    return pl.pallas_call(
        functools.partial(_norm_mod_body, off=off, d=d),
        grid=(n // ROW_TILE,),
        in_specs=[
            pl.BlockSpec((ROW_TILE, d), lambda i: (i, 0)),
            pl.BlockSpec((1, d), lambda i: (0, 0)),
            pl.BlockSpec((None, 1, mods2.shape[-1]), lambda i: (jnp.where(i == 0, 1, 0), 0, 0)),
        ],
        out_specs=pl.BlockSpec((ROW_TILE, d), lambda i: (i, 0)),
        out_shape=jax.ShapeDtypeStruct((n, d), bf16),
        compiler_params=_params(("arbitrary",)),
        name="norm_modulate",
    )(x, g.reshape(1, d), mods2)


def _final_norm_body(x_ref, g_ref, o_ref):
    x = x_ref[...]
    o_ref[...] = x * lax.rsqrt(jnp.mean(x * x, axis=-1, keepdims=True) + RMS_EPS) * g_ref[...]


def final_norm(x, g, n_ctx_tiles):
    n, d = x.shape
    t = n - n_ctx_tiles * ROW_TILE
    return pl.pallas_call(
        _final_norm_body,
        grid=(t // ROW_TILE,),
        in_specs=[
            pl.BlockSpec((ROW_TILE, d), lambda i: (i + n_ctx_tiles, 0)),
            pl.BlockSpec((1, d), lambda i: (0, 0)),
        ],
        out_specs=pl.BlockSpec((ROW_TILE, d), lambda i: (i, 0)),
        out_shape=jax.ShapeDtypeStruct((t, d), f32),
        compiler_params=_params(("arbitrary",)),
        name="final_norm",
    )(x, g.reshape(1, d))


def _mm_body(a_ref, w_ref, o_ref, *, sigmoid):
    acc = _dot(a_ref[...], w_ref[...])
    o_ref[...] = (jax.nn.sigmoid(acc) if sigmoid else acc).astype(o_ref.dtype)


def _row_tile(n):
    for tm in (1056, 768, 512, 384, 256):
        if n % tm == 0:
            return tm
    raise ValueError(f"no row tile for {n} rows")


def matmul(a, w, out_dtype, tn, sigmoid=False, layer=None, col0=0, ncols=None):
    m, k = a.shape
    n = (w.shape[-1] - col0) if ncols is None else ncols
    tm = _row_tile(m)
    assert col0 % tn == 0 and n % tn == 0
    j0 = col0 // tn
    if layer is None:
        w_spec = pl.BlockSpec((k, tn), lambda i, j: (0, j0 + j))
    else:
        w_spec = pl.BlockSpec((None, k, tn), lambda i, j: (layer, 0, j0 + j))
    return pl.pallas_call(
        functools.partial(_mm_body, sigmoid=sigmoid),
        grid=(m // tm, n // tn),
        in_specs=[pl.BlockSpec((tm, k), lambda i, j: (i, 0)), w_spec],
        out_specs=pl.BlockSpec((tm, tn), lambda i, j: (i, j)),
        out_shape=jax.ShapeDtypeStruct((m, n), out_dtype),
        compiler_params=_params(("arbitrary", "arbitrary")),
        name="matmul",
    )(a, w)


def _cast_body(x_ref, o_ref):
    o_ref[...] = x_ref[...].astype(o_ref.dtype)


def cast_w_in_groups(w_in, mid_start, mid_cols):
    nl, d, c = w_in.shape
    tc = 2 * LANES
    assert mid_start % tc == 0 and mid_cols % tc == 0 and c % tc == 0
    j_mid, n_mid = mid_start // tc, mid_cols // tc

    def cast(n_blocks, col_block):
        return pl.pallas_call(
            _cast_body,
            grid=(nl, n_blocks),
            in_specs=[pl.BlockSpec((None, d, tc), lambda l, j: (l, 0, col_block(j)))],
            out_specs=pl.BlockSpec((None, d, tc), lambda l, j: (l, 0, j)),
            out_shape=jax.ShapeDtypeStruct((nl, d, n_blocks * tc), bf16),
            compiler_params=_params(("arbitrary", "arbitrary")),
            name="cast_w_in_groups",
        )(w_in)

    return (cast(c // tc - n_mid, lambda j: jnp.where(j < j_mid, j, j + n_mid)),
            cast(n_mid, lambda j: j_mid + j))


def _gate_rows(m_ref, tm, n_ctx):
    rows = pl.program_id(0) * tm + lax.broadcasted_iota(jnp.int32, (tm, 1), 0)
    return jnp.where(rows < n_ctx, m_ref[1:2, :], m_ref[0:1, :])


def _mm_res_body(a_ref, w_ref, x_ref, m_ref, o_ref, *, tm, n_ctx):
    acc = _dot(a_ref[...], w_ref[...])
    o_ref[...] = x_ref[...] + _gate_rows(m_ref, tm, n_ctx) * acc


def matmul_gated_residual(a, w, layer, x, mods8, gate_off, n_ctx, tn=512):
    m, k = a.shape
    n = w.shape[-1]
    tm = _row_tile(m)
    nj = n // tn
    return pl.pallas_call(
        functools.partial(_mm_res_body, tm=tm, n_ctx=n_ctx),
        grid=(m // tm, nj),
        in_specs=[
            pl.BlockSpec((tm, k), lambda i, j: (i, 0)),
            pl.BlockSpec((None, k, tn), lambda i, j: (layer, 0, j)),
            pl.BlockSpec((tm, tn), lambda i, j: (i, j)),
            pl.BlockSpec((8, tn), lambda i, j: (0, gate_off * nj + j)),
        ],
        out_specs=pl.BlockSpec((tm, tn), lambda i, j: (i, j)),
        out_shape=jax.ShapeDtypeStruct((m, n), f32),
        compiler_params=_params(("arbitrary", "arbitrary")),
        name="matmul_gated_residual",
    )(a, w, x, mods8)


def _merge_body(ya_ref, yb_ref, yc_ref, g0_ref, g1_ref, g2_ref, w_ref, o_ref):
    acc = g0_ref[...] * _dot(ya_ref[...], w_ref[0])
    acc = acc + g1_ref[...] * _dot(yb_ref[...], w_ref[1])
    acc = acc + g2_ref[...] * _dot(yc_ref[...], w_ref[2])
    o_ref[...] = acc.astype(o_ref.dtype)


def merge_branches(ya, yb, yc, gates, w_branch, layer, tn=1024):
    m, k = ya.shape
    d = w_branch.shape[-1]
    tm = 528 if m % 528 == 0 else ROW_TILE
    nj = d // tn
    y_spec = pl.BlockSpec((tm, k), lambda i, j: (i, 0))
    return pl.pallas_call(
        _merge_body,
        grid=(m // tm, nj),
        in_specs=[
            y_spec, y_spec, y_spec,
            pl.BlockSpec((tm, tn), lambda i, j: (i, j)),
            pl.BlockSpec((tm, tn), lambda i, j: (i, nj + j)),
            pl.BlockSpec((tm, tn), lambda i, j: (i, 2 * nj + j)),
            pl.BlockSpec((None, N_BRANCH, k, tn), lambda i, j: (layer, 0, 0, j)),
        ],
        out_specs=pl.BlockSpec((tm, tn), lambda i, j: (i, j)),
        out_shape=jax.ShapeDtypeStruct((m, d), bf16),
        compiler_params=_params(("arbitrary", "arbitrary")),
        name="merge_branches",
    )(ya, yb, yc, gates, gates, gates, w_branch)


def _mlp_body(h_ref, w1_ref, w2_ref, x_ref, m_ref, o_ref, acc_ref, *, tm, n_ctx):
    f = pl.program_id(1)

    @pl.when(f == 0)
    def _():
        acc_ref[...] = jnp.zeros_like(acc_ref)

    h1 = jnp.maximum(_dot(h_ref[...], w1_ref[...]), 0.0)
    acc_ref[...] += _dot((h1 * h1).astype(bf16), w2_ref[...])

    @pl.when(f == pl.num_programs(1) - 1)
    def _():
        o_ref[...] = x_ref[...] + _gate_rows(m_ref, tm, n_ctx) * acc_ref[...]


def mlp_gated_residual(h, w1, w2, layer, x, mods8, gate_off, n_ctx, tf=1024):
    m, d = h.shape
    dff = w1.shape[-1]
    tm = 528 if m % 528 == 0 else ROW_TILE
    return pl.pallas_call(
        functools.partial(_mlp_body, tm=tm, n_ctx=n_ctx),
        grid=(m // tm, dff // tf),
        in_specs=[
            pl.BlockSpec((tm, d), lambda i, f: (i, 0)),
            pl.BlockSpec((None, d, tf), lambda i, f: (layer, 0, f)),
            pl.BlockSpec((None, tf, d), lambda i, f: (layer, f, 0)),
            pl.BlockSpec((tm, d), lambda i, f: (i, 0)),
            pl.BlockSpec((8, d), lambda i, f: (0, gate_off)),
        ],
        out_specs=pl.BlockSpec((tm, d), lambda i, f: (i, 0)),
        out_shape=jax.ShapeDtypeStruct((m, d), f32),
        scratch_shapes=[pltpu.VMEM((tm, d), f32)],
        compiler_params=_params(("arbitrary", "arbitrary")),
        name="mlp_gated_residual",
    )(h, w1, w2, x, mods8)


def _seg_sum(x, e_ref):
    hi = x.astype(bf16)
    lo = (x - hi.astype(f32)).astype(bf16)
    e = e_ref[...]
    w = e.shape[0]
    parts = [_dot(hi[:, g:g + w], e) + _dot(lo[:, g:g + w], e) for g in range(0, x.shape[1], w)]
    return parts[0] if len(parts) == 1 else jnp.concatenate(parts, axis=1)


def _to_pairs(o_ref, val, lead=None):
    for p in range(val.shape[-1] // LANES):
        piece = val[:, p * LANES:(p + 1) * LANES]
        if lead is None:
            o_ref[p] = piece
        else:
            o_ref[lead, p] = piece


def _rwkv_prep_body(u_ref, up_ref, un_ref, l_ref, lp_ref, ln_ref, cw_ref, cwl_ref, wu_ref, au_ref, gu_ref,
                    w0_ref, a0_ref, kk_ref, ka_ref, rk_ref, e_ref,
                    r_o, v_o, kkn_o, bonus_o, g_o, lw_o, kd_o, bd_o, *, n_ctx_tiles, aw):
    i = pl.program_id(0)
    last = pl.num_programs(0) - 1
    left_zero = (i == 0) | (i == n_ctx_tiles)
    right_zero = (i == n_ctx_tiles - 1) | (i == last)
    rows = lax.broadcasted_iota(jnp.int32, (PREP_TILE, 1), 0)

    def conv(x_ref, xp_ref, xn_ref, w_ref):
        x = x_ref[...]
        prev_row = jnp.where(left_zero, 0.0, xp_ref[7:8, :])
        next_row = jnp.where(right_zero, 0.0, xn_ref[0:1, :])
        x_prev = jnp.where(rows == 0, prev_row, pltpu.roll(x, 1, 0))
        x_next = jnp.where(rows == PREP_TILE - 1, next_row, pltpu.roll(x, PREP_TILE - 1, 0))
        return w_ref[0:1, :] * x_prev + w_ref[1:2, :] * x + w_ref[2:3, :] * x_next

    y = conv(u_ref, up_ref, un_ref, cw_ref)
    r, k, v = y[:, :aw], y[:, aw:2 * aw], y[:, 2 * aw:]
    yl = conv(l_ref, lp_ref, ln_ref, cwl_ref)
    th = jnp.tanh(yl).astype(bf16)
    sg = jax.nn.sigmoid(yl).astype(bf16)
    ylb = yl.astype(bf16)

    _to_pairs(g_o, _dot(sg, gu_ref[...]))
    _to_pairs(r_o, r)
    _to_pairs(v_o, v.astype(bf16))

    kkr = k * kk_ref[...]
    norm = jnp.sqrt(_seg_sum(kkr * kkr, e_ref))
    kkn = kkr / jnp.maximum(norm, 1e-12)
    _to_pairs(kkn_o, kkn)

    ksum = None
    for dr in range(2):
        z = w0_ref[dr:dr + 1, :] + _dot(th, wu_ref[dr])
        _to_pairs(lw_o, -DECAY_SCALE * jax.nn.sigmoid(z), lead=dr)
        a = jax.nn.sigmoid(a0_ref[dr:dr + 1, :] + _dot(ylb, au_ref[dr]))
        kd = k * (1.0 + (a - 1.0) * ka_ref[...])
        _to_pairs(kd_o, kd, lead=dr)
        _to_pairs(bd_o, kkn * a, lead=dr)
        ksum = kd if ksum is None else ksum + kd
    rk = _seg_sum(r * (0.5 * ksum) * rk_ref[...], e_ref)
    _to_pairs(bonus_o, rk * v)


def rwkv_prepare(u3, ub, pw, n_ctx):
    n = u3.shape[0]
    aw = u3.shape[1] // 3
    npair = aw // LANES
    lw = pw["conv_l"].shape[1]
    tpb = PREP_TILE // 8
    nb8 = n // 8

    def prev_map(i):
        return (jnp.maximum(i * tpb - 1, 0), 0)

    def next_map(i):
        return (jnp.minimum((i + 1) * tpb, nb8 - 1), 0)

    full = lambda a: pl.BlockSpec(a.shape, lambda i: (0,) * a.ndim)
    pm = jax.ShapeDtypeStruct((npair, n, LANES), f32)
    pm2 = jax.ShapeDtypeStruct((2, npair, n, LANES), f32)
    pm_bf16 = jax.ShapeDtypeStruct((npair, n, LANES), bf16)
    pm_spec = pl.BlockSpec((npair, PREP_TILE, LANES), lambda i: (0, i, 0))
    pm2_spec = pl.BlockSpec((2, npair, PREP_TILE, LANES), lambda i: (0, 0, i, 0))
    consts = [pw["conv_rkv"], pw["conv_l"], pw["w_up"], pw["a_up"], pw["g_up"], pw["w0"], pw["a0"],
              pw["k_k"], pw["k_a"], pw["r_k"], pw["seg"]]
    return pl.pallas_call(
        functools.partial(_rwkv_prep_body, n_ctx_tiles=n_ctx // PREP_TILE, aw=aw),
        grid=(n // PREP_TILE,),
        in_specs=[
            pl.BlockSpec((PREP_TILE, 3 * aw), lambda i: (i, 0)),
            pl.BlockSpec((8, 3 * aw), prev_map),
            pl.BlockSpec((8, 3 * aw), next_map),
            pl.BlockSpec((PREP_TILE, lw), lambda i: (i, 0)),
            pl.BlockSpec((8, lw), prev_map),
            pl.BlockSpec((8, lw), next_map),
        ] + [full(a) for a in consts],
        out_specs=[pm_spec] * 5 + [pm2_spec] * 3,
        out_shape=[pm, pm_bf16] + [pm] * 3 + [pm2] * 3,
        compiler_params=_params(("arbitrary",)),
        name="rwkv_prepare",
    )(u3, u3, u3, ub, ub, ub, *consts)


def _dot_tn(a, b):
    return lax.dot_general(a, b, (((0,), (0,)), ((), ())), preferred_element_type=f32)


def _rwkv_chunk_body(r_ref, v_ref, kk_ref, lw_ref, k_ref, b_ref, y_ref, h_ref, *, n_pairs):
    d = pl.program_id(0)
    j = pl.program_id(1)
    c = CHUNK
    c2 = 2 * c

    @pl.when(j == 0)
    def _():
        h_ref[...] = jnp.zeros_like(h_ref)

    sign = jnp.where(d == 0, 1, -1)
    ti = lax.broadcasted_iota(jnp.int32, (c, c), 0)
    si = lax.broadcasted_iota(jnp.int32, (c, c), 1)
    incl_c = jnp.where((ti - si) * sign >= 0, 1.0, 0.0).astype(bf16)
    t2 = lax.broadcasted_iota(jnp.int32, (c2, c2), 0)
    s2 = lax.broadcasted_iota(jnp.int32, (c2, c2), 1)
    ahead = (t2 - s2) * sign
    blk = lambda n: (t2 // n) == (s2 // n)
    incl = blk(c) & (ahead >= 0)
    strict = blk(c) & (ahead > 0)
    eye = jnp.where(t2 == s2, 1.0, 0.0)
    inv_levels = []
    n = 2
    while n < c:
        inv_levels.append(strict & blk(2 * n) & ~blk(n))
        n *= 2
    low = lax.broadcasted_iota(jnp.int32, (1, LANES), 1) < A_HEAD_DIM

    def stack(x):
        return jnp.concatenate([jnp.where(low, x, 0.0), jnp.where(low, 0.0, x)], axis=0).astype(bf16)

    pairs = range(n_pairs)
    first = jnp.where(d == 0, 0, CHUNKS_PER_STEP - 1)
    offs = [pl.multiple_of((first + sign * sc) * c, c) for sc in range(CHUNKS_PER_STEP)]
    units = [(sc, p) for sc in range(CHUNKS_PER_STEP) for p in pairs]
    each = lambda f: [f(sc, p) for sc, p in units]
    at = lambda sc, p: sc * n_pairs + p
    ld = lambda ref, sc, p: ref[p, pl.ds(offs[sc], c), :]
    lw = each(lambda sc, p: ld(lw_ref, sc, p))
    lw_hi = [x.astype(bf16) for x in lw]
    lw_lo = [(x - hi.astype(f32)).astype(bf16) for x, hi in zip(lw, lw_hi)]
    cum = [_dot(incl_c, hi) + _dot(incl_c, lo) for hi, lo in zip(lw_hi, lw_lo)]
    tot = [jnp.sum(x, axis=0, keepdims=True) for x in lw]
    e_neg = [jnp.exp(-x) for x in cum]
    e_end = [jnp.exp(t - x) for t, x in zip(tot, cum)]
    r_s = each(lambda sc, p: stack(ld(r_ref, sc, p) * jnp.exp(cum[at(sc, p)])))
    kk_s = each(lambda sc, p: stack(ld(kk_ref, sc, p) * jnp.exp(cum[at(sc, p)] - lw[at(sc, p)])))
    kb_s = each(lambda sc, p: jnp.concatenate([stack(ld(b_ref, sc, p) * e_neg[at(sc, p)]),
                                               stack(ld(k_ref, sc, p) * e_neg[at(sc, p)])], axis=0))
    kbd_s = each(lambda sc, p: jnp.concatenate([stack(ld(k_ref, sc, p) * e_end[at(sc, p)]),
                                                stack(ld(b_ref, sc, p) * e_end[at(sc, p)])], axis=0))
    v_s = each(lambda sc, p: stack(ld(v_ref, sc, p)))
    a1 = [_dot_nt(a, b) for a, b in zip(kk_s, kb_s)]
    a2 = [_dot_nt(a, b) for a, b in zip(r_s, kb_s)]
    a_kb = [jnp.where(strict, a[:, :c2], 0.0) for a in a1]
    a_kk = [jnp.where(strict, a[:, c2:], 0.0).astype(bf16) for a in a1]
    a_rb = [jnp.where(incl, a[:, :c2], 0.0).astype(bf16) for a in a2]
    a_rk = [jnp.where(incl, a[:, c2:], 0.0).astype(bf16) for a in a2]
    t_inv = [eye - jnp.where(blk(2), a, 0.0) for a in a_kb]
    for m in inv_levels:
        t_b = [t.astype(bf16) for t in t_inv]
        x = [_dot(t, jnp.where(m, a, 0.0).astype(bf16)).astype(bf16) for t, a in zip(t_b, a_kb)]
        t_inv = [t - _dot(xx, tb) for t, xx, tb in zip(t_inv, x, t_b)]
    g = [_dot(t.astype(bf16), jnp.concatenate([kk, akk], axis=1)).astype(bf16)
         for t, kk, akk in zip(t_inv, kk_s, a_kk)]
    w_col = [jnp.sum(eye * jnp.exp(t), axis=1, keepdims=True) for t in tot]
    h = [h_ref[p] for p in pairs]
    for sc in range(CHUNKS_PER_STEP):
        ix = [at(sc, p) for p in pairs]
        h_b = [x.astype(bf16) for x in h]
        u = [_dot(g[i], jnp.concatenate([h_b[p], v_s[i]], axis=0)).astype(bf16) for p, i in enumerate(ix)]
        y = [_dot(jnp.concatenate([r_s[i], a_rk[i], -a_rb[i]], axis=1), jnp.concatenate([h_b[p], v_s[i], u[p]], axis=0))
             for p, i in enumerate(ix)]
        h1 = [_dot_tn(kbd_s[i], jnp.concatenate([v_s[i], -u[p]], axis=0)) for p, i in enumerate(ix)]
        for p in pairs:
            y_ref[p, pl.ds(offs[sc], c), :] = y[p][:c] + y[p][c:]
        h = [h[p] * w_col[i] + h1[p] for p, i in enumerate(ix)]
    for p in pairs:
        h_ref[p] = h[p]


def rwkv_chunk_scan(r, v, kk, lw, kd, bd, n_ctx):
    npair, n, _ = r.shape
    blk_rows = CHUNKS_PER_STEP * CHUNK
    nc = n // blk_rows
    ncc = n_ctx // blk_rows

    def cidx(d, j):
        bwd = jnp.where(j < ncc, ncc - 1 - j, nc - 1 - (j - ncc))
        return jnp.where(d == 0, j, bwd)

    shared = pl.BlockSpec((npair, blk_rows, LANES), lambda d, j: (0, cidx(d, j), 0))
    per_dir = pl.BlockSpec((None, npair, blk_rows, LANES), lambda d, j: (d, 0, cidx(d, j), 0))
    return pl.pallas_call(
        functools.partial(_rwkv_chunk_body, n_pairs=npair),
        grid=(2, nc),
        in_specs=[shared, shared, shared, per_dir, per_dir, per_dir],
        out_specs=per_dir,
        out_shape=jax.ShapeDtypeStruct((2, npair, n, LANES), f32),
        scratch_shapes=[pltpu.VMEM((npair, LANES, LANES), f32)],
        compiler_params=_params(("arbitrary", "arbitrary")),
        name="rwkv_chunk_scan",
    )(r, v, kk, lw, kd, bd)


def _rwkv_readout_body(y_ref, bonus_ref, g_ref, lng_ref, lnb_ref, e_ref, o_ref, *, n_pairs):
    inv = 1.0 / A_HEAD_DIM
    for p in range(n_pairs):
        y = y_ref[0, p] + y_ref[1, p]
        yc = y - _seg_sum(y, e_ref) * inv
        var = _seg_sum(yc * yc, e_ref) * inv
        yn = yc * lax.rsqrt(var + GN_EPS) * lng_ref[p] + lnb_ref[p]
        o_ref[:, p * LANES:(p + 1) * LANES] = ((yn + bonus_ref[p]) * g_ref[p]).astype(o_ref.dtype)


def rwkv_readout(y, bonus, g, ln_g, ln_b, seg_pair):
    _, npair, n, _ = y.shape
    pm_spec = pl.BlockSpec((npair, ROW_TILE, LANES), lambda i: (0, i, 0))
    vec_spec = pl.BlockSpec((npair, 1, LANES), lambda i: (0, 0, 0))
    return pl.pallas_call(
        functools.partial(_rwkv_readout_body, n_pairs=npair),
        grid=(n // ROW_TILE,),
        in_specs=[pl.BlockSpec((2, npair, ROW_TILE, LANES), lambda i: (0, 0, i, 0)), pm_spec, pm_spec,
                  vec_spec, vec_spec, pl.BlockSpec((LANES, LANES), lambda i: (0, 0))],
        out_specs=pl.BlockSpec((ROW_TILE, npair * LANES), lambda i: (i, 0)),
        out_shape=jax.ShapeDtypeStruct((n, npair * LANES), bf16),
        compiler_params=_params(("arbitrary",)),
        name="rwkv_readout",
    )(y, bonus, g, ln_g.reshape(npair, 1, LANES), ln_b.reshape(npair, 1, LANES), seg_pair)


def _mla_prep_body(u_ref, cs_ref, gq_ref, gkv_ref, wqn_ref, wqp_ref, wqr_ref, wkv_ref, q_o, k_o, v_o,
                   *, n_heads, c_kvd, c_q, c_kr):
    ql = u_ref[:, c_q:c_q + 512]
    qn = ql * lax.rsqrt(jnp.sum(ql * ql, axis=-1, keepdims=True) * (1.0 / Q_LORA) + RMS_EPS) * gq_ref[...]
    qn = qn.astype(bf16)
    kvd = u_ref[:, c_kvd:c_kvd + KV_LORA]
    kvn = kvd * lax.rsqrt(jnp.mean(kvd * kvd, axis=-1, keepdims=True) + RMS_EPS) * gkv_ref[...]
    kv = _dot(kvn.astype(bf16), wkv_ref[...])
    cs = cs_ref[...]
    lane = lax.broadcasted_iota(jnp.int32, (1, LANES), 1)
    low = lane < QK_ROPE
    cos_t = jnp.where(low, cs, 0.0)
    sin_t = jnp.where(low, pltpu.roll(cs, QK_ROPE, 1), 0.0)
    kr = u_ref[:, c_kr:c_kr + LANES]
    prod = kr * cs
    k_pe = jnp.where(low, prod + pltpu.roll(prod, QK_ROPE, 1), 0.0).astype(bf16)
    q_nope = _dot(qn, wqn_ref[...])
    q_pe = _dot(qn, wqp_ref[...])
    q_pr = _dot(qn, wqr_ref[...])
    for h in range(n_heads):
        sl = slice(h * LANES, (h + 1) * LANES)
        q_o[h, :, 0:LANES] = (q_nope[:, sl] * MLA_Q_SCALE).astype(bf16)
        q_o[h, :, LANES:2 * LANES] = ((q_pe[:, sl] * cos_t + q_pr[:, sl] * sin_t) * MLA_Q_SCALE).astype(bf16)
        k_o[h, :, 0:LANES] = kv[:, sl].astype(bf16)
        k_o[h, :, LANES:2 * LANES] = k_pe
        v_o[h] = kv[:, n_heads * LANES + h * LANES:n_heads * LANES + (h + 1) * LANES].astype(bf16)


def mla_prepare(ub, cs, pw, cols):
    n = ub.shape[0]
    nh = pw["wq_nope"].shape[1] // LANES
    full = lambda a: pl.BlockSpec(a.shape, lambda i: (0,) * a.ndim)
    consts = [pw["gq"], pw["gkv"], pw["wq_nope"], pw["wq_pe"], pw["wq_pr"], pw["wkv"]]
    return pl.pallas_call(
        functools.partial(_mla_prep_body, n_heads=nh, **cols),
        grid=(n // ROW_TILE,),
        in_specs=[pl.BlockSpec((ROW_TILE, ub.shape[1]), lambda i: (i, 0)),
                  pl.BlockSpec((ROW_TILE, LANES), lambda i: (i, 0))] + [full(a) for a in consts],
        out_specs=[pl.BlockSpec((nh, ROW_TILE, 2 * LANES), lambda i: (0, i, 0)),
                   pl.BlockSpec((nh, ROW_TILE, 2 * LANES), lambda i: (0, i, 0)),
                   pl.BlockSpec((nh, ROW_TILE, LANES), lambda i: (0, i, 0))],
        out_shape=[jax.ShapeDtypeStruct((nh, n, 2 * LANES), bf16),
                   jax.ShapeDtypeStruct((nh, n, 2 * LANES), bf16),
                   jax.ShapeDtypeStruct((nh, n, LANES), bf16)],
        compiler_params=_params(("arbitrary",)),
        name="mla_prepare",
    )(ub, cs, *consts)


def _mla_attn_body(q_ref, k_ref, v_ref, o_ref, *, n_ctx, n_tok, tk, tq):
    i = pl.program_id(1)
    hs = range(MLA_HEADS_PER_STEP)
    subs = range(tq // ROW_TILE)

    def rows(sb):
        return slice(sb * ROW_TILE, (sb + 1) * ROW_TILE)

    def write(chains, o):
        for (h, sb), val in zip(chains, o):
            o_ref[rows(sb), h * V_DIM:(h + 1) * V_DIM] = val.astype(o_ref.dtype)

    def context_queries():
        chains = [(h, 0) for h in hs]
        each = lambda f: [f(n, h) for n, (h, _) in enumerate(chains)]
        s = each(lambda n, h: _dot_nt(q_ref[h, rows(0), :], k_ref[h, 0:n_ctx, :]))
        p = each(lambda n, h: jnp.exp2(s[n] - jnp.max(s[n], axis=-1, keepdims=True)))
        pv = each(lambda n, h: _dot(p[n].astype(bf16), v_ref[h, 0:n_ctx, :]))
        write(chains, each(lambda n, h: pv[n] / jnp.sum(p[n], axis=-1, keepdims=True)))

    def latent_queries(chains):
        each = lambda f: [f(n, h, sb) for n, (h, sb) in enumerate(chains)]
        q = each(lambda n, h, sb: q_ref[h, rows(sb), :])

        def step(c, carry):
            off = pl.multiple_of(c * tk, tk)
            nch = len(chains)
            s, out = [None] * nch, [None] * nch

            def finish(n):
                h = chains[n][0]
                m_old, l_old, acc_old = carry[n]
                m_new = jnp.maximum(m_old, jnp.max(s[n], axis=-1, keepdims=True))
                alpha = jnp.exp2(m_old - m_new)
                p = jnp.exp2(s[n] - m_new)
                l_new = alpha * l_old + jnp.sum(p, axis=-1, keepdims=True)
                pv = _dot(p.astype(bf16), v_ref[h, pl.ds(off, tk), :])
                out[n] = (m_new, l_new, alpha * acc_old + pv)

            for n in range(nch + MLA_CHAIN_LAG):
                if n < nch:
                    s[n] = _dot_nt(q[n], k_ref[chains[n][0], pl.ds(off, tk), :])
                if n >= MLA_CHAIN_LAG:
                    finish(n - MLA_CHAIN_LAG)
            return tuple(out)

        init = tuple((jnp.full((ROW_TILE, 1), NEG_BIG, f32), jnp.zeros((ROW_TILE, 1), f32),
                      jnp.zeros((ROW_TILE, V_DIM), f32)) for _ in chains)
        fin = lax.fori_loop(0, n_tok // tk, step, init)
        write(chains, [f[2] / f[1] for f in fin])

    @pl.when(i == 0)
    def _():
        context_queries()
        if len(subs) > 1:
            latent_queries([(h, sb) for h in hs for sb in subs[1:]])

    @pl.when(i > 0)
    def _():
        latent_queries([(h, sb) for h in hs for sb in subs])


def mla_attention(q, k, v, n_ctx):
    nh, n, dq = q.shape
    hps = MLA_HEADS_PER_STEP
    assert n_ctx == ROW_TILE
    tk = next(t for t in (768, 512, 256) if n % t == 0)
    tq = tk
    return pl.pallas_call(
        functools.partial(_mla_attn_body, n_ctx=n_ctx, n_tok=n, tk=tk, tq=tq),
        grid=(nh // hps, n // tq),
        in_specs=[
            pl.BlockSpec((hps, tq, dq), lambda h, i: (h, i, 0)),
            pl.BlockSpec((hps, n, dq), lambda h, i: (h, 0, 0)),
            pl.BlockSpec((hps, n, V_DIM), lambda h, i: (h, 0, 0)),
        ],
        out_specs=pl.BlockSpec((tq, hps * V_DIM), lambda h, i: (i, h)),
        out_shape=jax.ShapeDtypeStruct((n, nh * V_DIM), bf16),
        compiler_params=_params(("arbitrary", "arbitrary")),
        name="mla_attention",
    )(q, k, v)


def _na_body(q_ref, k_ref, v_ref, bias_ref, o_ref, *, n_ctx, n_rows):
    j = pl.program_id(1)
    lane = lax.broadcasted_iota(jnp.int32, (1, LANES), 1)
    low = lane < C_HEAD_DIM
    win = WIN_ROWS * GRID_W
    lanes = lambda lp: slice(lp * LANES, (lp + 1) * LANES)
    pairs = range(NA_PAIRS_PER_STEP)
    kc = [k_ref[0:n_ctx, lanes(lp)] for lp in pairs]
    vc = [v_ref[0:n_ctx, lanes(lp)] for lp in pairs]

    def head_mask(hh, x):
        return jnp.where(low if hh == 0 else ~low, x, jnp.zeros_like(x))

    @pl.when(j == 0)
    def _():
        units = [(lp, hh) for lp in pairs for hh in range(2)]
        each = lambda f: [f(n, lp, hh) for n, (lp, hh) in enumerate(units)]
        qm = each(lambda n, lp, hh: head_mask(hh, q_ref[:, lanes(lp)] * NA_SCALE))
        s = each(lambda n, lp, hh: _dot_nt(qm[n], kc[lp]))
        p = each(lambda n, lp, hh: jnp.exp(s[n] - jnp.max(s[n], axis=-1, keepdims=True)))
        o = each(lambda n, lp, hh: _dot(p[n].astype(bf16), vc[lp]) / jnp.sum(p[n], axis=-1, keepdims=True))
        for lp in pairs:
            o_ref[:, lanes(lp)] = jnp.where(low, o[2 * lp], o[2 * lp + 1]).astype(o_ref.dtype)

    @pl.when(j > 0)
    def _():
        grid_rows = range(ROW_TILE // GRID_W)
        units = [(il, lp, hh) for il in grid_rows for lp in pairs for hh in range(2)]
        per_row = lambda f: [f(il) for il in grid_rows]
        per_rp = lambda f: [[f(il, lp) for lp in pairs] for il in grid_rows]
        per_unit = lambda f: [f(n, il, lp, hh) for n, (il, lp, hh) in enumerate(units)]
        i = per_row(lambda il: (j - 1) * (ROW_TILE // GRID_W) + il)
        r0 = per_row(lambda il: jnp.clip(i[il] - WIN_ROWS // 2, 0, n_rows - WIN_ROWS))
        d0 = per_row(lambda il: r0[il] - i[il] + WIN_ROWS - 1)
        off = per_row(lambda il: pl.multiple_of(n_ctx + r0[il] * GRID_W, GRID_W))
        q2 = per_rp(lambda il, lp: q_ref[il * GRID_W:(il + 1) * GRID_W, lanes(lp)] * NA_SCALE)
        kw = per_rp(lambda il, lp: k_ref[pl.ds(off[il], win), lanes(lp)])
        vw = per_rp(lambda il, lp: v_ref[pl.ds(off[il], win), lanes(lp)])
        qm = per_unit(lambda n, il, lp, hh: head_mask(hh, q2[il][lp]))
        bias = lambda hd, d: jnp.concatenate([bias_ref[hd, d + 2 * jj] for jj in range(WIN_ROWS // 2)], axis=1)
        s_w = per_unit(lambda n, il, lp, hh: _dot_nt(qm[n], kw[il][lp]) + bias(2 * lp + hh, d0[il]))
        s_c = per_unit(lambda n, il, lp, hh: _dot_nt(qm[n], kc[lp]))
        m = per_unit(lambda n, il, lp, hh: jnp.maximum(jnp.max(s_w[n], axis=-1, keepdims=True),
                                                       jnp.max(s_c[n], axis=-1, keepdims=True)))
        p_w = per_unit(lambda n, il, lp, hh: jnp.exp(s_w[n] - m[n]))
        p_c = per_unit(lambda n, il, lp, hh: jnp.exp(s_c[n] - m[n]))
        l = per_unit(lambda n, il, lp, hh: jnp.sum(p_w[n], axis=-1, keepdims=True)
                     + jnp.sum(p_c[n], axis=-1, keepdims=True))
        o = per_unit(lambda n, il, lp, hh: (_dot(p_w[n].astype(bf16), vw[il][lp])
                                            + _dot(p_c[n].astype(bf16), vc[lp])) / l[n])
        for n in range(0, len(units), 2):
            il, lp, _ = units[n]
            o_ref[il * GRID_W:(il + 1) * GRID_W, lanes(lp)] = jnp.where(low, o[n], o[n + 1]).astype(o_ref.dtype)


def na_attention(uc, bias, n_ctx):
    n = uc.shape[0]
    width = uc.shape[1] // 3
    pps = NA_PAIRS_PER_STEP
    ngroup = width // (pps * LANES)
    n_rows = (n - n_ctx) // GRID_W
    return pl.pallas_call(
        functools.partial(_na_body, n_ctx=n_ctx, n_rows=n_rows),
        grid=(ngroup, n // ROW_TILE),
        in_specs=[
            pl.BlockSpec((ROW_TILE, pps * LANES), lambda p, j: (j, p)),
            pl.BlockSpec((n, pps * LANES), lambda p, j: (0, ngroup + p)),
            pl.BlockSpec((n, pps * LANES), lambda p, j: (0, 2 * ngroup + p)),
            pl.BlockSpec((2 * pps,) + bias.shape[1:], lambda p, j: (p, 0, 0, 0)),
        ],
        out_specs=pl.BlockSpec((ROW_TILE, pps * LANES), lambda p, j: (j, p)),
        out_shape=jax.ShapeDtypeStruct((n, width), bf16),
        compiler_params=_params(("arbitrary", "arbitrary")),
        name="na_attention",
    )(uc, uc, uc, bias)


def _rope_tables(n_ctx, seq):
    t = jnp.arange(seq)
    row = (t // GRID_W).astype(f32)
    col = (t % GRID_W).astype(f32)
    n_freq = QK_ROPE // 4
    inv = ROPE_BASE ** (-jnp.arange(n_freq, dtype=f32) / n_freq)
    ar, ac = row[:, None] * inv[None, :], col[:, None] * inv[None, :]
    cos = jnp.concatenate([jnp.cos(ar), jnp.cos(ar), jnp.cos(ac), jnp.cos(ac)], axis=1)
    sin = jnp.concatenate([-jnp.sin(ar), jnp.sin(ar), -jnp.sin(ac), jnp.sin(ac)], axis=1)
    lat = jnp.concatenate([cos, sin], axis=1)
    ctx = jnp.concatenate([jnp.ones((n_ctx, QK_ROPE), f32), jnp.zeros((n_ctx, QK_ROPE), f32)], axis=1)
    return jnp.concatenate([ctx, lat], axis=0)


def _pair_swap_perm():
    q = QK_ROPE // 4
    return np.concatenate([np.arange(q, 2 * q), np.arange(0, q), np.arange(3 * q, 4 * q), np.arange(2 * q, 3 * q)])


def _na_bias_tables(rpb):
    cols = np.arange(GRID_W)
    c0 = np.clip(cols - WIN_COLS // 2, 0, GRID_W - WIN_COLS)
    inside = (cols[None, :] >= c0[:, None]) & (cols[None, :] < c0[:, None] + WIN_COLS)
    rel = cols[None, :] - cols[:, None] + WIN_COLS - 1
    place = (np.arange(2 * WIN_COLS - 1)[:, None, None] == rel[None]) & inside[None]
    full = jnp.einsum("lhdk,kcq->lhdcq", rpb, jnp.asarray(place, f32), precision=lax.Precision.HIGHEST)
    full = full + jnp.asarray(np.where(inside, 0.0, NEG_BIG), f32)
    return jnp.concatenate([full[:, :, :-1], full[:, :, 1:]], axis=-1)


def _layer_weights(l, w_small, rwkv_conv, rwkv_w0, rwkv_w_up, rwkv_a0, rwkv_a_up, rwkv_g_up, rwkv_k_k, rwkv_k_a,
                   rwkv_r_k, mla_q_norm_g, mla_w_uq, mla_kv_norm_g, mla_w_ukv):
    d = w_small.shape[1]
    aw = rwkv_k_k.shape[1]
    nh_b = mla_w_ukv.shape[2] // (QK_NOPE + V_DIM)
    n_lora = 2 * LORA_W + 2 * LORA_A + LORA_G
    c_lora = 3 * aw
    c_mla = n_lora
    c_end = c_mla + Q_LORA + KV_LORA + QK_ROPE
    wi = w_small[l]
    perm = _pair_swap_perm()
    kr = wi[:, c_mla + Q_LORA + KV_LORA:c_end]
    w_b = jnp.concatenate([
        wi[:, :c_mla],
        wi[:, c_mla + Q_LORA:c_mla + Q_LORA + KV_LORA],
        wi[:, c_mla:c_mla + Q_LORA], jnp.zeros((d, 512 - Q_LORA), bf16),
        kr, kr[:, perm]], axis=1)
    cols = dict(c_kvd=n_lora, c_q=n_lora + KV_LORA, c_kr=n_lora + KV_LORA + 512)

    def lora_pad(w, start):
        r = w.shape[1]
        out = jnp.zeros((2, n_lora, aw), f32)
        for z in range(2):
            out = out.at[z, start + z * r:start + (z + 1) * r].set(w[z])
        return out.astype(bf16)

    g_up = jnp.zeros((n_lora, aw), f32).at[2 * LORA_W + 2 * LORA_A:].set(rwkv_g_up[l]).astype(bf16)
    head_id = np.arange(2 * LANES) // A_HEAD_DIM
    seg = jnp.asarray(head_id[:, None] == head_id[None, :], bf16)
    rwkv = dict(conv_rkv=rwkv_conv[l][:, :c_lora], conv_l=rwkv_conv[l][:, c_lora:c_lora + n_lora],
                w_up=lora_pad(rwkv_w_up[l], 0), a_up=lora_pad(rwkv_a_up[l], 2 * LORA_W), g_up=g_up,
                w0=rwkv_w0[l], a0=rwkv_a0[l], k_k=rwkv_k_k[l][None], k_a=rwkv_k_a[l][None],
                r_k=rwkv_r_k[l].reshape(1, aw), seg=seg)

    uq = mla_w_uq[l].reshape(Q_LORA, nh_b, QK_NOPE + QK_ROPE)
    uq = jnp.concatenate([uq, jnp.zeros((512 - Q_LORA, nh_b, QK_NOPE + QK_ROPE), f32)], axis=0)
    zpad = jnp.zeros((512, nh_b, LANES - QK_ROPE), f32)
    pe = uq[:, :, QK_NOPE:]
    ukv = mla_w_ukv[l].reshape(KV_LORA, nh_b, QK_NOPE + V_DIM)
    mla = dict(
        gq=jnp.concatenate([mla_q_norm_g[l], jnp.zeros((512 - Q_LORA,), f32)])[None],
        gkv=mla_kv_norm_g[l][None],
        wq_nope=uq[:, :, :QK_NOPE].reshape(512, nh_b * QK_NOPE).astype(bf16),
        wq_pe=jnp.concatenate([pe, zpad], axis=2).reshape(512, nh_b * LANES).astype(bf16),
        wq_pr=jnp.concatenate([pe[:, :, perm], zpad], axis=2).reshape(512, nh_b * LANES).astype(bf16),
        wkv=jnp.concatenate([ukv[:, :, :QK_NOPE].reshape(KV_LORA, -1), ukv[:, :, QK_NOPE:].reshape(KV_LORA, -1)],
                            axis=1).astype(bf16))
    return dict(w_b=w_b, cols=cols, rwkv=rwkv, mla=mla)


def kernel(x, c, ctx, c_ctx, ada_w, ada_b, norm_mix_g, norm_mlp_g, w_in, rwkv_conv, rwkv_w0, rwkv_w_up, rwkv_a0, rwkv_a_up, rwkv_g_up, rwkv_k_k, rwkv_k_a, rwkv_r_k, rwkv_ln_g, rwkv_ln_b, mla_q_norm_g, mla_w_uq, mla_kv_norm_g, mla_w_ukv, na_rpb, w_branch, w_out, mlp_w1, mlp_w2, final_norm_g):
    batch, seq, d = x.shape
    n_ctx = ctx.shape[1]
    depth = ada_w.shape[0]
    assert batch == 1 and n_ctx == ROW_TILE and seq % ROW_TILE == 0 and seq // GRID_W >= WIN_ROWS
    n_ctx_tiles = n_ctx // ROW_TILE

    xs = jnp.concatenate([ctx[0], x[0]], axis=0)
    cond = jnp.concatenate([c, c_ctx[None], jnp.zeros((6, d), f32)], axis=0)
    mods = ada_modulation(cond, ada_w, ada_b)
    cs = _rope_tables(n_ctx, seq)
    na_bias = _na_bias_tables(na_rpb)
    w_branch_b, w_out_b = w_branch.astype(bf16), w_out.astype(bf16)
    mlp_w1_b, mlp_w2_b = mlp_w1.astype(bf16), mlp_w2.astype(bf16)
    aw = rwkv_k_k.shape[1]
    n_b = w_in.shape[2] - 3 * aw - 3 * aw - N_BRANCH * d
    w_main, w_small = cast_w_in_groups(w_in, 3 * aw, n_b)

    for l in range(depth):
        lw = _layer_weights(l, w_small, rwkv_conv, rwkv_w0, rwkv_w_up, rwkv_a0, rwkv_a_up, rwkv_g_up, rwkv_k_k,
                            rwkv_k_a, rwkv_r_k, mla_q_norm_g, mla_w_uq, mla_kv_norm_g, mla_w_ukv)
        mods8 = mods[l]
        mods2 = mods8[:2, None, :]
        h = norm_modulate(xs, norm_mix_g[l], mods2, 0)
        u3 = matmul(h, w_main, f32, 1024, layer=l, col0=0, ncols=3 * aw)
        ub = matmul(h, lw["w_b"], f32, lw["w_b"].shape[1])
        uc = matmul(h, w_main, bf16, 1024, layer=l, col0=3 * aw, ncols=3 * aw)
        gates = matmul(h, w_main, bf16, 1024, sigmoid=True, layer=l, col0=6 * aw, ncols=N_BRANCH * d)

        r, v, kk, bonus, g, lwd, kd, bd = rwkv_prepare(u3, ub, lw["rwkv"], n_ctx)
        y = rwkv_chunk_scan(r, v, kk, lwd, kd, bd, n_ctx)
        ya = rwkv_readout(y, bonus, g, rwkv_ln_g[l], rwkv_ln_b[l], lw["rwkv"]["seg"][:LANES, :LANES])

        q_b, k_b, v_b = mla_prepare(ub, cs, lw["mla"], lw["cols"])
        yb = mla_attention(q_b, k_b, v_b, n_ctx)

        yc = na_attention(uc, na_bias[l], n_ctx)

        merged = merge_branches(ya, yb, yc, gates, w_branch_b, l)
        xs = matmul_gated_residual(merged, w_out_b, l, xs, mods8, 2, n_ctx)
        h2 = norm_modulate(xs, norm_mlp_g[l], mods2, 3)
        xs = mlp_gated_residual(h2, mlp_w1_b, mlp_w2_b, l, xs, mods8, 5, n_ctx)

    return final_norm(xs, final_norm_g, n_ctx_tiles)[None]
```

```python
import functools
import math

import jax
import jax.numpy as jnp
import numpy as np
from jax import lax
from jax.experimental import pallas as pl
from jax.experimental.pallas import tpu as pltpu

f32 = jnp.float32
bf16 = jnp.bfloat16

GRID_W = 64
RMS_EPS = 1e-6
N_BRANCH = 3
A_HEAD_DIM = 64
LORA_W = 96
LORA_A = 96
LORA_G = 256
GN_EPS = 64e-5
QK_NOPE = 128
QK_ROPE = 64
V_DIM = 128
Q_LORA = 448
KV_LORA = 128
MLA_SCALE = (QK_NOPE + QK_ROPE) ** -0.5
MLA_Q_SCALE = MLA_SCALE * math.log2(math.e)
ROPE_BASE = 10000.0
C_HEAD_DIM = 64
WIN_ROWS = 8
WIN_COLS = 16
NA_SCALE = C_HEAD_DIM ** -0.5
DECAY_SCALE = math.exp(-0.5)

LANES = 128
ROW_TILE = 256
CHUNK = 64
CHUNKS_PER_STEP = 2
PREP_TILE = 128
NA_PAIRS_PER_STEP = 2
MLA_CHAIN_LAG = 2
MLA_HEADS_PER_STEP = 2
VMEM_LIMIT = 48 * 1024 * 1024
NEG_BIG = -1e30


def _params(sem):
    return pltpu.CompilerParams(dimension_semantics=sem, vmem_limit_bytes=VMEM_LIMIT)


def _dot(a, b):
    return jnp.dot(a, b, preferred_element_type=f32)


def _dot_nt(a, b):
    return lax.dot_general(a, b, (((1,), (1,)), ((), ())), preferred_element_type=f32)


def _ada_body(s_ref, w_ref, b_ref, o_ref):
    s = s_ref[...]
    s = s * jax.nn.sigmoid(s)
    o_ref[...] = _dot(s, w_ref[...]) + b_ref[...]


def ada_modulation(cond, ada_w, ada_b):
    nl, d, n6 = ada_w.shape
    tn = 1024
    return pl.pallas_call(
        _ada_body,
        grid=(nl, n6 // tn),
        in_specs=[
            pl.BlockSpec((8, d), lambda l, j: (0, 0)),
            pl.BlockSpec((None, d, tn), lambda l, j: (l, 0, j)),
            pl.BlockSpec((None, 1, tn), lambda l, j: (l, 0, j)),
        ],
        out_specs=pl.BlockSpec((None, 8, tn), lambda l, j: (l, 0, j)),
        out_shape=jax.ShapeDtypeStruct((nl, 8, n6), f32),
        compiler_params=_params(("arbitrary", "arbitrary")),
        name="ada_modulation",
    )(cond, ada_w, ada_b.reshape(nl, 1, n6))


def _norm_mod_body(x_ref, g_ref, m_ref, o_ref, *, off, d):
    x = x_ref[...]
    y = x * lax.rsqrt(jnp.mean(x * x, axis=-1, keepdims=True) + RMS_EPS) * g_ref[...]
    shift = m_ref[:, off * d:(off + 1) * d]
    scale = m_ref[:, (off + 1) * d:(off + 2) * d]
    o_ref[...] = (y * (1.0 + scale) + shift).astype(o_ref.dtype)


def norm_modulate(x, g, mods2, off):
    n, d = x.shape
    return pl.pallas_call(
        functools.partial(_norm_mod_body, off=off, d=d),
        grid=(n // ROW_TILE,),
        in_specs=[
            pl.BlockSpec((ROW_TILE, d), lambda i: (i, 0)),
            pl.BlockSpec((1, d), lambda i: (0, 0)),
            pl.BlockSpec((None, 1, mods2.shape[-1]), lambda i: (jnp.where(i == 0, 1, 0), 0, 0)),
        ],
        out_specs=pl.BlockSpec((ROW_TILE, d), lambda i: (i, 0)),
        out_shape=jax.ShapeDtypeStruct((n, d), bf16),
        compiler_params=_params(("arbitrary",)),
        name="norm_modulate",
    )(x, g.reshape(1, d), mods2)


def _final_norm_body(x_ref, g_ref, o_ref):
    x = x_ref[...]
    o_ref[...] = x * lax.rsqrt(jnp.mean(x * x, axis=-1, keepdims=True) + RMS_EPS) * g_ref[...]


def final_norm(x, g, n_ctx_tiles):
    n, d = x.shape
    t = n - n_ctx_tiles * ROW_TILE
    return pl.pallas_call(
        _final_norm_body,
        grid=(t // ROW_TILE,),
        in_specs=[
            pl.BlockSpec((ROW_TILE, d), lambda i: (i + n_ctx_tiles, 0)),
            pl.BlockSpec((1, d), lambda i: (0, 0)),
        ],
        out_specs=pl.BlockSpec((ROW_TILE, d), lambda i: (i, 0)),
        out_shape=jax.ShapeDtypeStruct((t, d), f32),
        compiler_params=_params(("arbitrary",)),
        name="final_norm",
    )(x, g.reshape(1, d))


def _mm_body(a_ref, w_ref, o_ref, *, sigmoid):
    acc = _dot(a_ref[...], w_ref[...])
    o_ref[...] = (jax.nn.sigmoid(acc) if sigmoid else acc).astype(o_ref.dtype)


def _row_tile(n):
    for tm in (1056, 768, 512, 384, 256):
        if n % tm == 0:
            return tm
    raise ValueError(f"no row tile for {n} rows")


def matmul(a, w, out_dtype, tn, sigmoid=False, layer=None, col0=0, ncols=None):
    m, k = a.shape
    n = (w.shape[-1] - col0) if ncols is None else ncols
    tm = _row_tile(m)
    assert col0 % tn == 0 and n % tn == 0
    j0 = col0 // tn
    if layer is None:
        w_spec = pl.BlockSpec((k, tn), lambda i, j: (0, j0 + j))
    else:
        w_spec = pl.BlockSpec((None, k, tn), lambda i, j: (layer, 0, j0 + j))
    return pl.pallas_call(
        functools.partial(_mm_body, sigmoid=sigmoid),
        grid=(m // tm, n // tn),
        in_specs=[pl.BlockSpec((tm, k), lambda i, j: (i, 0)), w_spec],
        out_specs=pl.BlockSpec((tm, tn), lambda i, j: (i, j)),
        out_shape=jax.ShapeDtypeStruct((m, n), out_dtype),
        compiler_params=_params(("arbitrary", "arbitrary")),
        name="matmul",
    )(a, w)


def _cast_body(x_ref, o_ref):
    o_ref[...] = x_ref[...].astype(o_ref.dtype)


def cast_w_in_groups(w_in, mid_start, mid_cols):
    nl, d, c = w_in.shape
    tc = 2 * LANES
    assert mid_start % tc == 0 and mid_cols % tc == 0 and c % tc == 0
    j_mid, n_mid = mid_start // tc, mid_cols // tc

    def cast(n_blocks, col_block):
        return pl.pallas_call(
            _cast_body,
            grid=(nl, n_blocks),
            in_specs=[pl.BlockSpec((None, d, tc), lambda l, j: (l, 0, col_block(j)))],
            out_specs=pl.BlockSpec((None, d, tc), lambda l, j: (l, 0, j)),
            out_shape=jax.ShapeDtypeStruct((nl, d, n_blocks * tc), bf16),
            compiler_params=_params(("arbitrary", "arbitrary")),
            name="cast_w_in_groups",
        )(w_in)

    return (cast(c // tc - n_mid, lambda j: jnp.where(j < j_mid, j, j + n_mid)),
            cast(n_mid, lambda j: j_mid + j))


def _gate_rows(m_ref, tm, n_ctx):
    rows = pl.program_id(0) * tm + lax.broadcasted_iota(jnp.int32, (tm, 1), 0)
    return jnp.where(rows < n_ctx, m_ref[1:2, :], m_ref[0:1, :])


def _norm_mod_rows(x, g_ref, m_ref, shift_off, d, tm, n_ctx):
    rows = pl.program_id(0) * tm + lax.broadcasted_iota(jnp.int32, (tm, 1), 0)
    is_ctx = rows < n_ctx
    y = x * lax.rsqrt(jnp.mean(x * x, axis=-1, keepdims=True) + RMS_EPS) * g_ref[...]
    shift = jnp.where(is_ctx, m_ref[1:2, shift_off * d:(shift_off + 1) * d], m_ref[0:1, shift_off * d:(shift_off + 1) * d])
    scale = jnp.where(is_ctx, m_ref[1:2, (shift_off + 1) * d:(shift_off + 2) * d],
                      m_ref[0:1, (shift_off + 1) * d:(shift_off + 2) * d])
    return y * (1.0 + scale) + shift


def _out_proj_body(a_ref, w_ref, x_ref, m_ref, g_ref, o_ref, h_ref, *, tm, n_ctx, d, gate_off, norm_off):
    acc = _dot(a_ref[...], w_ref[...])
    rows = pl.program_id(0) * tm + lax.broadcasted_iota(jnp.int32, (tm, 1), 0)
    gate = jnp.where(rows < n_ctx, m_ref[1:2, gate_off * d:(gate_off + 1) * d], m_ref[0:1, gate_off * d:(gate_off + 1) * d])
    x_new = x_ref[...] + gate * acc
    o_ref[...] = x_new
    h_ref[...] = _norm_mod_rows(x_new, g_ref, m_ref, norm_off, d, tm, n_ctx).astype(h_ref.dtype)


def out_proj_residual_norm(a, w, layer, x, mods8, gate_off, g_next, norm_off, n_ctx):
    m, k = a.shape
    d = w.shape[-1]
    tm = 528 if m % 528 == 0 else ROW_TILE
    return pl.pallas_call(
        functools.partial(_out_proj_body, tm=tm, n_ctx=n_ctx, d=d, gate_off=gate_off, norm_off=norm_off),
        grid=(m // tm,),
        in_specs=[
            pl.BlockSpec((tm, k), lambda i: (i, 0)),
            pl.BlockSpec((None, k, d), lambda i: (layer, 0, 0)),
            pl.BlockSpec((tm, d), lambda i: (i, 0)),
            pl.BlockSpec(mods8.shape, lambda i: (0, 0)),
            pl.BlockSpec((1, d), lambda i: (0, 0)),
        ],
        out_specs=[pl.BlockSpec((tm, d), lambda i: (i, 0)), pl.BlockSpec((tm, d), lambda i: (i, 0))],
        out_shape=[jax.ShapeDtypeStruct((m, d), f32), jax.ShapeDtypeStruct((m, d), bf16)],
        compiler_params=_params(("arbitrary",)),
        name="out_proj_residual_norm",
    )(a, w, x, mods8, g_next.reshape(1, d))


def _merge_body(ya_ref, yb_ref, yc_ref, g0_ref, g1_ref, g2_ref, w_ref, o_ref):
    acc = g0_ref[...] * _dot(ya_ref[...], w_ref[0])
    acc = acc + g1_ref[...] * _dot(yb_ref[...], w_ref[1])
    acc = acc + g2_ref[...] * _dot(yc_ref[...], w_ref[2])
    o_ref[...] = acc.astype(o_ref.dtype)


def merge_branches(ya, yb, yc, gates, w_branch, layer, tn=1024):
    m, k = ya.shape
    d = w_branch.shape[-1]
    tm = 528 if m % 528 == 0 else ROW_TILE
    nj = d // tn
    y_spec = pl.BlockSpec((tm, k), lambda i, j: (i, 0))
    return pl.pallas_call(
        _merge_body,
        grid=(m // tm, nj),
        in_specs=[
            y_spec, y_spec, y_spec,
            pl.BlockSpec((tm, tn), lambda i, j: (i, j)),
            pl.BlockSpec((tm, tn), lambda i, j: (i, nj + j)),
            pl.BlockSpec((tm, tn), lambda i, j: (i, 2 * nj + j)),
            pl.BlockSpec((None, N_BRANCH, k, tn), lambda i, j: (layer, 0, 0, j)),
        ],
        out_specs=pl.BlockSpec((tm, tn), lambda i, j: (i, j)),
        out_shape=jax.ShapeDtypeStruct((m, d), bf16),
        compiler_params=_params(("arbitrary", "arbitrary")),
        name="merge_branches",
    )(ya, yb, yc, gates, gates, gates, w_branch)


def _mlp_body(h_ref, w1_ref, w2_ref, x_ref, m_ref, o_ref, acc_ref, *, tm, n_ctx):
    f = pl.program_id(1)

    @pl.when(f == 0)
    def _():
        acc_ref[...] = jnp.zeros_like(acc_ref)

    h1 = jnp.maximum(_dot(h_ref[...], w1_ref[...]), 0.0)
    acc_ref[...] += _dot((h1 * h1).astype(bf16), w2_ref[...])

    @pl.when(f == pl.num_programs(1) - 1)
    def _():
        o_ref[...] = x_ref[...] + _gate_rows(m_ref, tm, n_ctx) * acc_ref[...]


def mlp_gated_residual(h, w1, w2, layer, x, mods8, gate_off, n_ctx, tf=1024):
    m, d = h.shape
    dff = w1.shape[-1]
    tm = 528 if m % 528 == 0 else ROW_TILE
    return pl.pallas_call(
        functools.partial(_mlp_body, tm=tm, n_ctx=n_ctx),
        grid=(m // tm, dff // tf),
        in_specs=[
            pl.BlockSpec((tm, d), lambda i, f: (i, 0)),
            pl.BlockSpec((None, d, tf), lambda i, f: (layer, 0, f)),
            pl.BlockSpec((None, tf, d), lambda i, f: (layer, f, 0)),
            pl.BlockSpec((tm, d), lambda i, f: (i, 0)),
            pl.BlockSpec((8, d), lambda i, f: (0, gate_off)),
        ],
        out_specs=pl.BlockSpec((tm, d), lambda i, f: (i, 0)),
        out_shape=jax.ShapeDtypeStruct((m, d), f32),
        scratch_shapes=[pltpu.VMEM((tm, d), f32)],
        compiler_params=_params(("arbitrary", "arbitrary")),
        name="mlp_gated_residual",
    )(h, w1, w2, x, mods8)


def _seg_sum(x, e_ref):
    hi = x.astype(bf16)
    lo = (x - hi.astype(f32)).astype(bf16)
    e = e_ref[...]
    w = e.shape[0]
    parts = [_dot(hi[:, g:g + w], e) + _dot(lo[:, g:g + w], e) for g in range(0, x.shape[1], w)]
    return parts[0] if len(parts) == 1 else jnp.concatenate(parts, axis=1)


def _to_pairs(o_ref, val, lead=None):
    for p in range(val.shape[-1] // LANES):
        piece = val[:, p * LANES:(p + 1) * LANES]
        if lead is None:
            o_ref[p] = piece
        else:
            o_ref[lead, p] = piece


def _rwkv_prep_body(u_ref, up_ref, un_ref, l_ref, lp_ref, ln_ref, cw_ref, cwl_ref, wu_ref, au_ref, gu_ref,
                    w0_ref, a0_ref, kk_ref, ka_ref, rk_ref, e_ref,
                    r_o, v_o, kkn_o, bonus_o, g_o, lw_o, kd_o, bd_o, *, n_ctx_tiles, aw):
    i = pl.program_id(0)
    last = pl.num_programs(0) - 1
    left_zero = (i == 0) | (i == n_ctx_tiles)
    right_zero = (i == n_ctx_tiles - 1) | (i == last)
    rows = lax.broadcasted_iota(jnp.int32, (PREP_TILE, 1), 0)

    def conv(x_ref, xp_ref, xn_ref, w_ref):
        x = x_ref[...]
        prev_row = jnp.where(left_zero, 0.0, xp_ref[7:8, :])
        next_row = jnp.where(right_zero, 0.0, xn_ref[0:1, :])
        x_prev = jnp.where(rows == 0, prev_row, pltpu.roll(x, 1, 0))
        x_next = jnp.where(rows == PREP_TILE - 1, next_row, pltpu.roll(x, PREP_TILE - 1, 0))
        return w_ref[0:1, :] * x_prev + w_ref[1:2, :] * x + w_ref[2:3, :] * x_next

    y = conv(u_ref, up_ref, un_ref, cw_ref)
    r, k, v = y[:, :aw], y[:, aw:2 * aw], y[:, 2 * aw:]
    yl = conv(l_ref, lp_ref, ln_ref, cwl_ref)
    th = jnp.tanh(yl).astype(bf16)
    sg = jax.nn.sigmoid(yl).astype(bf16)
    ylb = yl.astype(bf16)

    _to_pairs(g_o, _dot(sg, gu_ref[...]))
    _to_pairs(r_o, r)
    _to_pairs(v_o, v.astype(bf16))

    kkr = k * kk_ref[...]
    norm = jnp.sqrt(_seg_sum(kkr * kkr, e_ref))
    kkn = kkr / jnp.maximum(norm, 1e-12)
    _to_pairs(kkn_o, kkn)

    ksum = None
    for dr in range(2):
        z = w0_ref[dr:dr + 1, :] + _dot(th, wu_ref[dr])
        _to_pairs(lw_o, -DECAY_SCALE * jax.nn.sigmoid(z), lead=dr)
        a = jax.nn.sigmoid(a0_ref[dr:dr + 1, :] + _dot(ylb, au_ref[dr]))
        kd = k * (1.0 + (a - 1.0) * ka_ref[...])
        _to_pairs(kd_o, kd, lead=dr)
        _to_pairs(bd_o, kkn * a, lead=dr)
        ksum = kd if ksum is None else ksum + kd
    rk = _seg_sum(r * (0.5 * ksum) * rk_ref[...], e_ref)
    _to_pairs(bonus_o, rk * v)


def rwkv_prepare(u3, ub, pw, n_ctx):
    n = u3.shape[0]
    aw = u3.shape[1] // 3
    npair = aw // LANES
    lw = pw["conv_l"].shape[1]
    tpb = PREP_TILE // 8
    nb8 = n // 8

    def prev_map(i):
        return (jnp.maximum(i * tpb - 1, 0), 0)

    def next_map(i):
        return (jnp.minimum((i + 1) * tpb, nb8 - 1), 0)

    full = lambda a: pl.BlockSpec(a.shape, lambda i: (0,) * a.ndim)
    pm = jax.ShapeDtypeStruct((npair, n, LANES), f32)
    pm2 = jax.ShapeDtypeStruct((2, npair, n, LANES), f32)
    pm_bf16 = jax.ShapeDtypeStruct((npair, n, LANES), bf16)
    pm_spec = pl.BlockSpec((npair, PREP_TILE, LANES), lambda i: (0, i, 0))
    pm2_spec = pl.BlockSpec((2, npair, PREP_TILE, LANES), lambda i: (0, 0, i, 0))
    consts = [pw["conv_rkv"], pw["conv_l"], pw["w_up"], pw["a_up"], pw["g_up"], pw["w0"], pw["a0"],
              pw["k_k"], pw["k_a"], pw["r_k"], pw["seg"]]
    return pl.pallas_call(
        functools.partial(_rwkv_prep_body, n_ctx_tiles=n_ctx // PREP_TILE, aw=aw),
        grid=(n // PREP_TILE,),
        in_specs=[
            pl.BlockSpec((PREP_TILE, 3 * aw), lambda i: (i, 0)),
            pl.BlockSpec((8, 3 * aw), prev_map),
            pl.BlockSpec((8, 3 * aw), next_map),
            pl.BlockSpec((PREP_TILE, lw), lambda i: (i, 0)),
            pl.BlockSpec((8, lw), prev_map),
            pl.BlockSpec((8, lw), next_map),
        ] + [full(a) for a in consts],
        out_specs=[pm_spec] * 5 + [pm2_spec] * 3,
        out_shape=[pm, pm_bf16] + [pm] * 3 + [pm2] * 3,
        compiler_params=_params(("arbitrary",)),
        name="rwkv_prepare",
    )(u3, u3, u3, ub, ub, ub, *consts)


def _dot_tn(a, b):
    return lax.dot_general(a, b, (((0,), (0,)), ((), ())), preferred_element_type=f32)


def _rwkv_chunk_body(r_ref, v_ref, kk_ref, lw_ref, k_ref, b_ref, y_ref, h_ref, *, n_pairs):
    d = pl.program_id(0)
    j = pl.program_id(1)
    c = CHUNK
    c2 = 2 * c

    @pl.when(j == 0)
    def _():
        h_ref[...] = jnp.zeros_like(h_ref)

    sign = jnp.where(d == 0, 1, -1)
    ti = lax.broadcasted_iota(jnp.int32, (c, c), 0)
    si = lax.broadcasted_iota(jnp.int32, (c, c), 1)
    incl_c = jnp.where((ti - si) * sign >= 0, 1.0, 0.0).astype(bf16)
    t2 = lax.broadcasted_iota(jnp.int32, (c2, c2), 0)
    s2 = lax.broadcasted_iota(jnp.int32, (c2, c2), 1)
    ahead = (t2 - s2) * sign
    blk = lambda n: (t2 // n) == (s2 // n)
    incl = blk(c) & (ahead >= 0)
    strict = blk(c) & (ahead > 0)
    eye = jnp.where(t2 == s2, 1.0, 0.0)
    inv_levels = []
    n = 2
    while n < c:
        inv_levels.append(strict & blk(2 * n) & ~blk(n))
        n *= 2
    low = lax.broadcasted_iota(jnp.int32, (1, LANES), 1) < A_HEAD_DIM

    def stack(x):
        return jnp.concatenate([jnp.where(low, x, 0.0), jnp.where(low, 0.0, x)], axis=0).astype(bf16)

    pairs = range(n_pairs)
    first = jnp.where(d == 0, 0, CHUNKS_PER_STEP - 1)
    offs = [pl.multiple_of((first + sign * sc) * c, c) for sc in range(CHUNKS_PER_STEP)]
    units = [(sc, p) for sc in range(CHUNKS_PER_STEP) for p in pairs]
    each = lambda f: [f(sc, p) for sc, p in units]
    at = lambda sc, p: sc * n_pairs + p
    ld = lambda ref, sc, p: ref[p, pl.ds(offs[sc], c), :]
    lw = each(lambda sc, p: ld(lw_ref, sc, p))
    lw_hi = [x.astype(bf16) for x in lw]
    lw_lo = [(x - hi.astype(f32)).astype(bf16) for x, hi in zip(lw, lw_hi)]
    cum = [_dot(incl_c, hi) + _dot(incl_c, lo) for hi, lo in zip(lw_hi, lw_lo)]
    tot = [jnp.sum(x, axis=0, keepdims=True) for x in lw]
    e_neg = [jnp.exp(-x) for x in cum]
    e_end = [jnp.exp(t - x) for t, x in zip(tot, cum)]
    r_s = each(lambda sc, p: stack(ld(r_ref, sc, p) * jnp.exp(cum[at(sc, p)])))
    kk_s = each(lambda sc, p: stack(ld(kk_ref, sc, p) * jnp.exp(cum[at(sc, p)] - lw[at(sc, p)])))
    kb_s = each(lambda sc, p: jnp.concatenate([stack(ld(b_ref, sc, p) * e_neg[at(sc, p)]),
                                               stack(ld(k_ref, sc, p) * e_neg[at(sc, p)])], axis=0))
    kbd_s = each(lambda sc, p: jnp.concatenate([stack(ld(k_ref, sc, p) * e_end[at(sc, p)]),
                                                stack(ld(b_ref, sc, p) * e_end[at(sc, p)])], axis=0))
    v_s = each(lambda sc, p: stack(ld(v_ref, sc, p)))
    a1 = [_dot_nt(a, b) for a, b in zip(kk_s, kb_s)]
    a2 = [_dot_nt(a, b) for a, b in zip(r_s, kb_s)]
    a_kb = [jnp.where(strict, a[:, :c2], 0.0) for a in a1]
    a_kk = [jnp.where(strict, a[:, c2:], 0.0).astype(bf16) for a in a1]
    a_rb = [jnp.where(incl, a[:, :c2], 0.0).astype(bf16) for a in a2]
    a_rk = [jnp.where(incl, a[:, c2:], 0.0).astype(bf16) for a in a2]
    t_inv = [eye - jnp.where(blk(2), a, 0.0) for a in a_kb]
    for m in inv_levels:
        t_b = [t.astype(bf16) for t in t_inv]
        x = [_dot(t, jnp.where(m, a, 0.0).astype(bf16)).astype(bf16) for t, a in zip(t_b, a_kb)]
        t_inv = [t - _dot(xx, tb) for t, xx, tb in zip(t_inv, x, t_b)]
    g = [_dot(t.astype(bf16), jnp.concatenate([kk, akk], axis=1)).astype(bf16)
         for t, kk, akk in zip(t_inv, kk_s, a_kk)]
    w_col = [jnp.sum(eye * jnp.exp(t), axis=1, keepdims=True) for t in tot]
    h = [h_ref[p] for p in pairs]
    for sc in range(CHUNKS_PER_STEP):
        ix = [at(sc, p) for p in pairs]
        h_b = [x.astype(bf16) for x in h]
        u = [_dot(g[i], jnp.concatenate([h_b[p], v_s[i]], axis=0)).astype(bf16) for p, i in enumerate(ix)]
        y = [_dot(jnp.concatenate([r_s[i], a_rk[i], -a_rb[i]], axis=1), jnp.concatenate([h_b[p], v_s[i], u[p]], axis=0))
             for p, i in enumerate(ix)]
        h1 = [_dot_tn(kbd_s[i], jnp.concatenate([v_s[i], -u[p]], axis=0)) for p, i in enumerate(ix)]
        for p in pairs:
            y_ref[p, pl.ds(offs[sc], c), :] = y[p][:c] + y[p][c:]
        h = [h[p] * w_col[i] + h1[p] for p, i in enumerate(ix)]
    for p in pairs:
        h_ref[p] = h[p]


def rwkv_chunk_scan(r, v, kk, lw, kd, bd, n_ctx):
    npair, n, _ = r.shape
    blk_rows = CHUNKS_PER_STEP * CHUNK
    nc = n // blk_rows
    ncc = n_ctx // blk_rows

    def cidx(d, j):
        bwd = jnp.where(j < ncc, ncc - 1 - j, nc - 1 - (j - ncc))
        return jnp.where(d == 0, j, bwd)

    shared = pl.BlockSpec((npair, blk_rows, LANES), lambda d, j: (0, cidx(d, j), 0))
    per_dir = pl.BlockSpec((None, npair, blk_rows, LANES), lambda d, j: (d, 0, cidx(d, j), 0))
    return pl.pallas_call(
        functools.partial(_rwkv_chunk_body, n_pairs=npair),
        grid=(2, nc),
        in_specs=[shared, shared, shared, per_dir, per_dir, per_dir],
        out_specs=per_dir,
        out_shape=jax.ShapeDtypeStruct((2, npair, n, LANES), f32),
        scratch_shapes=[pltpu.VMEM((npair, LANES, LANES), f32)],
        compiler_params=_params(("arbitrary", "arbitrary")),
        name="rwkv_chunk_scan",
    )(r, v, kk, lw, kd, bd)


def _rwkv_readout_body(y_ref, bonus_ref, g_ref, lng_ref, lnb_ref, e_ref, o_ref, *, n_pairs):
    inv = 1.0 / A_HEAD_DIM
    for p in range(n_pairs):
        y = y_ref[0, p] + y_ref[1, p]
        yc = y - _seg_sum(y, e_ref) * inv
        var = _seg_sum(yc * yc, e_ref) * inv
        yn = yc * lax.rsqrt(var + GN_EPS) * lng_ref[p] + lnb_ref[p]
        o_ref[:, p * LANES:(p + 1) * LANES] = ((yn + bonus_ref[p]) * g_ref[p]).astype(o_ref.dtype)


def rwkv_readout(y, bonus, g, ln_g, ln_b, seg_pair):
    _, npair, n, _ = y.shape
    pm_spec = pl.BlockSpec((npair, ROW_TILE, LANES), lambda i: (0, i, 0))
    vec_spec = pl.BlockSpec((npair, 1, LANES), lambda i: (0, 0, 0))
    return pl.pallas_call(
        functools.partial(_rwkv_readout_body, n_pairs=npair),
        grid=(n // ROW_TILE,),
        in_specs=[pl.BlockSpec((2, npair, ROW_TILE, LANES), lambda i: (0, 0, i, 0)), pm_spec, pm_spec,
                  vec_spec, vec_spec, pl.BlockSpec((LANES, LANES), lambda i: (0, 0))],
        out_specs=pl.BlockSpec((ROW_TILE, npair * LANES), lambda i: (i, 0)),
        out_shape=jax.ShapeDtypeStruct((n, npair * LANES), bf16),
        compiler_params=_params(("arbitrary",)),
        name="rwkv_readout",
    )(y, bonus, g, ln_g.reshape(npair, 1, LANES), ln_b.reshape(npair, 1, LANES), seg_pair)


def _mla_prep_body(u_ref, cs_ref, gq_ref, gkv_ref, wqn_ref, wqp_ref, wqr_ref, wkv_ref, q_o, k_o, v_o,
                   *, n_heads, c_kvd, c_q, c_kr):
    ql = u_ref[:, c_q:c_q + 512]
    qn = ql * lax.rsqrt(jnp.sum(ql * ql, axis=-1, keepdims=True) * (1.0 / Q_LORA) + RMS_EPS) * gq_ref[...]
    qn = qn.astype(bf16)
    kvd = u_ref[:, c_kvd:c_kvd + KV_LORA]
    kvn = kvd * lax.rsqrt(jnp.mean(kvd * kvd, axis=-1, keepdims=True) + RMS_EPS) * gkv_ref[...]
    kv = _dot(kvn.astype(bf16), wkv_ref[...])
    cs = cs_ref[...]
    lane = lax.broadcasted_iota(jnp.int32, (1, LANES), 1)
    low = lane < QK_ROPE
    cos_t = jnp.where(low, cs, 0.0)
    sin_t = jnp.where(low, pltpu.roll(cs, QK_ROPE, 1), 0.0)
    kr = u_ref[:, c_kr:c_kr + LANES]
    prod = kr * cs
    k_pe = jnp.where(low, prod + pltpu.roll(prod, QK_ROPE, 1), 0.0).astype(bf16)
    q_nope = _dot(qn, wqn_ref[...])
    q_pe = _dot(qn, wqp_ref[...])
    q_pr = _dot(qn, wqr_ref[...])
    for h in range(n_heads):
        sl = slice(h * LANES, (h + 1) * LANES)
        q_o[h, :, 0:LANES] = (q_nope[:, sl] * MLA_Q_SCALE).astype(bf16)
        q_o[h, :, LANES:2 * LANES] = ((q_pe[:, sl] * cos_t + q_pr[:, sl] * sin_t) * MLA_Q_SCALE).astype(bf16)
        k_o[h, :, 0:LANES] = kv[:, sl].astype(bf16)
        k_o[h, :, LANES:2 * LANES] = k_pe
        v_o[h] = kv[:, n_heads * LANES + h * LANES:n_heads * LANES + (h + 1) * LANES].astype(bf16)


def mla_prepare(ub, cs, pw, cols):
    n = ub.shape[0]
    nh = pw["wq_nope"].shape[1] // LANES
    full = lambda a: pl.BlockSpec(a.shape, lambda i: (0,) * a.ndim)
    consts = [pw["gq"], pw["gkv"], pw["wq_nope"], pw["wq_pe"], pw["wq_pr"], pw["wkv"]]
    return pl.pallas_call(
        functools.partial(_mla_prep_body, n_heads=nh, **cols),
        grid=(n // ROW_TILE,),
        in_specs=[pl.BlockSpec((ROW_TILE, ub.shape[1]), lambda i: (i, 0)),
                  pl.BlockSpec((ROW_TILE, LANES), lambda i: (i, 0))] + [full(a) for a in consts],
        out_specs=[pl.BlockSpec((nh, ROW_TILE, 2 * LANES), lambda i: (0, i, 0)),
                   pl.BlockSpec((nh, ROW_TILE, 2 * LANES), lambda i: (0, i, 0)),
                   pl.BlockSpec((nh, ROW_TILE, LANES), lambda i: (0, i, 0))],
        out_shape=[jax.ShapeDtypeStruct((nh, n, 2 * LANES), bf16),
                   jax.ShapeDtypeStruct((nh, n, 2 * LANES), bf16),
                   jax.ShapeDtypeStruct((nh, n, LANES), bf16)],
        compiler_params=_params(("arbitrary",)),
        name="mla_prepare",
    )(ub, cs, *consts)


def _mla_attn_body(q_ref, k_ref, v_ref, o_ref, *, n_ctx, n_tok, tk, tq):
    i = pl.program_id(1)
    hs = range(MLA_HEADS_PER_STEP)
    subs = range(tq // ROW_TILE)

    def rows(sb):
        return slice(sb * ROW_TILE, (sb + 1) * ROW_TILE)

    def write(chains, o):
        for (h, sb), val in zip(chains, o):
            o_ref[rows(sb), h * V_DIM:(h + 1) * V_DIM] = val.astype(o_ref.dtype)

    def context_queries():
        chains = [(h, 0) for h in hs]
        each = lambda f: [f(n, h) for n, (h, _) in enumerate(chains)]
        s = each(lambda n, h: _dot_nt(q_ref[h, rows(0), :], k_ref[h, 0:n_ctx, :]))
        p = each(lambda n, h: jnp.exp2(s[n] - jnp.max(s[n], axis=-1, keepdims=True)))
        pv = each(lambda n, h: _dot(p[n].astype(bf16), v_ref[h, 0:n_ctx, :]))
        write(chains, each(lambda n, h: pv[n] / jnp.sum(p[n], axis=-1, keepdims=True)))

    def latent_queries(chains):
        each = lambda f: [f(n, h, sb) for n, (h, sb) in enumerate(chains)]
        q = each(lambda n, h, sb: q_ref[h, rows(sb), :])

        def step(c, carry):
            off = pl.multiple_of(c * tk, tk)
            nch = len(chains)
            s, out = [None] * nch, [None] * nch

            def finish(n):
                h = chains[n][0]
                m_old, l_old, acc_old = carry[n]
                m_new = jnp.maximum(m_old, jnp.max(s[n], axis=-1, keepdims=True))
                alpha = jnp.exp2(m_old - m_new)
                p = jnp.exp2(s[n] - m_new)
                l_new = alpha * l_old + jnp.sum(p, axis=-1, keepdims=True)
                pv = _dot(p.astype(bf16), v_ref[h, pl.ds(off, tk), :])
                out[n] = (m_new, l_new, alpha * acc_old + pv)

            for n in range(nch + MLA_CHAIN_LAG):
                if n < nch:
                    s[n] = _dot_nt(q[n], k_ref[chains[n][0], pl.ds(off, tk), :])
                if n >= MLA_CHAIN_LAG:
                    finish(n - MLA_CHAIN_LAG)
            return tuple(out)

        init = tuple((jnp.full((ROW_TILE, 1), NEG_BIG, f32), jnp.zeros((ROW_TILE, 1), f32),
                      jnp.zeros((ROW_TILE, V_DIM), f32)) for _ in chains)
        fin = lax.fori_loop(0, n_tok // tk, step, init)
        write(chains, [f[2] / f[1] for f in fin])

    @pl.when(i == 0)
    def _():
        context_queries()
        if len(subs) > 1:
            latent_queries([(h, sb) for h in hs for sb in subs[1:]])

    @pl.when(i > 0)
    def _():
        latent_queries([(h, sb) for h in hs for sb in subs])


def mla_attention(q, k, v, n_ctx):
    nh, n, dq = q.shape
    hps = MLA_HEADS_PER_STEP
    assert n_ctx == ROW_TILE
    tk = next(t for t in (768, 512, 256) if n % t == 0)
    tq = tk
    return pl.pallas_call(
        functools.partial(_mla_attn_body, n_ctx=n_ctx, n_tok=n, tk=tk, tq=tq),
        grid=(nh // hps, n // tq),
        in_specs=[
            pl.BlockSpec((hps, tq, dq), lambda h, i: (h, i, 0)),
            pl.BlockSpec((hps, n, dq), lambda h, i: (h, 0, 0)),
            pl.BlockSpec((hps, n, V_DIM), lambda h, i: (h, 0, 0)),
        ],
        out_specs=pl.BlockSpec((tq, hps * V_DIM), lambda h, i: (i, h)),
        out_shape=jax.ShapeDtypeStruct((n, nh * V_DIM), bf16),
        compiler_params=_params(("arbitrary", "arbitrary")),
        name="mla_attention",
    )(q, k, v)


def _na_body(q_ref, k_ref, v_ref, bias_ref, o_ref, *, n_ctx, n_rows):
    j = pl.program_id(1)
    lane = lax.broadcasted_iota(jnp.int32, (1, LANES), 1)
    low = lane < C_HEAD_DIM
    win = WIN_ROWS * GRID_W
    lanes = lambda lp: slice(lp * LANES, (lp + 1) * LANES)
    pairs = range(NA_PAIRS_PER_STEP)
    kc = [k_ref[0:n_ctx, lanes(lp)] for lp in pairs]
    vc = [v_ref[0:n_ctx, lanes(lp)] for lp in pairs]

    def head_mask(hh, x):
        return jnp.where(low if hh == 0 else ~low, x, jnp.zeros_like(x))

    @pl.when(j == 0)
    def _():
        units = [(lp, hh) for lp in pairs for hh in range(2)]
        each = lambda f: [f(n, lp, hh) for n, (lp, hh) in enumerate(units)]
        qm = each(lambda n, lp, hh: head_mask(hh, q_ref[:, lanes(lp)] * NA_SCALE))
        s = each(lambda n, lp, hh: _dot_nt(qm[n], kc[lp]))
        p = each(lambda n, lp, hh: jnp.exp(s[n] - jnp.max(s[n], axis=-1, keepdims=True)))
        o = each(lambda n, lp, hh: _dot(p[n].astype(bf16), vc[lp]) / jnp.sum(p[n], axis=-1, keepdims=True))
        for lp in pairs:
            o_ref[:, lanes(lp)] = jnp.where(low, o[2 * lp], o[2 * lp + 1]).astype(o_ref.dtype)

    @pl.when(j > 0)
    def _():
        grid_rows = range(ROW_TILE // GRID_W)
        units = [(il, lp, hh) for il in grid_rows for lp in pairs for hh in range(2)]
        per_row = lambda f: [f(il) for il in grid_rows]
        per_rp = lambda f: [[f(il, lp) for lp in pairs] for il in grid_rows]
        per_unit = lambda f: [f(n, il, lp, hh) for n, (il, lp, hh) in enumerate(units)]
        i = per_row(lambda il: (j - 1) * (ROW_TILE // GRID_W) + il)
        r0 = per_row(lambda il: jnp.clip(i[il] - WIN_ROWS // 2, 0, n_rows - WIN_ROWS))
        d0 = per_row(lambda il: r0[il] - i[il] + WIN_ROWS - 1)
        off = per_row(lambda il: pl.multiple_of(n_ctx + r0[il] * GRID_W, GRID_W))
        q2 = per_rp(lambda il, lp: q_ref[il * GRID_W:(il + 1) * GRID_W, lanes(lp)] * NA_SCALE)
        kw = per_rp(lambda il, lp: k_ref[pl.ds(off[il], win), lanes(lp)])
        vw = per_rp(lambda il, lp: v_ref[pl.ds(off[il], win), lanes(lp)])
        qm = per_unit(lambda n, il, lp, hh: head_mask(hh, q2[il][lp]))
        bias = lambda hd, d: jnp.concatenate([bias_ref[hd, d + 2 * jj] for jj in range(WIN_ROWS // 2)], axis=1)
        s_w = per_unit(lambda n, il, lp, hh: _dot_nt(qm[n], kw[il][lp]) + bias(2 * lp + hh, d0[il]))
        s_c = per_unit(lambda n, il, lp, hh: _dot_nt(qm[n], kc[lp]))
        m = per_unit(lambda n, il, lp, hh: jnp.maximum(jnp.max(s_w[n], axis=-1, keepdims=True),
                                                       jnp.max(s_c[n], axis=-1, keepdims=True)))
        p_w = per_unit(lambda n, il, lp, hh: jnp.exp(s_w[n] - m[n]))
        p_c = per_unit(lambda n, il, lp, hh: jnp.exp(s_c[n] - m[n]))
        l = per_unit(lambda n, il, lp, hh: jnp.sum(p_w[n], axis=-1, keepdims=True)
                     + jnp.sum(p_c[n], axis=-1, keepdims=True))
        o = per_unit(lambda n, il, lp, hh: (_dot(p_w[n].astype(bf16), vw[il][lp])
                                            + _dot(p_c[n].astype(bf16), vc[lp])) / l[n])
        for n in range(0, len(units), 2):
            il, lp, _ = units[n]
            o_ref[il * GRID_W:(il + 1) * GRID_W, lanes(lp)] = jnp.where(low, o[n], o[n + 1]).astype(o_ref.dtype)


def na_attention(uc, bias, n_ctx):
    n = uc.shape[0]
    width = uc.shape[1] // 3
    pps = NA_PAIRS_PER_STEP
    ngroup = width // (pps * LANES)
    n_rows = (n - n_ctx) // GRID_W
    return pl.pallas_call(
        functools.partial(_na_body, n_ctx=n_ctx, n_rows=n_rows),
        grid=(ngroup, n // ROW_TILE),
        in_specs=[
            pl.BlockSpec((ROW_TILE, pps * LANES), lambda p, j: (j, p)),
            pl.BlockSpec((n, pps * LANES), lambda p, j: (0, ngroup + p)),
            pl.BlockSpec((n, pps * LANES), lambda p, j: (0, 2 * ngroup + p)),
            pl.BlockSpec((2 * pps,) + bias.shape[1:], lambda p, j: (p, 0, 0, 0)),
        ],
        out_specs=pl.BlockSpec((ROW_TILE, pps * LANES), lambda p, j: (j, p)),
        out_shape=jax.ShapeDtypeStruct((n, width), bf16),
        compiler_params=_params(("arbitrary", "arbitrary")),
        name="na_attention",
    )(uc, uc, uc, bias)


def _rope_tables(n_ctx, seq):
    t = jnp.arange(seq)
    row = (t // GRID_W).astype(f32)
    col = (t % GRID_W).astype(f32)
    n_freq = QK_ROPE // 4
    inv = ROPE_BASE ** (-jnp.arange(n_freq, dtype=f32) / n_freq)
    ar, ac = row[:, None] * inv[None, :], col[:, None] * inv[None, :]
    cos = jnp.concatenate([jnp.cos(ar), jnp.cos(ar), jnp.cos(ac), jnp.cos(ac)], axis=1)
    sin = jnp.concatenate([-jnp.sin(ar), jnp.sin(ar), -jnp.sin(ac), jnp.sin(ac)], axis=1)
    lat = jnp.concatenate([cos, sin], axis=1)
    ctx = jnp.concatenate([jnp.ones((n_ctx, QK_ROPE), f32), jnp.zeros((n_ctx, QK_ROPE), f32)], axis=1)
    return jnp.concatenate([ctx, lat], axis=0)


def _pair_swap_perm():
    q = QK_ROPE // 4
    return np.concatenate([np.arange(q, 2 * q), np.arange(0, q), np.arange(3 * q, 4 * q), np.arange(2 * q, 3 * q)])


def _na_bias_tables(rpb):
    cols = np.arange(GRID_W)
    c0 = np.clip(cols - WIN_COLS // 2, 0, GRID_W - WIN_COLS)
    inside = (cols[None, :] >= c0[:, None]) & (cols[None, :] < c0[:, None] + WIN_COLS)
    rel = cols[None, :] - cols[:, None] + WIN_COLS - 1
    place = (np.arange(2 * WIN_COLS - 1)[:, None, None] == rel[None]) & inside[None]
    full = jnp.einsum("lhdk,kcq->lhdcq", rpb, jnp.asarray(place, f32), precision=lax.Precision.HIGHEST)
    full = full + jnp.asarray(np.where(inside, 0.0, NEG_BIG), f32)
    return jnp.concatenate([full[:, :, :-1], full[:, :, 1:]], axis=-1)


def _layer_weights(l, w_small, rwkv_conv, rwkv_w0, rwkv_w_up, rwkv_a0, rwkv_a_up, rwkv_g_up, rwkv_k_k, rwkv_k_a,
                   rwkv_r_k, mla_q_norm_g, mla_w_uq, mla_kv_norm_g, mla_w_ukv):
    d = w_small.shape[1]
    aw = rwkv_k_k.shape[1]
    nh_b = mla_w_ukv.shape[2] // (QK_NOPE + V_DIM)
    n_lora = 2 * LORA_W + 2 * LORA_A + LORA_G
    c_lora = 3 * aw
    c_mla = n_lora
    c_end = c_mla + Q_LORA + KV_LORA + QK_ROPE
    wi = w_small[l]
    perm = _pair_swap_perm()
    kr = wi[:, c_mla + Q_LORA + KV_LORA:c_end]
    w_b = jnp.concatenate([
        wi[:, :c_mla],
        wi[:, c_mla + Q_LORA:c_mla + Q_LORA + KV_LORA],
        wi[:, c_mla:c_mla + Q_LORA], jnp.zeros((d, 512 - Q_LORA), bf16),
        kr, kr[:, perm]], axis=1)
    cols = dict(c_kvd=n_lora, c_q=n_lora + KV_LORA, c_kr=n_lora + KV_LORA + 512)

    def lora_pad(w, start):
        r = w.shape[1]
        out = jnp.zeros((2, n_lora, aw), f32)
        for z in range(2):
            out = out.at[z, start + z * r:start + (z + 1) * r].set(w[z])
        return out.astype(bf16)

    g_up = jnp.zeros((n_lora, aw), f32).at[2 * LORA_W + 2 * LORA_A:].set(rwkv_g_up[l]).astype(bf16)
    head_id = np.arange(2 * LANES) // A_HEAD_DIM
    seg = jnp.asarray(head_id[:, None] == head_id[None, :], bf16)
    rwkv = dict(conv_rkv=rwkv_conv[l][:, :c_lora], conv_l=rwkv_conv[l][:, c_lora:c_lora + n_lora],
                w_up=lora_pad(rwkv_w_up[l], 0), a_up=lora_pad(rwkv_a_up[l], 2 * LORA_W), g_up=g_up,
                w0=rwkv_w0[l], a0=rwkv_a0[l], k_k=rwkv_k_k[l][None], k_a=rwkv_k_a[l][None],
                r_k=rwkv_r_k[l].reshape(1, aw), seg=seg)

    uq = mla_w_uq[l].reshape(Q_LORA, nh_b, QK_NOPE + QK_ROPE)
    uq = jnp.concatenate([uq, jnp.zeros((512 - Q_LORA, nh_b, QK_NOPE + QK_ROPE), f32)], axis=0)
    zpad = jnp.zeros((512, nh_b, LANES - QK_ROPE), f32)
    pe = uq[:, :, QK_NOPE:]
    ukv = mla_w_ukv[l].reshape(KV_LORA, nh_b, QK_NOPE + V_DIM)
    mla = dict(
        gq=jnp.concatenate([mla_q_norm_g[l], jnp.zeros((512 - Q_LORA,), f32)])[None],
        gkv=mla_kv_norm_g[l][None],
        wq_nope=uq[:, :, :QK_NOPE].reshape(512, nh_b * QK_NOPE).astype(bf16),
        wq_pe=jnp.concatenate([pe, zpad], axis=2).reshape(512, nh_b * LANES).astype(bf16),
        wq_pr=jnp.concatenate([pe[:, :, perm], zpad], axis=2).reshape(512, nh_b * LANES).astype(bf16),
        wkv=jnp.concatenate([ukv[:, :, :QK_NOPE].reshape(KV_LORA, -1), ukv[:, :, QK_NOPE:].reshape(KV_LORA, -1)],
                            axis=1).astype(bf16))
    return dict(w_b=w_b, cols=cols, rwkv=rwkv, mla=mla)


def kernel(x, c, ctx, c_ctx, ada_w, ada_b, norm_mix_g, norm_mlp_g, w_in, rwkv_conv, rwkv_w0, rwkv_w_up, rwkv_a0, rwkv_a_up, rwkv_g_up, rwkv_k_k, rwkv_k_a, rwkv_r_k, rwkv_ln_g, rwkv_ln_b, mla_q_norm_g, mla_w_uq, mla_kv_norm_g, mla_w_ukv, na_rpb, w_branch, w_out, mlp_w1, mlp_w2, final_norm_g):
    batch, seq, d = x.shape
    n_ctx = ctx.shape[1]
    depth = ada_w.shape[0]
    assert batch == 1 and n_ctx == ROW_TILE and seq % ROW_TILE == 0 and seq // GRID_W >= WIN_ROWS
    n_ctx_tiles = n_ctx // ROW_TILE

    xs = jnp.concatenate([ctx[0], x[0]], axis=0)
    cond = jnp.concatenate([c, c_ctx[None], jnp.zeros((6, d), f32)], axis=0)
    mods = ada_modulation(cond, ada_w, ada_b)
    cs = _rope_tables(n_ctx, seq)
    na_bias = _na_bias_tables(na_rpb)
    w_branch_b, w_out_b = w_branch.astype(bf16), w_out.astype(bf16)
    mlp_w1_b, mlp_w2_b = mlp_w1.astype(bf16), mlp_w2.astype(bf16)
    aw = rwkv_k_k.shape[1]
    n_b = w_in.shape[2] - 3 * aw - 3 * aw - N_BRANCH * d
    w_main, w_small = cast_w_in_groups(w_in, 3 * aw, n_b)

    for l in range(depth):
        lw = _layer_weights(l, w_small, rwkv_conv, rwkv_w0, rwkv_w_up, rwkv_a0, rwkv_a_up, rwkv_g_up, rwkv_k_k,
                            rwkv_k_a, rwkv_r_k, mla_q_norm_g, mla_w_uq, mla_kv_norm_g, mla_w_ukv)
        mods8 = mods[l]
        mods2 = mods8[:2, None, :]
        h = norm_modulate(xs, norm_mix_g[l], mods2, 0)
        u3 = matmul(h, w_main, f32, 1024, layer=l, col0=0, ncols=3 * aw)
        ub = matmul(h, lw["w_b"], f32, lw["w_b"].shape[1])
        uc = matmul(h, w_main, bf16, 1024, layer=l, col0=3 * aw, ncols=3 * aw)
        gates = matmul(h, w_main, bf16, 1024, sigmoid=True, layer=l, col0=6 * aw, ncols=N_BRANCH * d)

        r, v, kk, bonus, g, lwd, kd, bd = rwkv_prepare(u3, ub, lw["rwkv"], n_ctx)
        y = rwkv_chunk_scan(r, v, kk, lwd, kd, bd, n_ctx)
        ya = rwkv_readout(y, bonus, g, rwkv_ln_g[l], rwkv_ln_b[l], lw["rwkv"]["seg"][:LANES, :LANES])

        q_b, k_b, v_b = mla_prepare(ub, cs, lw["mla"], lw["cols"])
        yb = mla_attention(q_b, k_b, v_b, n_ctx)

        yc = na_attention(uc, na_bias[l], n_ctx)

        merged = merge_branches(ya, yb, yc, gates, w_branch_b, l)
        xs, h2 = out_proj_residual_norm(merged, w_out_b, l, xs, mods8, 2, norm_mlp_g[l], 3, n_ctx)
        xs = mlp_gated_residual(h2, mlp_w1_b, mlp_w2_b, l, xs, mods8, 5, n_ctx)

    return final_norm(xs, final_norm_g, n_ctx_tiles)[None]
```
